```python
import jax
import jax.numpy as jnp
from jax import lax
import numpy as np


D_MODEL = 1024
BATCH = 2
SEQ = 16384
DEPTH = 4

GRID_W = 64
CTX_LEN = 256
HEAD_DIM = 64
GQA_HEADS = 8
GQA_KV_HEADS = 2
NA_HEADS = 8
NA_ROWS = 8
NA_COLS = 16
RET_HEADS = 4
RET_DK = 128
RET_DV = 128
RET_CHUNK = 128
MLA_HEADS = 4
MLA_Q_LORA = 256
MLA_KV_LORA = 128
MLA_NOPE = 128
MLA_ROPE = 64
MLA_V = 128
N_EXPERTS = 32
TOP_K = 4
D_FF = 1024
SWIGLU_LIMIT = 7.0
SWIGLU_ALPHA = 1.702
MOE_BLOCK = 256
Q_BLOCK = 128
ROPE_THETA = 10000.0
EPS = 1e-6
GN_EPS = 1e-5
N_EVEN = (DEPTH + 1) // 2
N_ODD = DEPTH // 2
EVEN_WIDTHS = (GQA_HEADS * HEAD_DIM, GQA_KV_HEADS * HEAD_DIM, GQA_KV_HEADS * HEAD_DIM, NA_HEADS * HEAD_DIM, NA_HEADS * HEAD_DIM, NA_HEADS * HEAD_DIM)
ODD_WIDTHS = (RET_HEADS * RET_DK, RET_HEADS * RET_DK, RET_HEADS * RET_DV, RET_HEADS * RET_DV, MLA_Q_LORA, MLA_KV_LORA, MLA_ROPE)
EVEN_OUT = (GQA_HEADS + NA_HEADS) * HEAD_DIM
ODD_OUT = RET_HEADS * RET_DV + MLA_HEADS * MLA_V
RPB_SIZE = (2 * NA_ROWS - 1) * (2 * NA_COLS - 1)

kernel_name = 'hybrid_dit_gqa_na_retnet_mla_moe'


def rms_norm(x, g):
    xf = x.astype(jnp.float32)
    y = xf * lax.rsqrt(jnp.mean(xf * xf, axis=-1, keepdims=True) + EPS)
    return (y * g.astype(jnp.float32)).astype(x.dtype)


def modulate(x, g, shift, scale):
    return rms_norm(x, g) * (1.0 + scale) + shift


def split_cols(p, widths):
    return jnp.split(p, [int(v) for v in np.cumsum(widths)[:-1]], axis=-1)


def heads(t, n):
    b, l, w = t.shape
    return t.reshape(b, l, n, w // n).transpose(0, 2, 1, 3)


def unheads(t):
    b, h, l, d = t.shape
    return t.transpose(0, 2, 1, 3).reshape(b, l, h * d)


def axial_rope(rows, cols, d_rot):
    n_freq = d_rot // 4
    inv = ROPE_THETA ** (-jnp.arange(n_freq, dtype=jnp.float32) / n_freq)
    ang = jnp.concatenate([rows[:, None] * inv, cols[:, None] * inv], axis=-1)
    return jnp.cos(ang), jnp.sin(ang)


def apply_rope(x, cos, sin):
    xf = x.astype(jnp.float32).reshape(x.shape[:-1] + (-1, 2))
    x1, x2 = xf[..., 0], xf[..., 1]
    out = jnp.stack([x1 * cos - x2 * sin, x1 * sin + x2 * cos], axis=-1)
    return out.reshape(x.shape).astype(x.dtype)


def blocked_attention(q, k, v):
    b, kvh, g, lq, dq = q.shape
    scale = dq ** -0.5
    nb = lq // Q_BLOCK
    qb = jnp.moveaxis(q.reshape(b, kvh, g, nb, Q_BLOCK, dq), 3, 0)

    def block(qs):
        s = jnp.einsum('bkgqd,bksd->bkgqs', qs, k, preferred_element_type=jnp.float32) * scale
        p = jax.nn.softmax(s, axis=-1).astype(v.dtype)
        return jnp.einsum('bkgqs,bksd->bkgqd', p, v)

    o = lax.map(block, qb)
    return jnp.moveaxis(o, 0, 3).reshape(b, kvh, g, lq, v.shape[-1])


def neighbourhood_tables(seq_len):
    rows_n = seq_len // GRID_W
    wr = min(NA_ROWS, rows_n)
    t = jnp.arange(seq_len)
    r, col = t // GRID_W, t % GRID_W
    r0 = jnp.clip(r - wr // 2, 0, rows_n - wr)
    c0 = jnp.clip(col - NA_COLS // 2, 0, GRID_W - NA_COLS)
    kr = r0[:, None, None] + jnp.arange(wr)[None, :, None]
    kc = c0[:, None, None] + jnp.arange(NA_COLS)[None, None, :]
    idx = (kr * GRID_W + kc).reshape(seq_len, wr * NA_COLS)
    rel = ((kr - r[:, None, None] + NA_ROWS - 1) * (2 * NA_COLS - 1)
           + (kc - col[:, None, None] + NA_COLS - 1)).reshape(seq_len, wr * NA_COLS)
    return idx, rel


def neighbourhood_attention(q, k, v, k_ctx, v_ctx, rpb, idx, rel):
    b, h, s, d = q.shape
    scale = d ** -0.5
    nb = s // Q_BLOCK
    w = idx.shape[1]
    qb = jnp.moveaxis(q.reshape(b, h, nb, Q_BLOCK, d), 2, 0)
    ib = idx.reshape(nb, Q_BLOCK, w)
    rb = rel.reshape(nb, Q_BLOCK, w)

    def block(inp):
        qs, ii, rr = inp
        kw = k[:, :, ii]
        vw = v[:, :, ii]
        s_win = jnp.einsum('bhqd,bhqwd->bhqw', qs, kw, preferred_element_type=jnp.float32) * scale
        s_win = s_win + rpb[:, rr][None].astype(jnp.float32)
        s_ctx = jnp.einsum('bhqd,bhcd->bhqc', qs, k_ctx, preferred_element_type=jnp.float32) * scale
        p = jax.nn.softmax(jnp.concatenate([s_win, s_ctx], axis=-1), axis=-1).astype(v.dtype)
        return (jnp.einsum('bhqw,bhqwd->bhqd', p[..., :w], vw)
                + jnp.einsum('bhqc,bhcd->bhqd', p[..., w:], v_ctx))

    o = lax.map(block, (qb, ib, rb))
    return jnp.moveaxis(o, 0, 2).reshape(b, h, s, d)


def retention_chunkwise(q, k, v, log_gamma, state0, inclusive):
    b, h, L, dk = q.shape
    dv = v.shape[-1]
    n = L // RET_CHUNK
    pos = jnp.arange(RET_CHUNK, dtype=jnp.float32)
    diff = pos[:, None] - pos[None, :]
    mask = (diff >= 0) if inclusive else (diff > 0)
    decay_in = jnp.where(mask, jnp.exp(log_gamma[:, None, None] * jnp.where(mask, diff, 0.0)), 0.0)
    q_decay = jnp.exp(log_gamma[:, None] * (pos + 1.0))[..., None]
    k_decay = jnp.exp(log_gamma[:, None] * (RET_CHUNK - 1.0 - pos))[..., None]
    chunk_decay = jnp.exp(log_gamma * RET_CHUNK)[:, None, None]

    def chunks(t):
        return jnp.moveaxis(t.astype(jnp.float32).reshape(b, h, n, RET_CHUNK, t.shape[-1]), 2, 0)

    def step(state, inp):
        qc, kc, vc = inp
        inner = jnp.einsum('bhts,bhsv->bhtv', jnp.einsum('bhtd,bhsd->bhts', qc, kc) * decay_in, vc)
        cross = jnp.einsum('bhtd,bhdv->bhtv', qc * q_decay, state)
        new_state = state * chunk_decay + jnp.einsum('bhsd,bhsv->bhdv', kc * k_decay, vc)
        return new_state, inner + cross

    final, y = lax.scan(step, state0, (chunks(q), chunks(k), chunks(v)))
    return jnp.moveaxis(y, 0, 2).reshape(b, h, L, dv), final


def head_group_norm(y):
    mu = jnp.mean(y, axis=-1, keepdims=True)
    var = jnp.mean(jnp.square(y - mu), axis=-1, keepdims=True)
    return (y - mu) * lax.rsqrt(var + GN_EPS)


def mixer_even(h_lat, h_ctx, w_in, w_out, a_qn, a_kn, b_qn, b_kn, rpb, rope, nbr_idx, nbr_rel, need_ctx):
    def project(h):
        qa, ka, va, qb, kb, vb = split_cols(h @ w_in, EVEN_WIDTHS)
        return (rms_norm(heads(qa, GQA_HEADS), a_qn), rms_norm(heads(ka, GQA_KV_HEADS), a_kn), heads(va, GQA_KV_HEADS),
                rms_norm(heads(qb, NA_HEADS), b_qn), rms_norm(heads(kb, NA_HEADS), b_kn), heads(vb, NA_HEADS))

    qa, ka, va, qb, kb, vb = project(h_lat)
    qa_c, ka_c, va_c, qb_c, kb_c, vb_c = project(h_ctx)
    cos, sin = rope
    qa = apply_rope(qa, cos, sin)
    ka = apply_rope(ka, cos, sin)
    b, _, s, _ = qa.shape
    grp = GQA_HEADS // GQA_KV_HEADS
    oa = blocked_attention(qa.reshape(b, GQA_KV_HEADS, grp, s, HEAD_DIM),
                           jnp.concatenate([ka, ka_c], axis=2), jnp.concatenate([va, va_c], axis=2))
    ob = neighbourhood_attention(qb, kb, vb, kb_c, vb_c, rpb, nbr_idx, nbr_rel)
    o_lat = jnp.concatenate([unheads(oa.reshape(b, GQA_HEADS, s, HEAD_DIM)), unheads(ob)], axis=-1) @ w_out
    o_ctx = None
    if need_ctx:
        cl = qa_c.shape[2]
        oa_c = blocked_attention(qa_c.reshape(b, GQA_KV_HEADS, grp, cl, HEAD_DIM), ka_c, va_c)
        ob_c = blocked_attention(qb_c[:, :, None], kb_c, vb_c)
        o_ctx = jnp.concatenate([unheads(oa_c.reshape(b, GQA_HEADS, cl, HEAD_DIM)), unheads(ob_c[:, :, 0])], axis=-1) @ w_out
    return o_lat, o_ctx


def mixer_odd(h_lat, h_ctx, w_in, w_out, decay_logit, gn_g, cq_n, ckv_n, w_uq, w_ukv, q_n, k_n, rope_c, rope_d, need_ctx):
    def project(h):
        rq, rk, rv, rg, cq, ckv, kr = split_cols(h @ w_in, ODD_WIDTHS)
        b, L, _ = h.shape
        rq = heads(rq, RET_HEADS) * (RET_DK ** -0.5)
        rk = heads(rk, RET_HEADS)
        rv = heads(rv, RET_HEADS)
        q = heads(rms_norm(cq, cq_n) @ w_uq, MLA_HEADS)
        kv = heads(rms_norm(ckv, ckv_n) @ w_ukv, MLA_HEADS)
        k_rope = jnp.broadcast_to(kr[:, None], (b, MLA_HEADS, L, MLA_ROPE))
        k = jnp.concatenate([kv[..., :MLA_NOPE], k_rope], axis=-1)
        return rq, rk, rv, rg, rms_norm(q, q_n), rms_norm(k, k_n), kv[..., MLA_NOPE:]

    rq, rk, rv, rg, q, k, v = project(h_lat)
    rq_c, rk_c, rv_c, rg_c, q_c, k_c, v_c = project(h_ctx)
    cos_c, sin_c = rope_c
    cos_d, sin_d = rope_d
    rq = apply_rope(rq, cos_c, sin_c)
    rk = apply_rope(rk, cos_c, sin_c)
    q = jnp.concatenate([q[..., :MLA_NOPE], apply_rope(q[..., MLA_NOPE:], cos_d, sin_d)], axis=-1)
    k = jnp.concatenate([k[..., :MLA_NOPE], apply_rope(k[..., MLA_NOPE:], cos_d, sin_d)], axis=-1)

    log_g = jax.nn.log_sigmoid(decay_logit.astype(jnp.float32))
    b = h_lat.shape[0]
    s0 = jnp.zeros((b, RET_HEADS, RET_DK, RET_DV), jnp.float32)
    flip = lambda t: jnp.flip(t, axis=2)
    yf_c, st_f = retention_chunkwise(rq_c, rk_c, rv_c, log_g[0], s0, True)
    yb_c, st_b = retention_chunkwise(flip(rq_c), flip(rk_c), flip(rv_c), log_g[1], s0, False)
    yf, _ = retention_chunkwise(rq, rk, rv, log_g[0], st_f, True)
    yb, _ = retention_chunkwise(flip(rq), flip(rk), flip(rv), log_g[1], st_b, False)

    def retention_out(y, gate):
        return (unheads(head_group_norm(y)) * gn_g * jax.nn.silu(gate.astype(jnp.float32))).astype(gate.dtype)

    def mla(qq, kk, vv):
        return unheads(blocked_attention(qq[:, :, None], kk, vv)[:, :, 0])

    o_lat = jnp.concatenate([retention_out(yf + flip(yb), rg),
                             mla(q, jnp.concatenate([k, k_c], axis=2), jnp.concatenate([v, v_c], axis=2))], axis=-1) @ w_out
    o_ctx = None
    if need_ctx:
        o_ctx = jnp.concatenate([retention_out(yf_c + flip(yb_c), rg_c), mla(q_c, k_c, v_c)], axis=-1) @ w_out
    return o_lat, o_ctx


def moe(h, w_r, b_r, w1, b1, w2, b2):
    n, d = h.shape
    logits = (h @ w_r).astype(jnp.float32) + b_r.astype(jnp.float32)
    top_v, top_e = lax.top_k(logits, TOP_K)
    gates = jax.nn.softmax(top_v, axis=-1)
    nk = n * TOP_K
    flat_e = top_e.reshape(nk)
    flat_tok = jnp.repeat(jnp.arange(n, dtype=jnp.int32), TOP_K)
    flat_g = gates.reshape(nk)
    order = jnp.argsort(flat_e)
    se, stok, sg = flat_e[order], flat_tok[order], flat_g[order]
    counts = jnp.bincount(flat_e, length=N_EXPERTS)
    padded = ((counts + MOE_BLOCK - 1) // MOE_BLOCK) * MOE_BLOCK
    start = jnp.cumsum(counts) - counts
    pend = jnp.cumsum(padded)
    pstart = pend - padded
    dest = pstart[se] + (jnp.arange(nk) - start[se])
    cap = ((nk + N_EXPERTS * (MOE_BLOCK - 1)) // MOE_BLOCK + 1) * MOE_BLOCK
    n_blk = cap // MOE_BLOCK
    buf_tok = jnp.full((cap,), n, jnp.int32).at[dest].set(stok)
    buf_g = jnp.zeros((cap,), jnp.float32).at[dest].set(sg)
    blk_e = jnp.clip(jnp.searchsorted(pend, jnp.arange(n_blk) * MOE_BLOCK, side='right'), 0, N_EXPERTS - 1)
    h_pad = jnp.concatenate([h, jnp.zeros((1, d), h.dtype)], axis=0)

    def block(inp):
        tok, g, e = inp
        u = h_pad[tok] @ w1[e] + b1[e]
        gl, up = jnp.split(u.astype(jnp.float32), 2, axis=-1)
        gl = jnp.minimum(gl, SWIGLU_LIMIT)
        up = jnp.clip(up, -SWIGLU_LIMIT, SWIGLU_LIMIT)
        act = (gl * jax.nn.sigmoid(SWIGLU_ALPHA * gl) * (up + 1.0)).astype(h.dtype)
        return ((act @ w2[e] + b2[e]).astype(jnp.float32) * g[:, None]).astype(h.dtype)

    out = lax.map(block, (buf_tok.reshape(n_blk, MOE_BLOCK), buf_g.reshape(n_blk, MOE_BLOCK), blk_e))
    y = jnp.zeros((n + 1, d), h.dtype).at[buf_tok].add(out.reshape(cap, d))
    return y[:n]


def setup_inputs(seed: int = 0) -> dict:
    key = jax.random.key(seed)
    ks = jax.random.split(key, 32)
    f32 = jnp.float32
    D = D_MODEL

    def nrm(k, shape, scale):
        return jax.random.normal(k, shape, f32) * scale

    def gain(k, shape):
        return 1.0 + 0.02 * jax.random.normal(k, shape, f32)

    e_in = sum(EVEN_WIDTHS)
    o_in = sum(ODD_WIDTHS)
    kexp = 5.0 + jnp.arange(RET_HEADS, dtype=f32)
    base_logit = jnp.log(2.0 ** kexp - 1.0)
    return {
        'x': nrm(ks[0], (BATCH, SEQ, D), 1.0),
        'c': nrm(ks[1], (BATCH, D), 1.0),
        'ctx': nrm(ks[2], (BATCH, CTX_LEN, D), 1.0),
        'c_ctx': nrm(ks[3], (D,), 1.0),
        'norm1_g': gain(ks[4], (DEPTH, D)),
        'norm2_g': gain(ks[5], (DEPTH, D)),
        'w_mod': nrm(ks[6], (DEPTH, D, 6 * D), 0.5 * D ** -0.5),
        'b_mod': nrm(ks[7], (DEPTH, 6 * D), 0.01),
        'w_in_even': nrm(ks[8], (N_EVEN, D, e_in), D ** -0.5),
        'w_out_even': nrm(ks[9], (N_EVEN, EVEN_OUT, D), EVEN_OUT ** -0.5),
        'a_q_norm': gain(ks[10], (N_EVEN, HEAD_DIM)),
        'a_k_norm': gain(ks[11], (N_EVEN, HEAD_DIM)),
        'b_q_norm': gain(ks[12], (N_EVEN, HEAD_DIM)),
        'b_k_norm': gain(ks[13], (N_EVEN, HEAD_DIM)),
        'b_rpb': nrm(ks[14], (N_EVEN, NA_HEADS, RPB_SIZE), 0.1),
        'w_in_odd': nrm(ks[15], (N_ODD, D, o_in), D ** -0.5),
        'w_out_odd': nrm(ks[16], (N_ODD, ODD_OUT, D), ODD_OUT ** -0.5),
        'ret_decay': base_logit[None, None, :] + nrm(ks[17], (N_ODD, 2, RET_HEADS), 0.1),
        'ret_gn': gain(ks[18], (N_ODD, RET_HEADS * RET_DV)),
        'mla_cq_norm': gain(ks[19], (N_ODD, MLA_Q_LORA)),
        'mla_ckv_norm': gain(ks[20], (N_ODD, MLA_KV_LORA)),
        'w_uq': nrm(ks[21], (N_ODD, MLA_Q_LORA, MLA_HEADS * (MLA_NOPE + MLA_ROPE)), MLA_Q_LORA ** -0.5),
        'w_ukv': nrm(ks[22], (N_ODD, MLA_KV_LORA, MLA_HEADS * (MLA_NOPE + MLA_V)), MLA_KV_LORA ** -0.5),
        'mla_q_norm': gain(ks[23], (N_ODD, MLA_NOPE + MLA_ROPE)),
        'mla_k_norm': gain(ks[24], (N_ODD, MLA_NOPE + MLA_ROPE)),
        'w_router': nrm(ks[25], (DEPTH, D, N_EXPERTS), D ** -0.5),
        'b_router': nrm(ks[26], (DEPTH, N_EXPERTS), 0.01),
        'w_exp1': nrm(ks[27], (DEPTH, N_EXPERTS, D, 2 * D_FF), D ** -0.5),
        'b_exp1': nrm(ks[28], (DEPTH, N_EXPERTS, 2 * D_FF), 0.01),
        'w_exp2': nrm(ks[29], (DEPTH, N_EXPERTS, D_FF, D), D_FF ** -0.5),
        'b_exp2': nrm(ks[30], (DEPTH, N_EXPERTS, D), 0.01),
    }


def reference(x, c, ctx, c_ctx, norm1_g, norm2_g, w_mod, b_mod, w_in_even, w_out_even, a_q_norm, a_k_norm,
              b_q_norm, b_k_norm, b_rpb, w_in_odd, w_out_odd, ret_decay, ret_gn, mla_cq_norm, mla_ckv_norm,
              w_uq, w_ukv, mla_q_norm, mla_k_norm, w_router, b_router, w_exp1, b_exp1, w_exp2, b_exp2):
    bsz, s, d = x.shape
    cl = ctx.shape[1]
    t = jnp.arange(s)
    rows = (t // GRID_W).astype(jnp.float32)
    cols = (t % GRID_W).astype(jnp.float32)
    rope_a = axial_rope(rows, cols, HEAD_DIM)
    rope_c = axial_rope(rows, cols, RET_DK)
    rope_d = axial_rope(rows, cols, MLA_ROPE)
    nbr_idx, nbr_rel = neighbourhood_tables(s)
    silu_c = jax.nn.silu(c)
    silu_cc = jax.nn.silu(c_ctx)
    x_lat, x_ctx = x, ctx
    for l in range(DEPTH):
        need_ctx = l < DEPTH - 1
        sh1, sc1, g1, sh2, sc2, g2 = jnp.split((silu_c @ w_mod[l] + b_mod[l])[:, None, :], 6, axis=-1)
        csh1, csc1, cg1, csh2, csc2, cg2 = jnp.split(silu_cc @ w_mod[l] + b_mod[l], 6, axis=-1)
        a_lat = modulate(x_lat, norm1_g[l], sh1, sc1)
        a_ctx = modulate(x_ctx, norm1_g[l], csh1, csc1)
        i = l // 2
        if l % 2 == 0:
            o_lat, o_ctx = mixer_even(a_lat, a_ctx, w_in_even[i], w_out_even[i], a_q_norm[i], a_k_norm[i],
                                      b_q_norm[i], b_k_norm[i], b_rpb[i], rope_a, nbr_idx, nbr_rel, need_ctx)
        else:
            o_lat, o_ctx = mixer_odd(a_lat, a_ctx, w_in_odd[i], w_out_odd[i], ret_decay[i], ret_gn[i],
                                     mla_cq_norm[i], mla_ckv_norm[i], w_uq[i], w_ukv[i], mla_q_norm[i],
                                     mla_k_norm[i], rope_c, rope_d, need_ctx)
        x_lat = x_lat + g1 * o_lat
        m_lat = modulate(x_lat, norm2_g[l], sh2, sc2).reshape(bsz * s, d)
        if need_ctx:
            x_ctx = x_ctx + cg1 * o_ctx
            m_ctx = modulate(x_ctx, norm2_g[l], csh2, csc2).reshape(bsz * cl, d)
            y = moe(jnp.concatenate([m_lat, m_ctx], axis=0), w_router[l], b_router[l], w_exp1[l], b_exp1[l], w_exp2[l], b_exp2[l])
            x_lat = x_lat + g2 * y[:bsz * s].reshape(bsz, s, d)
            x_ctx = x_ctx + cg2 * y[bsz * s:].reshape(bsz, cl, d)
        else:
            y = moe(m_lat, w_router[l], b_router[l], w_exp1[l], b_exp1[l], w_exp2[l], b_exp2[l])
            x_lat = x_lat + g2 * y.reshape(bsz, s, d)
    return x_lat
```

```python
import functools
import math

import numpy as np
import jax
import jax.numpy as jnp
from jax import lax
from jax.experimental import pallas as pl
from jax.experimental.pallas import tpu as pltpu

F32 = jnp.float32
BF16 = jnp.bfloat16

GRID_W = 64
HEAD_DIM = 64
GQA_HEADS = 8
GQA_KV_HEADS = 2
NA_HEADS = 8
NA_ROWS = 8
NA_COLS = 16
RET_HEADS = 4
RET_DK = 128
MLA_HEADS = 4
MLA_Q_LORA = 256
MLA_KV_LORA = 128
MLA_NOPE = 128
MLA_ROPE = 64
MLA_V = 128
N_EXPERTS = 32
TOP_K = 4
SWIGLU_LIMIT = 7.0
SWIGLU_ALPHA = 1.702
ROPE_THETA = 10000.0
EPS = 1e-6
GN_EPS = 1e-5
LOG2E = math.log2(math.e)
NEG = -1e30

LANES = 128
ROW_TILE = 512
Q_SUB = 256
NA_QB = 128
NA_WIN_ROWS = 10
RET_CHUNK = 256
MOE_BM = 512
COPY_CHUNK = 1024
VMEM_LIMIT = 56 * 1024 * 1024


def _cp(sem):
    return pltpu.CompilerParams(dimension_semantics=sem, vmem_limit_bytes=VMEM_LIMIT)


def _mod_kernel(c_ref, w_ref, b_ref, o_ref):
    c = c_ref[...]
    s = c * jax.nn.sigmoid(c)
    o_ref[0] = jnp.dot(s, w_ref[0], precision=lax.Precision.HIGHEST, preferred_element_type=F32) + b_ref[0]


def _mod_vectors(c_rows, w_mod, b_mod):
    depth, d, d6 = w_mod.shape
    tn = 1536
    return pl.pallas_call(
        _mod_kernel,
        grid=(depth, d6 // tn),
        in_specs=[pl.BlockSpec((8, d), lambda l, j: (0, 0)),
                  pl.BlockSpec((1, d, tn), lambda l, j: (l, 0, j)),
                  pl.BlockSpec((1, 1, tn), lambda l, j: (l, 0, j))],
        out_specs=pl.BlockSpec((1, 8, tn), lambda l, j: (l, 0, j)),
        out_shape=jax.ShapeDtypeStruct((depth, 8, d6), F32),
        compiler_params=_cp(("parallel", "parallel")),
    )(c_rows, w_mod, b_mod.reshape(depth, 1, d6))


def _modulated_norm(x, g, sc, sh):
    ms = jnp.mean(x * x, axis=-1, keepdims=True)
    return x * lax.rsqrt(ms + EPS) * g * (1.0 + sc) + sh


def _pair_rope(y, cos, sin_signed):
    lane = lax.broadcasted_iota(jnp.int32, y.shape, 1)
    partner = jnp.where((lane & 1) == 0, pltpu.roll(y, LANES - 1, 1), pltpu.roll(y, 1, 1))
    return y * cos + partner * sin_signed


def _split_dot(a_f32, w_bf16):
    hi = a_f32.astype(BF16)
    lo = (a_f32 - hi.astype(F32)).astype(BF16)
    return (jnp.dot(hi, w_bf16, preferred_element_type=F32) + jnp.dot(lo, w_bf16, preferred_element_type=F32))


def _proj_kernel(x_ref, sh_ref, sc_ref, g_ref, w_ref, gain_ref, bd_ref, cos_ref, sin_ref, *out_refs, segs):
    a = _modulated_norm(x_ref[...], g_ref[...], sc_ref[0], sh_ref[0]).astype(BF16)
    cos = cos_ref[...]
    sin = sin_ref[...]
    for (start, width, mode, scale), o_ref in zip(segs, out_refs):
        y_seg = jnp.dot(a, w_ref[:, start:start + width], preferred_element_type=F32)
        if mode == "plain":
            o_ref[...] = y_seg.astype(o_ref.dtype)
            continue
        for j in range(width // LANES):
            y = y_seg[:, j * LANES:(j + 1) * LANES]
            if "norm" in mode:
                ms = _split_dot(y * y, bd_ref[...])
                y = y * lax.rsqrt(ms + EPS) * gain_ref[:, start + j * LANES:start + (j + 1) * LANES]
            if scale != 1.0:
                y = y * scale
            if "rope" in mode:
                y = _pair_rope(y, cos, sin)
            o_ref[:, j * LANES:(j + 1) * LANES] = y.astype(o_ref.dtype)


def _proj(x_all, mod_l, norm_g, w, gain, cos_t, sin_t, segs, out_dtypes, n_lat_tiles_per_batch, n_batch):
    n, d = x_all.shape
    tm = ROW_TILE
    n_tiles = n // tm
    wtot = w.shape[1]
    tpb = n_lat_tiles_per_batch
    n_lat_tiles = tpb * n_batch

    def mod_row(t):
        return jnp.where(t < n_lat_tiles, t // tpb, n_batch)

    def rope_row(t):
        return jnp.where(t < n_lat_tiles, t % tpb, tpb)

    bd = np.kron(np.eye(2, dtype=np.float32), np.full((HEAD_DIM, HEAD_DIM), 1.0 / HEAD_DIM, np.float32))
    in_specs = [
        pl.BlockSpec((tm, d), lambda t: (t, 0)),
        pl.BlockSpec((1, 1, d), lambda t: (mod_row(t), 0, 0)),
        pl.BlockSpec((1, 1, d), lambda t: (mod_row(t), 0, 1)),
        pl.BlockSpec((1, d), lambda t: (0, 0)),
        pl.BlockSpec((d, wtot), lambda t: (0, 0)),
        pl.BlockSpec((1, wtot), lambda t: (0, 0)),
        pl.BlockSpec((LANES, LANES), lambda t: (0, 0)),
        pl.BlockSpec((tm, LANES), lambda t: (rope_row(t), 0)),
        pl.BlockSpec((tm, LANES), lambda t: (rope_row(t), 0)),
    ]
    out_specs = [pl.BlockSpec((tm, s[1]), lambda t: (t, 0)) for s in segs]
    out_shape = [jax.ShapeDtypeStruct((n, s[1]), dt) for s, dt in zip(segs, out_dtypes)]
    return pl.pallas_call(
        functools.partial(_proj_kernel, segs=tuple(segs)),
        grid=(n_tiles,),
        in_specs=in_specs,
        out_specs=out_specs,
        out_shape=out_shape,
        compiler_params=_cp(("parallel",)),
    )(x_all, mod_l, mod_l, norm_g.reshape(1, d), w, gain, jnp.asarray(bd, BF16), cos_t, sin_t)


def _mla_proj_kernel(x_ref, wuq_ref, wukv_ref, gcq_ref, gckv_ref, gq_ref, gkn_ref, gkr_ref, cos_ref, sin_ref,
                     q_ref, k_ref, v_ref):
    x = x_ref[...]
    cq = x[:, :MLA_Q_LORA]
    ckv = x[:, MLA_Q_LORA:MLA_Q_LORA + MLA_KV_LORA]
    kr = x[:, MLA_Q_LORA + MLA_KV_LORA:MLA_Q_LORA + MLA_KV_LORA + LANES]
    cos = cos_ref[...]
    sin = sin_ref[...]
    cqn = cq * lax.rsqrt(jnp.mean(cq * cq, axis=-1, keepdims=True) + EPS) * gcq_ref[...]
    ckvn = ckv * lax.rsqrt(jnp.mean(ckv * ckv, axis=-1, keepdims=True) + EPS) * gckv_ref[...]
    q = jnp.dot(cqn.astype(BF16), wuq_ref[...], preferred_element_type=F32)
    kv = jnp.dot(ckvn.astype(BF16), wukv_ref[...], preferred_element_type=F32)
    nh = MLA_HEADS
    d_qk = float(MLA_NOPE + MLA_ROPE)
    lane = lax.broadcasted_iota(jnp.int32, (1, LANES), 1)
    low = lane < MLA_ROPE

    def half_sums(slab):
        sq = slab * slab
        a = jnp.sum(jnp.where(low, sq, 0.0), axis=-1, keepdims=True)
        return a, jnp.sum(sq, axis=-1, keepdims=True) - a

    rope_w = nh * MLA_NOPE
    q_rope_ss = []
    for r in range(nh // 2):
        q_rope_ss.extend(half_sums(q[:, rope_w + r * LANES:rope_w + (r + 1) * LANES]))
    kr_ss, _ = half_sums(kr)
    rs_q, rs_k = [], []
    for h in range(nh):
        qn = q[:, h * MLA_NOPE:(h + 1) * MLA_NOPE]
        kn = kv[:, h * MLA_NOPE:(h + 1) * MLA_NOPE]
        rs_q.append(lax.rsqrt((jnp.sum(qn * qn, axis=-1, keepdims=True) + q_rope_ss[h]) / d_qk + EPS))
        rs_k.append(lax.rsqrt((jnp.sum(kn * kn, axis=-1, keepdims=True) + kr_ss) / d_qk + EPS))
        q_ref[:, h * MLA_NOPE:(h + 1) * MLA_NOPE] = (qn * rs_q[h] * gq_ref[:, h * MLA_NOPE:(h + 1) * MLA_NOPE]).astype(q_ref.dtype)
        k_ref[:, h * MLA_NOPE:(h + 1) * MLA_NOPE] = (kn * rs_k[h] * gkn_ref[...]).astype(k_ref.dtype)
    kr_rot = _pair_rope(kr * gkr_ref[...], cos, sin)
    for r in range(nh // 2):
        sl = slice(rope_w + r * LANES, rope_w + (r + 1) * LANES)
        yq = q[:, sl] * jnp.where(low, rs_q[2 * r], rs_q[2 * r + 1]) * gq_ref[:, sl]
        q_ref[:, sl] = _pair_rope(yq, cos, sin).astype(q_ref.dtype)
        k_ref[:, sl] = (kr_rot * jnp.where(low, rs_k[2 * r], rs_k[2 * r + 1])).astype(k_ref.dtype)
    v_ref[...] = kv[:, nh * MLA_NOPE:].astype(v_ref.dtype)


def _mla_proj(mla_in, wuq, wukv, gcq, gckv, gq, gkn, gkr, cos_t, sin_t, tpb, n_batch):
    n, win = mla_in.shape
    tm = ROW_TILE
    n_lat_tiles = tpb * n_batch

    def rope_row(t):
        return jnp.where(t < n_lat_tiles, t % tpb, tpb)

    qk_w = MLA_HEADS * (MLA_NOPE + MLA_ROPE)
    v_w = MLA_HEADS * MLA_V
    full = lambda a: pl.BlockSpec(a.shape, lambda t: (0,) * a.ndim)
    return pl.pallas_call(
        _mla_proj_kernel,
        grid=(n // tm,),
        in_specs=[pl.BlockSpec((tm, win), lambda t: (t, 0)), full(wuq), full(wukv), full(gcq), full(gckv),
                  full(gq), full(gkn), full(gkr),
                  pl.BlockSpec((tm, LANES), lambda t: (rope_row(t), 0)),
                  pl.BlockSpec((tm, LANES), lambda t: (rope_row(t), 0))],
        out_specs=[pl.BlockSpec((tm, qk_w), lambda t: (t, 0)), pl.BlockSpec((tm, qk_w), lambda t: (t, 0)),
                   pl.BlockSpec((tm, v_w), lambda t: (t, 0))],
        out_shape=[jax.ShapeDtypeStruct((n, qk_w), BF16), jax.ShapeDtypeStruct((n, qk_w), BF16),
                   jax.ShapeDtypeStruct((n, v_w), BF16)],
        compiler_params=_cp(("parallel",)),
    )(mla_in, wuq, wukv, gcq, gckv, gq, gkn, gkr, cos_t, sin_t)


def _flash_kernel(qt_ref, k_ref, vt_ref, ot_ref, m_sc, l_sc, acc_sc, *, chains, n_chunks):
    for c in range(len(chains)):
        m_sc[c] = jnp.full(m_sc.shape[1:], NEG, F32)
        l_sc[c] = jnp.zeros(l_sc.shape[1:], F32)
        acc_sc[c] = jnp.zeros(acc_sc.shape[1:], F32)

    def chunk(ci, carry):
        kc = k_ref[0, ci]
        vc = vt_ref[0, ci]
        for c, (g, off) in enumerate(chains):
            s = jnp.dot(kc, qt_ref[0, g, :, off:off + Q_SUB], preferred_element_type=F32)
            m_old = m_sc[c]
            m_new = jnp.maximum(m_old, jnp.max(s, axis=0, keepdims=True))
            alpha = jnp.exp2(m_old - m_new)
            p = jnp.exp2(s - m_new)
            l_sc[c] = alpha * l_sc[c] + jnp.sum(p, axis=0, keepdims=True)
            acc_sc[c] = alpha * acc_sc[c] + jnp.dot(vc, p.astype(BF16), preferred_element_type=F32)
            m_sc[c] = m_new
        return carry

    lax.fori_loop(0, n_chunks, chunk, 0)
    for c, (g, off) in enumerate(chains):
        ot_ref[0, g, :, off:off + Q_SUB] = (acc_sc[c] / l_sc[c]).astype(ot_ref.dtype)


def _key_chunk(lk):
    for tk in (1280, 1024, 768, 512, 256):
        if lk % tk == 0:
            return tk
    raise ValueError(f"key length {lk} must be a multiple of 256")


def _flash(qt, k, v, tq_blk):
    bk, g, dq, lq = qt.shape
    lk, dv = v.shape[1], v.shape[2]
    tk = _key_chunk(lk)
    nch = lk // tk
    kc = k.reshape(bk, nch, tk, dq)
    vt = v.reshape(bk, nch, tk, dv).transpose(0, 1, 3, 2)
    tq_blk = min(tq_blk, lq)
    assert lq % tq_blk == 0 and tq_blk % Q_SUB == 0
    chains = tuple((gi, j * Q_SUB) for gi in range(g) for j in range(tq_blk // Q_SUB))
    nc = len(chains)
    return pl.pallas_call(
        functools.partial(_flash_kernel, chains=chains, n_chunks=nch),
        grid=(bk, lq // tq_blk),
        in_specs=[pl.BlockSpec((1, g, dq, tq_blk), lambda b, i: (b, 0, 0, i)),
                  pl.BlockSpec((1, nch, tk, dq), lambda b, i: (b, 0, 0, 0)),
                  pl.BlockSpec((1, nch, dv, tk), lambda b, i: (b, 0, 0, 0))],
        out_specs=pl.BlockSpec((1, g, dv, tq_blk), lambda b, i: (b, 0, 0, i)),
        out_shape=jax.ShapeDtypeStruct((bk, g, dv, lq), BF16),
        scratch_shapes=[pltpu.VMEM((nc, 1, Q_SUB), F32), pltpu.VMEM((nc, 1, Q_SUB), F32),
                        pltpu.VMEM((nc, dv, Q_SUB), F32)],
        compiler_params=_cp(("parallel", "parallel")),
    )(qt, kc, vt)


def _na_tables(seq_len):
    rows_n = seq_len // GRID_W
    assert rows_n >= NA_WIN_ROWS and NA_ROWS <= rows_n
    nb = seq_len // NA_QB
    rpq = NA_QB // GRID_W
    band = NA_WIN_ROWS * GRID_W
    variants, var_id, bases = {}, [], []
    for j in range(nb):
        base = int(np.clip(rpq * j - NA_ROWS // 2, 0, rows_n - NA_WIN_ROWS))
        bases.append(base)
        t = np.arange(NA_QB) + j * NA_QB
        r, col = t // GRID_W, t % GRID_W
        r0 = np.clip(r - NA_ROWS // 2, 0, rows_n - NA_ROWS)
        c0 = np.clip(col - NA_COLS // 2, 0, GRID_W - NA_COLS)
        kk = np.arange(band)
        kr = base + kk // GRID_W
        kc = kk % GRID_W
        inside = ((kr[None] >= r0[:, None]) & (kr[None] < r0[:, None] + NA_ROWS)
                  & (kc[None] >= c0[:, None]) & (kc[None] < c0[:, None] + NA_COLS))
        rel = (kr[None] - r[:, None] + NA_ROWS - 1) * (2 * NA_COLS - 1) + (kc[None] - col[:, None] + NA_COLS - 1)
        tab = np.where(inside, rel, -1).astype(np.int32)
        assert (inside.sum(axis=1) == NA_ROWS * NA_COLS).all()
        key = tab.tobytes()
        if key not in variants:
            variants[key] = (len(variants), tab)
        var_id.append(variants[key][0])
    tabs = np.stack([v[1] for v in sorted(variants.values(), key=lambda kv: kv[0])])
    return np.asarray(bases, np.int32), np.asarray(var_id, np.int32), tabs


def _na_kernel(base_ref, var_ref, q_ref, k_ref, v_ref, kc_ref, vc_ref, bias_ref, o_ref):
    j = pl.program_id(1)
    band = NA_WIN_ROWS * GRID_W
    start = pl.multiple_of(base_ref[j] * GRID_W, GRID_W)
    q = q_ref[0]
    kw = k_ref[0, pl.ds(start, band), :]
    vw = v_ref[0, pl.ds(start, band), :]
    nt = (((1,), (1,)), ((), ()))
    s_win = lax.dot_general(q, kw, nt, preferred_element_type=F32) + bias_ref[0, 0]
    s_ctx = lax.dot_general(q, kc_ref[0], nt, preferred_element_type=F32)
    m = jnp.maximum(jnp.max(s_win, axis=-1, keepdims=True), jnp.max(s_ctx, axis=-1, keepdims=True))
    p_win = jnp.exp2(s_win - m)
    p_ctx = jnp.exp2(s_ctx - m)
    l = jnp.sum(p_win, axis=-1, keepdims=True) + jnp.sum(p_ctx, axis=-1, keepdims=True)
    o = (jnp.dot(p_win.astype(BF16), vw, preferred_element_type=F32)
         + jnp.dot(p_ctx.astype(BF16), vc_ref[0], preferred_element_type=F32))
    o_ref[0] = (o / l).astype(o_ref.dtype)


def _na_attention(q, k, v, k_ctx, v_ctx, bias, bases, var_id, n_heads):
    bh, s, d = q.shape
    cl = k_ctx.shape[1]
    band = NA_WIN_ROWS * GRID_W
    nb = s // NA_QB
    grid_spec = pltpu.PrefetchScalarGridSpec(
        num_scalar_prefetch=2,
        grid=(bh, nb),
        in_specs=[pl.BlockSpec((1, NA_QB, d), lambda b, j, bs, vr: (b, j, 0)),
                  pl.BlockSpec((1, s, d), lambda b, j, bs, vr: (b, 0, 0)),
                  pl.BlockSpec((1, s, d), lambda b, j, bs, vr: (b, 0, 0)),
                  pl.BlockSpec((1, cl, d), lambda b, j, bs, vr: (b, 0, 0)),
                  pl.BlockSpec((1, cl, d), lambda b, j, bs, vr: (b, 0, 0)),
                  pl.BlockSpec((1, 1, NA_QB, band), lambda b, j, bs, vr: (vr[j], b % n_heads, 0, 0))],
        out_specs=pl.BlockSpec((1, NA_QB, d), lambda b, j, bs, vr: (b, j, 0)),
    )
    return pl.pallas_call(
        _na_kernel,
        grid_spec=grid_spec,
        out_shape=jax.ShapeDtypeStruct((bh, s, d), BF16),
        compiler_params=_cp(("parallel", "arbitrary")),
    )(bases, var_id, q, k, v, k_ctx, v_ctx, bias)


def _ret_kernel(cdec_ref, qf_ref, kf_ref, vf_ref, qb_ref, kb_ref, vb_ref, dmask_ref, qdec_ref, kdec_ref,
                yf_ref, yb_ref, state_sc):
    @pl.when(pl.program_id(1) == 0)
    def _():
        state_sc[...] = jnp.zeros(state_sc.shape, F32)

    nt = (((1,), (1,)), ((), ()))
    tn = (((0,), (0,)), ((), ()))
    dk = RET_DK
    for d, (q_ref, k_ref, v_ref, y_ref) in enumerate(((qf_ref, kf_ref, vf_ref, yf_ref),
                                                      (qb_ref, kb_ref, vb_ref, yb_ref))):
        for h in range(RET_HEADS):
            sl = slice(h * dk, (h + 1) * dk)
            q = q_ref[:, sl]
            k = k_ref[:, sl]
            v = v_ref[:, sl]
            st = state_sc[d, h]
            a = lax.dot_general(q, k, nt, preferred_element_type=F32) * dmask_ref[d, h]
            inner = jnp.dot(a.astype(BF16), v, preferred_element_type=F32)
            cross = jnp.dot(q, st.astype(BF16), preferred_element_type=F32) * qdec_ref[d, h]
            y_ref[:, sl] = inner + cross
            vs = (v.astype(F32) * kdec_ref[d, h]).astype(BF16)
            state_sc[d, h] = st * cdec_ref[d * RET_HEADS + h] + lax.dot_general(k, vs, tn, preferred_element_type=F32)


def _retention(rq, rk, rv, dmask, qdec, kdec, cdec, n_batch, seq_len, ctx_len):
    n, w = rq.shape
    c = RET_CHUNK
    assert ctx_len == c and seq_len % c == 0
    ncl = seq_len // c
    ctx_blk0 = (n_batch * seq_len) // c

    def fwd(b, s, cd):
        return (jnp.where(s == 0, ctx_blk0 + b, b * ncl + s - 1), 0)

    def bwd(b, s, cd):
        return (jnp.where(s == 0, ctx_blk0 + b, b * ncl + ncl - s), 0)

    full = lambda a: pl.BlockSpec(a.shape, lambda b, s, cd: (0,) * a.ndim)
    grid_spec = pltpu.PrefetchScalarGridSpec(
        num_scalar_prefetch=1,
        grid=(n_batch, ncl + 1),
        in_specs=[pl.BlockSpec((c, w), fwd)] * 3 + [pl.BlockSpec((c, w), bwd)] * 3 + [full(dmask), full(qdec), full(kdec)],
        out_specs=[pl.BlockSpec((c, w), fwd), pl.BlockSpec((c, w), bwd)],
        scratch_shapes=[pltpu.VMEM((2, RET_HEADS, RET_DK, RET_DK), F32)],
    )
    return pl.pallas_call(
        _ret_kernel,
        grid_spec=grid_spec,
        out_shape=[jax.ShapeDtypeStruct((n, w), F32)] * 2,
        compiler_params=_cp(("parallel", "arbitrary")),
    )(cdec, rq, rk, rv, rq, rk, rv, dmask, qdec, kdec)


def _ret_finish_kernel(yf_ref, yb_ref, rg_ref, gn_ref, o_ref):
    y = yf_ref[...] + yb_ref[...]
    gate = rg_ref[...]
    gate = gate * jax.nn.sigmoid(gate)
    for h in range(RET_HEADS):
        sl = slice(h * RET_DK, (h + 1) * RET_DK)
        yh = y[:, sl]
        mu = jnp.mean(yh, axis=-1, keepdims=True)
        var = jnp.mean(jnp.square(yh - mu), axis=-1, keepdims=True)
        o_ref[:, sl] = ((yh - mu) * lax.rsqrt(var + GN_EPS) * gn_ref[:, sl] * gate[:, sl]).astype(o_ref.dtype)


def _ret_finish(yf, yb, rg, gn):
    n, w = yf.shape
    tm = ROW_TILE
    spec = pl.BlockSpec((tm, w), lambda t: (t, 0))
    return pl.pallas_call(
        _ret_finish_kernel,
        grid=(n // tm,),
        in_specs=[spec, spec, spec, pl.BlockSpec((1, w), lambda t: (0, 0))],
        out_specs=spec,
        out_shape=jax.ShapeDtypeStruct((n, w), BF16),
        compiler_params=_cp(("parallel",)),
    )(yf, yb, rg, gn)


def _out_kernel(a1_ref, a2_ref, x_ref, w_ref, g1_ref, ng_ref, sh_ref, sc_ref, wr_ref, br_ref,
                xo_ref, m_ref, e_ref, gt_ref):
    half = a1_ref.shape[1]
    o = (jnp.dot(a1_ref[...], w_ref[:half, :], preferred_element_type=F32)
         + jnp.dot(a2_ref[...], w_ref[half:, :], preferred_element_type=F32))
    x = x_ref[...] + g1_ref[0] * o
    xo_ref[...] = x
    m = _modulated_norm(x, ng_ref[...], sc_ref[0], sh_ref[0])
    m_ref[...] = m
    logits = jnp.dot(m, wr_ref[...], precision=lax.Precision.HIGHEST, preferred_element_type=F32) + br_ref[...]
    lane = lax.broadcasted_iota(jnp.int32, logits.shape, 1).astype(F32)
    e_out = jnp.zeros(logits.shape, F32)
    g_out = jnp.zeros(logits.shape, F32)
    top0 = None
    denom = None
    for kk in range(TOP_K):
        mx = jnp.max(logits, axis=-1, keepdims=True)
        idx = jnp.min(jnp.where(logits == mx, lane, float(LANES)), axis=-1, keepdims=True)
        if kk == 0:
            top0 = mx
            ex = jnp.ones_like(mx)
            denom = ex
        else:
            ex = jnp.exp(mx - top0)
            denom = denom + ex
        e_out = jnp.where(lane == kk, idx, e_out)
        g_out = jnp.where(lane == kk, ex, g_out)
        logits = jnp.where(lane == idx, NEG * 2.0, logits)
    e_ref[...] = e_out.astype(jnp.int32)
    gt_ref[...] = g_out / denom


def _out_proj(a1, a2, x_all, w_out, mod_l, norm2_g, w_r, b_r, n_rows, tpb, n_batch):
    d = x_all.shape[1]
    half = a1.shape[1]
    tm = ROW_TILE
    n_lat_tiles = tpb * n_batch

    def mod_row(t):
        return jnp.where(t < n_lat_tiles, t // tpb, n_batch)

    row = lambda wd: pl.BlockSpec((tm, wd), lambda t: (t, 0))
    modspec = lambda col: pl.BlockSpec((1, 1, d), lambda t: (mod_row(t), 0, col))
    return pl.pallas_call(
        _out_kernel,
        grid=(n_rows // tm,),
        in_specs=[row(half), row(half), row(d), pl.BlockSpec((2 * half, d), lambda t: (0, 0)),
                  modspec(2), pl.BlockSpec((1, d), lambda t: (0, 0)), modspec(3), modspec(4),
                  pl.BlockSpec((d, LANES), lambda t: (0, 0)), pl.BlockSpec((1, LANES), lambda t: (0, 0))],
        out_specs=[row(d), row(d), row(LANES), row(LANES)],
        out_shape=[jax.ShapeDtypeStruct((n_rows, d), F32), jax.ShapeDtypeStruct((n_rows, d), F32),
                   jax.ShapeDtypeStruct((n_rows, LANES), jnp.int32), jax.ShapeDtypeStruct((n_rows, LANES), F32)],
        compiler_params=_cp(("parallel",)),
    )(a1, a2, x_all, w_out, mod_l, norm2_g.reshape(1, d), mod_l, mod_l, w_r, b_r)


def _row_copy_kernel(sidx_ref, didx_ref, src_ref, dst_in_ref, dst_ref, sem):
    del dst_in_ref
    n = sidx_ref.shape[0]

    def issue(i, carry):
        pltpu.make_async_copy(src_ref.at[pl.ds(sidx_ref[i], 1)], dst_ref.at[pl.ds(didx_ref[i], 1)], sem).start()
        return carry

    lax.fori_loop(0, n, issue, 0)

    def drain(i, carry):
        pltpu.make_async_copy(src_ref.at[pl.ds(0, 1)], dst_ref.at[pl.ds(0, 1)], sem).wait()
        return carry

    lax.fori_loop(0, n, drain, 0)


def _row_copy(src, sidx, didx, dst_init):
    m = sidx.shape[0]
    assert m % COPY_CHUNK == 0
    idx_spec = pl.BlockSpec((COPY_CHUNK,), lambda i: (i,), memory_space=pltpu.SMEM)
    return pl.pallas_call(
        _row_copy_kernel,
        grid=(m // COPY_CHUNK,),
        in_specs=[idx_spec, idx_spec, pl.BlockSpec(memory_space=pl.ANY), pl.BlockSpec(memory_space=pl.ANY)],
        out_specs=pl.BlockSpec(memory_space=pl.ANY),
        out_shape=jax.ShapeDtypeStruct(dst_init.shape, dst_init.dtype),
        scratch_shapes=[pltpu.SemaphoreType.DMA(())],
        input_output_aliases={3: 0},
        compiler_params=pltpu.CompilerParams(dimension_semantics=("arbitrary",), has_side_effects=True),
    )(sidx, didx, src, dst_init)


def _expert_kernel(be_ref, nu_ref, x_ref, w1_ref, b1_ref, w2_ref, b2_ref, o_ref):
    i = pl.program_id(0)

    @pl.when(i < nu_ref[0])
    def _():
        dff = w2_ref.shape[1]
        u = jnp.dot(x_ref[...].astype(BF16), w1_ref[0], preferred_element_type=F32) + b1_ref[0]
        gl = jnp.minimum(u[:, :dff], SWIGLU_LIMIT)
        up = jnp.clip(u[:, dff:], -SWIGLU_LIMIT, SWIGLU_LIMIT)
        act = gl * jax.nn.sigmoid(SWIGLU_ALPHA * gl) * (up + 1.0)
        o_ref[...] = jnp.dot(act.astype(BF16), w2_ref[0], preferred_element_type=F32) + b2_ref[0]

    @pl.when(i >= nu_ref[0])
    def _():
        o_ref[...] = jnp.zeros(o_ref.shape, o_ref.dtype)


def _experts(hs, blk_e, n_used, w1, b1, w2, b2):
    cap, d = hs.shape
    ne, _, dff2 = w1.shape
    dff = dff2 // 2
    bm = MOE_BM
    grid_spec = pltpu.PrefetchScalarGridSpec(
        num_scalar_prefetch=2,
        grid=(cap // bm,),
        in_specs=[pl.BlockSpec((bm, d), lambda i, be, nu: (i, 0)),
                  pl.BlockSpec((1, d, dff2), lambda i, be, nu: (be[i], 0, 0)),
                  pl.BlockSpec((1, 1, dff2), lambda i, be, nu: (be[i], 0, 0)),
                  pl.BlockSpec((1, dff, d), lambda i, be, nu: (be[i], 0, 0)),
                  pl.BlockSpec((1, 1, d), lambda i, be, nu: (be[i], 0, 0))],
        out_specs=pl.BlockSpec((bm, d), lambda i, be, nu: (i, 0)),
    )
    return pl.pallas_call(
        _expert_kernel,
        grid_spec=grid_spec,
        out_shape=jax.ShapeDtypeStruct((cap, d), F32),
        compiler_params=_cp(("arbitrary",)),
    )(blk_e, n_used, hs, w1, b1.reshape(ne, 1, dff2), w2, b2.reshape(ne, 1, d))


def _combine_kernel(x_ref, y_ref, gt_ref, g2_ref, o_ref):
    gt = gt_ref[...]
    acc = y_ref[0] * gt[:, 0:1]
    for kk in range(1, TOP_K):
        acc = acc + y_ref[kk] * gt[:, kk:kk + 1]
    o_ref[...] = x_ref[...] + g2_ref[0] * acc


def _combine(x, y4, gates, mod_l, tpb, n_batch):
    n, d = x.shape
    tm = ROW_TILE
    n_lat_tiles = tpb * n_batch

    def mod_row(t):
        return jnp.where(t < n_lat_tiles, t // tpb, n_batch)

    return pl.pallas_call(
        _combine_kernel,
        grid=(n // tm,),
        in_specs=[pl.BlockSpec((tm, d), lambda t: (t, 0)), pl.BlockSpec((TOP_K, tm, d), lambda t: (0, t, 0)),
                  pl.BlockSpec((tm, LANES), lambda t: (t, 0)), pl.BlockSpec((1, 1, d), lambda t: (mod_row(t), 0, 5))],
        out_specs=pl.BlockSpec((tm, d), lambda t: (t, 0)),
        out_shape=jax.ShapeDtypeStruct((n, d), F32),
        compiler_params=_cp(("parallel",)),
    )(x, y4, gates, mod_l)


def _moe(m, top_e, gates, x, mod_l, w1, b1, w2, b2, tpb, n_batch):
    n, d = m.shape
    nk = n * TOP_K
    bm = MOE_BM
    n_blk = (nk + N_EXPERTS * (bm - 1)) // bm + 1
    cap = n_blk * bm
    flat_e = top_e[:, :TOP_K].reshape(nk)
    onehot = (flat_e[:, None] == jnp.arange(N_EXPERTS, dtype=jnp.int32)[None, :]).astype(jnp.int32)
    csum = jnp.cumsum(onehot, axis=0)
    rank = jnp.sum(jnp.where(onehot > 0, csum, 0), axis=1) - 1
    counts = csum[-1]
    padded = ((counts + bm - 1) // bm) * bm
    pend = jnp.cumsum(padded)
    pstart = pend - padded
    pos = (pstart[flat_e] + rank).astype(jnp.int32)
    n_used = (pend[-1] // bm).astype(jnp.int32)
    blk = jnp.minimum(jnp.arange(n_blk, dtype=jnp.int32), n_used - 1)
    blk_e = jnp.clip(jnp.searchsorted(pend, blk * bm, side="right"), 0, N_EXPERTS - 1).astype(jnp.int32)
    pair = jnp.arange(nk, dtype=jnp.int32)
    tok = pair // TOP_K
    hs = _row_copy(m, tok, pos, jnp.zeros((cap, d), F32))
    out = _experts(hs, blk_e, n_used.reshape(1), w1, b1, w2, b2)
    y4 = _row_copy(out, pos, (pair % TOP_K) * n + tok, jnp.zeros((TOP_K * n, d), F32))
    return _combine(x, y4.reshape(TOP_K, n, d), gates, mod_l, tpb, n_batch)


def _rope_tables(seq_len, d_rot, reps, n_extra):
    t = jnp.arange(seq_len)
    rows = (t // GRID_W).astype(F32)
    cols = (t % GRID_W).astype(F32)
    n_freq = d_rot // 4
    inv = ROPE_THETA ** (-jnp.arange(n_freq, dtype=F32) / n_freq)
    ang = jnp.concatenate([rows[:, None] * inv, cols[:, None] * inv], axis=-1)
    cos = jnp.repeat(jnp.cos(ang), 2, axis=-1)
    sin = jnp.repeat(jnp.sin(ang), 2, axis=-1) * jnp.tile(jnp.asarray([-1.0, 1.0], F32), d_rot // 2)
    cos = jnp.concatenate([jnp.tile(cos, (1, reps)), jnp.ones((n_extra, d_rot * reps), F32)], axis=0)
    sin = jnp.concatenate([jnp.tile(sin, (1, reps)), jnp.zeros((n_extra, d_rot * reps), F32)], axis=0)
    return cos, sin


def _retention_tables(decay_logit):
    log_g = jax.nn.log_sigmoid(decay_logit.astype(F32))
    c = RET_CHUNK
    pos = jnp.arange(c, dtype=F32)
    diff = pos[:, None] - pos[None, :]
    lf = log_g[0][:, None, None]
    lb = log_g[1][:, None, None]
    dm_f = jnp.where(diff >= 0, jnp.exp(lf * jnp.where(diff >= 0, diff, 0.0)), 0.0)
    dm_b = jnp.where(diff < 0, jnp.exp(lb * jnp.where(diff < 0, -diff, 0.0)), 0.0)
    qd_f = jnp.exp(log_g[0][:, None] * (pos + 1.0))
    qd_b = jnp.exp(log_g[1][:, None] * (c - pos))
    kd_f = jnp.exp(log_g[0][:, None] * (c - 1.0 - pos))
    kd_b = jnp.exp(log_g[1][:, None] * pos)
    bc = lambda a: jnp.broadcast_to(a[..., None], a.shape + (LANES,))
    dmask = jnp.stack([dm_f, dm_b])
    qdec = jnp.stack([bc(qd_f), bc(qd_b)])
    kdec = jnp.stack([bc(kd_f), bc(kd_b)])
    cdec = jnp.exp(log_g * c).reshape(-1)
    return dmask, qdec, kdec, cdec


def _heads_major(t, n_batch, length, n_heads):
    return t.reshape(n_batch, length, n_heads, -1).transpose(0, 2, 1, 3)


def _heads_t(t, n_batch, length, n_heads):
    return t.reshape(n_batch, length, n_heads, -1).transpose(0, 2, 3, 1)


def kernel(x, c, ctx, c_ctx, norm1_g, norm2_g, w_mod, b_mod, w_in_even, w_out_even, a_q_norm, a_k_norm, b_q_norm, b_k_norm, b_rpb, w_in_odd, w_out_odd, ret_decay, ret_gn, mla_cq_norm, mla_ckv_norm, w_uq, w_ukv, mla_q_norm, mla_k_norm, w_router, b_router, w_exp1, b_exp1, w_exp2, b_exp2):
    bsz, s, d = x.shape
    cl = ctx.shape[1]
    depth = w_mod.shape[0]
    tm = ROW_TILE
    assert s % tm == 0 and (bsz * cl) == tm and s % NA_QB == 0 and bsz + 1 <= 8
    tpb = s // tm
    nl = bsz * s
    n_all = nl + bsz * cl

    c_rows = jnp.zeros((8, d), F32).at[:bsz].set(c).at[bsz].set(c_ctx)
    mod = _mod_vectors(c_rows, w_mod, b_mod)

    cos_a, sin_a = _rope_tables(s, HEAD_DIM, LANES // HEAD_DIM, tm)
    cos_c, sin_c = _rope_tables(s, RET_DK, 1, tm)
    cos_d, sin_d = _rope_tables(s, MLA_ROPE, LANES // MLA_ROPE, tm)
    na_bases, na_var, na_tabs = _na_tables(s)
    na_tabs = jnp.asarray(na_tabs)

    x_all = jnp.concatenate([x.reshape(nl, d), ctx.reshape(bsz * cl, d)], axis=0)
    lat = lambda t: t[:nl]
    cx = lambda t: t[nl:]
    att_scale = HEAD_DIM ** -0.5 * LOG2E
    grp = GQA_HEADS // GQA_KV_HEADS
    bkv = bsz * GQA_KV_HEADS

    for l in range(depth):
        need_ctx = l < depth - 1
        i = l // 2
        mod_l = mod[l].reshape(8, 1, 6 * d)
        if l % 2 == 0:
            wq, wk, wv, wn = GQA_HEADS * HEAD_DIM, GQA_KV_HEADS * HEAD_DIM, GQA_KV_HEADS * HEAD_DIM, NA_HEADS * HEAD_DIM
            starts = np.cumsum([0, wq, wk, wv, wn, wn])
            segs = [(int(starts[0]), wq, "norm_rope", 1.0), (int(starts[1]), wk, "norm_rope", 1.0),
                    (int(starts[2]), wv, "plain", 1.0), (int(starts[3]), wn, "norm", 1.0),
                    (int(starts[4]), wn, "norm", 1.0), (int(starts[5]), wn, "plain", 1.0)]
            gain = jnp.concatenate([jnp.tile(a_q_norm[i], GQA_HEADS) * att_scale, jnp.tile(a_k_norm[i], GQA_KV_HEADS),
                                    jnp.ones((wv,), F32), jnp.tile(b_q_norm[i], NA_HEADS) * att_scale,
                                    jnp.tile(b_k_norm[i], NA_HEADS), jnp.ones((wn,), F32)]).reshape(1, -1)
            qa, ka, va, qb, kb, vb = _proj(x_all, mod_l, norm1_g[l], w_in_even[i].astype(BF16), gain, cos_a, sin_a,
                                           segs, [BF16] * 6, tpb, bsz)
            k_lat = _heads_major(lat(ka), bsz, s, GQA_KV_HEADS)
            k_cx = _heads_major(cx(ka), bsz, cl, GQA_KV_HEADS)
            v_lat = _heads_major(lat(va), bsz, s, GQA_KV_HEADS)
            v_cx = _heads_major(cx(va), bsz, cl, GQA_KV_HEADS)
            k_all = jnp.concatenate([k_lat, k_cx], axis=2).reshape(bkv, s + cl, HEAD_DIM)
            v_all = jnp.concatenate([v_lat, v_cx], axis=2).reshape(bkv, s + cl, HEAD_DIM)
            qt = _heads_t(lat(qa), bsz, s, GQA_HEADS).reshape(bkv, grp, HEAD_DIM, s)
            oa_t = _flash(qt, k_all, v_all, Q_SUB)
            oa = oa_t.reshape(bsz, GQA_HEADS, HEAD_DIM, s).transpose(0, 3, 1, 2).reshape(nl, wq)
            rp =b_rpb[i].astype(F32) * LOG2E
            bias = jnp.where(na_tabs[:, None] >= 0, rp[:, jnp.maximum(na_tabs, 0)].transpose(1, 0, 2, 3), NEG)
            bh = bsz * NA_HEADS
            qn = _heads_major(lat(qb), bsz, s, NA_HEADS).reshape(bh, s, HEAD_DIM)
            kn = _heads_major(lat(kb), bsz, s, NA_HEADS).reshape(bh, s, HEAD_DIM)
            vn = _heads_major(lat(vb), bsz, s, NA_HEADS).reshape(bh, s, HEAD_DIM)
            kn_c = _heads_major(cx(kb), bsz, cl, NA_HEADS).reshape(bh, cl, HEAD_DIM)
            vn_c = _heads_major(cx(vb), bsz, cl, NA_HEADS).reshape(bh, cl, HEAD_DIM)
            ob = _na_attention(qn, kn, vn, kn_c, vn_c, bias, jnp.asarray(na_bases), jnp.asarray(na_var), NA_HEADS)
            ob = ob.reshape(bsz, NA_HEADS, s, HEAD_DIM).transpose(0, 2, 1, 3).reshape(nl, wn)
            if need_ctx:
                qt_c = _heads_t(cx(qa), bsz, cl, GQA_HEADS).reshape(bkv, grp, HEAD_DIM, cl)
                oa_c = _flash(qt_c, k_cx.reshape(bkv, cl, HEAD_DIM), v_cx.reshape(bkv, cl, HEAD_DIM), Q_SUB)
                oa_c = oa_c.reshape(bsz, GQA_HEADS, HEAD_DIM, cl).transpose(0, 3, 1, 2).reshape(bsz * cl, wq)
                qnt_c = _heads_t(cx(qb), bsz, cl, NA_HEADS).reshape(bh, 1, HEAD_DIM, cl)
                ob_c = _flash(qnt_c, kn_c, vn_c, Q_SUB)
                ob_c = ob_c.reshape(bsz, NA_HEADS, HEAD_DIM, cl).transpose(0, 3, 1, 2).reshape(bsz * cl, wn)
                a1 = jnp.concatenate([oa, oa_c], axis=0)
                a2 = jnp.concatenate([ob, ob_c], axis=0)
            else:
                a1, a2 = oa, ob
            w_out = w_out_even[i].astype(BF16)
        else:
            rw = RET_HEADS * RET_DK
            kr_cols = w_in_odd[i][:, 4 * rw + MLA_Q_LORA + MLA_KV_LORA:]
            w_ext = jnp.concatenate([w_in_odd[i]] + [kr_cols] * (LANES * 2 // MLA_ROPE - 1), axis=1).astype(BF16)
            mla_w = MLA_Q_LORA + MLA_KV_LORA + 2 * LANES
            segs = [(0, rw, "rope", RET_DK ** -0.5), (rw, rw, "rope", 1.0), (2 * rw, rw, "plain", 1.0),
                    (3 * rw, rw, "plain", 1.0), (4 * rw, mla_w, "plain", 1.0)]
            gain = jnp.ones((1, w_ext.shape[1]), F32)
            rq, rk, rv, rg, mla_in = _proj(x_all, mod_l, norm1_g[l], w_ext, gain, cos_c, sin_c, segs,
                                           [BF16, BF16, BF16, F32, F32], tpb, bsz)
            dmask, qdec, kdec, cdec = _retention_tables(ret_decay[i])
            yf, yb = _retention(rq, rk, rv, dmask, qdec, kdec, cdec, bsz, s, cl)
            a1 = _ret_finish(yf, yb, rg, ret_gn[i].reshape(1, rw))
            dqk = MLA_NOPE + MLA_ROPE
            perm_q = np.concatenate([np.arange(h * dqk, h * dqk + MLA_NOPE) for h in range(MLA_HEADS)]
                                    + [np.arange(h * dqk + MLA_NOPE, (h + 1) * dqk) for h in range(MLA_HEADS)])
            dkv = MLA_NOPE + MLA_V
            perm_kv = np.concatenate([np.arange(h * dkv, h * dkv + MLA_NOPE) for h in range(MLA_HEADS)]
                                     + [np.arange(h * dkv + MLA_NOPE, (h + 1) * dkv) for h in range(MLA_HEADS)])
            mla_scale = dqk ** -0.5 * LOG2E
            gq = jnp.concatenate([jnp.tile(mla_q_norm[i][:MLA_NOPE], MLA_HEADS),
                                  jnp.tile(mla_q_norm[i][MLA_NOPE:], MLA_HEADS)]).reshape(1, -1) * mla_scale
            gkn = mla_k_norm[i][:MLA_NOPE].reshape(1, -1)
            gkr = jnp.tile(mla_k_norm[i][MLA_NOPE:], LANES // MLA_ROPE).reshape(1, -1)
            q_m, k_m, v_m = _mla_proj(mla_in, w_uq[i][:, perm_q].astype(BF16), w_ukv[i][:, perm_kv].astype(BF16),
                                      mla_cq_norm[i].reshape(1, -1), mla_ckv_norm[i].reshape(1, -1), gq, gkn, gkr,
                                      cos_d, sin_d, tpb, bsz)
            nw = MLA_HEADS * MLA_NOPE

            def qk_heads(t, length, transposed):
                nope = t[:, :nw].reshape(bsz, length, MLA_HEADS, MLA_NOPE)
                rope = t[:, nw:].reshape(bsz, length, MLA_HEADS, MLA_ROPE)
                full = jnp.concatenate([nope, rope], axis=-1)
                return full.transpose(0, 2, 3, 1) if transposed else full.transpose(0, 2, 1, 3)

            bhm = bsz * MLA_HEADS
            k_lat = qk_heads(lat(k_m), s, False)
            k_cx = qk_heads(cx(k_m), cl, False)
            v_lat = _heads_major(lat(v_m), bsz, s, MLA_HEADS)
            v_cx = _heads_major(cx(v_m), bsz, cl, MLA_HEADS)
            k_all = jnp.concatenate([k_lat, k_cx], axis=2).reshape(bhm, s + cl, dqk)
            v_all = jnp.concatenate([v_lat, v_cx], axis=2).reshape(bhm, s + cl, MLA_V)
            qt = qk_heads(lat(q_m), s, True).reshape(bhm, 1, dqk, s)
            om = _flash(qt, k_all, v_all, 4 * Q_SUB)
            om = om.reshape(bsz, MLA_HEADS, MLA_V, s).transpose(0, 3, 1, 2).reshape(nl, MLA_HEADS * MLA_V)
            if need_ctx:
                qt_c = qk_heads(cx(q_m), cl, True).reshape(bhm, 1, dqk, cl)
                om_c = _flash(qt_c, k_cx.reshape(bhm, cl, dqk), v_cx.reshape(bhm, cl, MLA_V), Q_SUB)
                om_c = om_c.reshape(bsz, MLA_HEADS, MLA_V, cl).transpose(0, 3, 1, 2).reshape(bsz * cl, MLA_HEADS * MLA_V)
                a2 = jnp.concatenate([om, om_c], axis=0)
            else:
                a1 = a1[:nl]
                a2 = om
            w_out = w_out_odd[i].astype(BF16)

        n_rows = n_all if need_ctx else nl
        w_r = jnp.zeros((d, LANES), F32).at[:, :N_EXPERTS].set(w_router[l])
        b_r = jnp.full((1, LANES), NEG, F32).at[0, :N_EXPERTS].set(b_router[l])
        x_new, m, top_e, gates = _out_proj(a1, a2, x_all, w_out, mod_l, norm2_g[l], w_r, b_r, n_rows, tpb, bsz)
        x_all = _moe(m, top_e, gates, x_new, mod_l, w_exp1[l].astype(BF16), b_exp1[l], w_exp2[l].astype(BF16),
                     b_exp2[l], tpb, bsz)
    return x_all[:nl].reshape(bsz, s, d)
```

```python
import functools
import math

import numpy as np
import jax
import jax.numpy as jnp
from jax import lax
from jax.experimental import pallas as pl
from jax.experimental.pallas import tpu as pltpu

F32 = jnp.float32
BF16 = jnp.bfloat16

GRID_W = 64
HEAD_DIM = 64
GQA_HEADS = 8
GQA_KV_HEADS = 2
NA_HEADS = 8
NA_ROWS = 8
NA_COLS = 16
RET_HEADS = 4
RET_DK = 128
MLA_HEADS = 4
MLA_Q_LORA = 256
MLA_KV_LORA = 128
MLA_NOPE = 128
MLA_ROPE = 64
MLA_V = 128
N_EXPERTS = 32
TOP_K = 4
SWIGLU_LIMIT = 7.0
SWIGLU_ALPHA = 1.702
ROPE_THETA = 10000.0
EPS = 1e-6
GN_EPS = 1e-5
LOG2E = math.log2(math.e)
NEG = -1e30

LANES = 128
ROW_TILE = 512
Q_SUB = 256
ONES_ROWS = 16
NA_QB = 128
NA_WIN_ROWS = 10
RET_CHUNK = 256
MOE_BM = 512
TOK_ROWS = 8
VMEM_LIMIT = 56 * 1024 * 1024


def _cp(sem):
    return pltpu.CompilerParams(dimension_semantics=sem, vmem_limit_bytes=VMEM_LIMIT)


def _mod_kernel(c_ref, w_ref, b_ref, o_ref):
    c = c_ref[...]
    s = c * jax.nn.sigmoid(c)
    o_ref[0] = jnp.dot(s, w_ref[0], precision=lax.Precision.HIGHEST, preferred_element_type=F32) + b_ref[0]


def _mod_vectors(c_rows, w_mod, b_mod):
    depth, d, d6 = w_mod.shape
    tn = 1536
    return pl.pallas_call(
        _mod_kernel,
        grid=(depth, d6 // tn),
        in_specs=[pl.BlockSpec((8, d), lambda l, j: (0, 0)),
                  pl.BlockSpec((1, d, tn), lambda l, j: (l, 0, j)),
                  pl.BlockSpec((1, 1, tn), lambda l, j: (l, 0, j))],
        out_specs=pl.BlockSpec((1, 8, tn), lambda l, j: (l, 0, j)),
        out_shape=jax.ShapeDtypeStruct((depth, 8, d6), F32),
        compiler_params=_cp(("parallel", "parallel")),
    )(c_rows, w_mod, b_mod.reshape(depth, 1, d6))


def _modulated_norm(x, g, sc, sh):
    ms = jnp.mean(x * x, axis=-1, keepdims=True)
    return x * lax.rsqrt(ms + EPS) * g * (1.0 + sc) + sh


def _pair_rope(y, cos, sin_signed):
    lane = lax.broadcasted_iota(jnp.int32, y.shape, 1)
    partner = jnp.where((lane & 1) == 0, pltpu.roll(y, LANES - 1, 1), pltpu.roll(y, 1, 1))
    return y * cos + partner * sin_signed


def _store_token_tiles(ref, val):
    n = val.shape[0]
    for j in range(TOK_ROWS):
        ref[pl.ds(j, n, stride=TOK_ROWS), :] = val[:, j * LANES:(j + 1) * LANES]


def _load_token_tiles(ref, n):
    return jnp.concatenate([ref[pl.ds(j, n, stride=TOK_ROWS), :] for j in range(TOK_ROWS)], axis=-1)


def _split_dot(a_f32, w_bf16):
    hi = a_f32.astype(BF16)
    lo = (a_f32 - hi.astype(F32)).astype(BF16)
    return (jnp.dot(hi, w_bf16, preferred_element_type=F32) + jnp.dot(lo, w_bf16, preferred_element_type=F32))


def _proj_kernel(x_ref, sh_ref, sc_ref, g_ref, w_ref, gain_ref, bd_ref, cos_ref, sin_ref, *out_refs, segs):
    a = _modulated_norm(x_ref[...], g_ref[...], sc_ref[0], sh_ref[0]).astype(BF16)
    cos = cos_ref[...]
    sin = sin_ref[...]
    for (start, width, mode, scale), o_ref in zip(segs, out_refs):
        y_seg = jnp.dot(a, w_ref[:, start:start + width], preferred_element_type=F32)
        if mode == "plain":
            o_ref[...] = y_seg.astype(o_ref.dtype)
            continue
        for j in range(width // LANES):
            y = y_seg[:, j * LANES:(j + 1) * LANES]
            if "norm" in mode:
                ms = _split_dot(y * y, bd_ref[...])
                y = y * lax.rsqrt(ms + EPS) * gain_ref[:, start + j * LANES:start + (j + 1) * LANES]
            if scale != 1.0:
                y = y * scale
            if "rope" in mode:
                y = _pair_rope(y, cos, sin)
            o_ref[:, j * LANES:(j + 1) * LANES] = y.astype(o_ref.dtype)


def _proj(x_all, mod_l, norm_g, w, gain, cos_t, sin_t, segs, out_dtypes, n_lat_tiles_per_batch, n_batch):
    n, d = x_all.shape
    tm = ROW_TILE
    n_tiles = n // tm
    wtot = w.shape[1]
    tpb = n_lat_tiles_per_batch
    n_lat_tiles = tpb * n_batch

    def mod_row(t):
        return jnp.where(t < n_lat_tiles, t // tpb, n_batch)

    def rope_row(t):
        return jnp.where(t < n_lat_tiles, t % tpb, tpb)

    bd = np.kron(np.eye(2, dtype=np.float32), np.full((HEAD_DIM, HEAD_DIM), 1.0 / HEAD_DIM, np.float32))
    in_specs = [
        pl.BlockSpec((tm, d), lambda t: (t, 0)),
        pl.BlockSpec((1, 1, d), lambda t: (mod_row(t), 0, 0)),
        pl.BlockSpec((1, 1, d), lambda t: (mod_row(t), 0, 1)),
        pl.BlockSpec((1, d), lambda t: (0, 0)),
        pl.BlockSpec((d, wtot), lambda t: (0, 0)),
        pl.BlockSpec((1, wtot), lambda t: (0, 0)),
        pl.BlockSpec((LANES, LANES), lambda t: (0, 0)),
        pl.BlockSpec((tm, LANES), lambda t: (rope_row(t), 0)),
        pl.BlockSpec((tm, LANES), lambda t: (rope_row(t), 0)),
    ]
    out_specs = [pl.BlockSpec((tm, s[1]), lambda t: (t, 0)) for s in segs]
    out_shape = [jax.ShapeDtypeStruct((n, s[1]), dt) for s, dt in zip(segs, out_dtypes)]
    return pl.pallas_call(
        functools.partial(_proj_kernel, segs=tuple(segs)),
        grid=(n_tiles,),
        in_specs=in_specs,
        out_specs=out_specs,
        out_shape=out_shape,
        compiler_params=_cp(("parallel",)),
    )(x_all, mod_l, mod_l, norm_g.reshape(1, d), w, gain, jnp.asarray(bd, BF16), cos_t, sin_t)


def _mla_proj_kernel(x_ref, wuq_ref, wukv_ref, gcq_ref, gckv_ref, gq_ref, gkn_ref, gkr_ref, cos_ref, sin_ref,
                     q_ref, k_ref, v_ref):
    x = x_ref[...]
    cq = x[:, :MLA_Q_LORA]
    ckv = x[:, MLA_Q_LORA:MLA_Q_LORA + MLA_KV_LORA]
    kr = x[:, MLA_Q_LORA + MLA_KV_LORA:MLA_Q_LORA + MLA_KV_LORA + LANES]
    cos = cos_ref[...]
    sin = sin_ref[...]
    cqn = cq * lax.rsqrt(jnp.mean(cq * cq, axis=-1, keepdims=True) + EPS) * gcq_ref[...]
    ckvn = ckv * lax.rsqrt(jnp.mean(ckv * ckv, axis=-1, keepdims=True) + EPS) * gckv_ref[...]
    q = jnp.dot(cqn.astype(BF16), wuq_ref[...], preferred_element_type=F32)
    kv = jnp.dot(ckvn.astype(BF16), wukv_ref[...], preferred_element_type=F32)
    nh = MLA_HEADS
    d_qk = float(MLA_NOPE + MLA_ROPE)
    lane = lax.broadcasted_iota(jnp.int32, (1, LANES), 1)
    low = lane < MLA_ROPE

    def half_sums(slab):
        sq = slab * slab
        a = jnp.sum(jnp.where(low, sq, 0.0), axis=-1, keepdims=True)
        return a, jnp.sum(sq, axis=-1, keepdims=True) - a

    rope_w = nh * MLA_NOPE
    q_rope_ss = []
    for r in range(nh // 2):
        q_rope_ss.extend(half_sums(q[:, rope_w + r * LANES:rope_w + (r + 1) * LANES]))
    kr_ss, _ = half_sums(kr)
    rs_q, rs_k = [], []
    for h in range(nh):
        qn = q[:, h * MLA_NOPE:(h + 1) * MLA_NOPE]
        kn = kv[:, h * MLA_NOPE:(h + 1) * MLA_NOPE]
        rs_q.append(lax.rsqrt((jnp.sum(qn * qn, axis=-1, keepdims=True) + q_rope_ss[h]) / d_qk + EPS))
        rs_k.append(lax.rsqrt((jnp.sum(kn * kn, axis=-1, keepdims=True) + kr_ss) / d_qk + EPS))
        q_ref[:, h * MLA_NOPE:(h + 1) * MLA_NOPE] = (qn * rs_q[h] * gq_ref[:, h * MLA_NOPE:(h + 1) * MLA_NOPE]).astype(q_ref.dtype)
        k_ref[:, h * MLA_NOPE:(h + 1) * MLA_NOPE] = (kn * rs_k[h] * gkn_ref[...]).astype(k_ref.dtype)
    kr_rot = _pair_rope(kr * gkr_ref[...], cos, sin)
    for r in range(nh // 2):
        sl = slice(rope_w + r * LANES, rope_w + (r + 1) * LANES)
        yq = q[:, sl] * jnp.where(low, rs_q[2 * r], rs_q[2 * r + 1]) * gq_ref[:, sl]
        q_ref[:, sl] = _pair_rope(yq, cos, sin).astype(q_ref.dtype)
        k_ref[:, sl] = (kr_rot * jnp.where(low, rs_k[2 * r], rs_k[2 * r + 1])).astype(k_ref.dtype)
    v_ref[...] = kv[:, nh * MLA_NOPE:].astype(v_ref.dtype)


def _mla_proj(mla_in, wuq, wukv, gcq, gckv, gq, gkn, gkr, cos_t, sin_t, tpb, n_batch):
    n, win = mla_in.shape
    tm = ROW_TILE
    n_lat_tiles = tpb * n_batch

    def rope_row(t):
        return jnp.where(t < n_lat_tiles, t % tpb, tpb)

    qk_w = MLA_HEADS * (MLA_NOPE + MLA_ROPE)
    v_w = MLA_HEADS * MLA_V
    full = lambda a: pl.BlockSpec(a.shape, lambda t: (0,) * a.ndim)
    return pl.pallas_call(
        _mla_proj_kernel,
        grid=(n // tm,),
        in_specs=[pl.BlockSpec((tm, win), lambda t: (t, 0)), full(wuq), full(wukv), full(gcq), full(gckv),
                  full(gq), full(gkn), full(gkr),
                  pl.BlockSpec((tm, LANES), lambda t: (rope_row(t), 0)),
                  pl.BlockSpec((tm, LANES), lambda t: (rope_row(t), 0))],
        out_specs=[pl.BlockSpec((tm, qk_w), lambda t: (t, 0)), pl.BlockSpec((tm, qk_w), lambda t: (t, 0)),
                   pl.BlockSpec((tm, v_w), lambda t: (t, 0))],
        out_shape=[jax.ShapeDtypeStruct((n, qk_w), BF16), jax.ShapeDtypeStruct((n, qk_w), BF16),
                   jax.ShapeDtypeStruct((n, v_w), BF16)],
        compiler_params=_cp(("parallel",)),
    )(mla_in, wuq, wukv, gcq, gckv, gq, gkn, gkr, cos_t, sin_t)


def _flash_kernel(qt_ref, k_ref, vt_ref, ot_ref, q_sc, s_a, s_b, mx_a, mx_b, m_sc, acc_sc, *, g, n_chunks, dv):
    tq = qt_ref.shape[3]
    for gi in range(g):
        q_sc[:, gi * tq:(gi + 1) * tq] = qt_ref[0, gi]
    m_sc[...] = jnp.full(m_sc.shape, NEG, F32)
    acc_sc[...] = jnp.zeros(acc_sc.shape, F32)

    def scores(ci, s_ref, mx_ref):
        s = jnp.dot(k_ref[0, ci], q_sc[...], preferred_element_type=F32)
        s_ref[...] = s
        mx_ref[...] = jnp.max(s, axis=0, keepdims=True)

    def accumulate(ci, s_ref, mx_ref):
        m_old = m_sc[...]
        m_new = jnp.maximum(m_old, mx_ref[...])
        alpha = jnp.exp2(m_old - m_new)
        p = jnp.exp2(s_ref[...] - m_new).astype(BF16)
        acc_sc[...] = alpha * acc_sc[...] + jnp.dot(vt_ref[0, ci], p, preferred_element_type=F32)
        m_sc[...] = m_new

    scores(0, s_a, mx_a)
    n_pairs = (n_chunks - 1) // 2

    def pair(j, carry):
        ci = 2 * j
        scores(ci + 1, s_b, mx_b)
        accumulate(ci, s_a, mx_a)
        scores(ci + 2, s_a, mx_a)
        accumulate(ci + 1, s_b, mx_b)
        return carry

    lax.fori_loop(0, n_pairs, pair, 0)
    if (n_chunks - 1) % 2 == 0:
        accumulate(n_chunks - 1, s_a, mx_a)
    else:
        scores(n_chunks - 1, s_b, mx_b)
        accumulate(n_chunks - 2, s_a, mx_a)
        accumulate(n_chunks - 1, s_b, mx_b)
    acc = acc_sc[...]
    o = acc[:dv] / acc[dv:dv + 1]
    for gi in range(g):
        ot_ref[0, gi] = o[:, gi * tq:(gi + 1) * tq].astype(ot_ref.dtype)


def _key_chunk(lk):
    for tk in (1280, 1024, 768, 512, 256):
        if lk % tk == 0:
            return tk
    raise ValueError(f"key length {lk} must be a multiple of 256")


def _flash(qt, k, v, tq_blk):
    bk, g, dq, lq = qt.shape
    lk, dv = v.shape[1], v.shape[2]
    tk = _key_chunk(lk)
    nch = lk // tk
    kc = k.reshape(bk, nch, tk, dq)
    dve = dv + ONES_ROWS
    vt = jnp.concatenate([v, jnp.ones((bk, lk, ONES_ROWS), v.dtype)], axis=-1)
    vt = vt.reshape(bk, nch, tk, dve).transpose(0, 1, 3, 2)
    tq_blk = min(tq_blk, lq)
    assert lq % tq_blk == 0 and tq_blk % LANES == 0
    w = g * tq_blk
    return pl.pallas_call(
        functools.partial(_flash_kernel, g=g, n_chunks=nch, dv=dv),
        grid=(bk, lq // tq_blk),
        in_specs=[pl.BlockSpec((1, g, dq, tq_blk), lambda b, i: (b, 0, 0, i)),
                  pl.BlockSpec((1, nch, tk, dq), lambda b, i: (b, 0, 0, 0)),
                  pl.BlockSpec((1, nch, dve, tk), lambda b, i: (b, 0, 0, 0))],
        out_specs=pl.BlockSpec((1, g, dv, tq_blk), lambda b, i: (b, 0, 0, i)),
        out_shape=jax.ShapeDtypeStruct((bk, g, dv, lq), BF16),
        scratch_shapes=[pltpu.VMEM((dq, w), BF16), pltpu.VMEM((tk, w), F32), pltpu.VMEM((tk, w), F32),
                        pltpu.VMEM((1, w), F32), pltpu.VMEM((1, w), F32), pltpu.VMEM((1, w), F32),
                        pltpu.VMEM((dve, w), F32)],
        compiler_params=_cp(("parallel", "parallel")),
    )(qt, kc, vt)


def _na_tables(seq_len):
    rows_n = seq_len // GRID_W
    assert rows_n >= NA_WIN_ROWS and NA_ROWS <= rows_n
    nb = seq_len // NA_QB
    rpq = NA_QB // GRID_W
    band = NA_WIN_ROWS * GRID_W
    variants, var_id, bases = {}, [], []
    for j in range(nb):
        base = int(np.clip(rpq * j - NA_ROWS // 2, 0, rows_n - NA_WIN_ROWS))
        bases.append(base)
        t = np.arange(NA_QB) + j * NA_QB
        r, col = t // GRID_W, t % GRID_W
        r0 = np.clip(r - NA_ROWS // 2, 0, rows_n - NA_ROWS)
        c0 = np.clip(col - NA_COLS // 2, 0, GRID_W - NA_COLS)
        kk = np.arange(band)
        kr = base + kk // GRID_W
        kc = kk % GRID_W
        inside = ((kr[None] >= r0[:, None]) & (kr[None] < r0[:, None] + NA_ROWS)
                  & (kc[None] >= c0[:, None]) & (kc[None] < c0[:, None] + NA_COLS))
        rel = (kr[None] - r[:, None] + NA_ROWS - 1) * (2 * NA_COLS - 1) + (kc[None] - col[:, None] + NA_COLS - 1)
        tab = np.where(inside, rel, -1).astype(np.int32)
        assert (inside.sum(axis=1) == NA_ROWS * NA_COLS).all()
        key = tab.tobytes()
        if key not in variants:
            variants[key] = (len(variants), tab)
        var_id.append(variants[key][0])
    tabs = np.stack([v[1] for v in sorted(variants.values(), key=lambda kv: kv[0])])
    n_dr = 2 * NA_ROWS - 1
    n_dc = 2 * NA_COLS - 1
    col = np.arange(GRID_W)
    c0 = np.clip(col - NA_COLS // 2, 0, GRID_W - NA_COLS)
    col_in = (col[None] >= c0[:, None]) & (col[None] < c0[:, None] + NA_COLS)
    dc = col[None] - col[:, None] + NA_COLS - 1
    t5 = tabs.reshape(len(tabs), rpq, GRID_W, NA_WIN_ROWS, GRID_W)
    row_sel = np.full((len(tabs), rpq, NA_WIN_ROWS), n_dr, np.int32)
    for v in range(len(tabs)):
        for a in range(rpq):
            for i in range(NA_WIN_ROWS):
                blk = t5[v, a, :, i, :]
                if (blk >= 0).any():
                    dr = int(blk[blk >= 0][0]) // n_dc
                    assert (np.where(col_in, dr * n_dc + dc, -1) == blk).all()
                    row_sel[v, a, i] = dr
                else:
                    assert (blk < 0).all()
    dc_onehot = (dc[None] == np.arange(n_dc)[:, None, None]).astype(np.float32)
    return np.asarray(bases, np.int32), np.asarray(var_id, np.int32), row_sel, col_in, dc_onehot


def _na_bias(rpb, row_sel, col_in, dc_onehot):
    h = rpb.shape[0]
    n_dr, n_dc = 2 * NA_ROWS - 1, 2 * NA_COLS - 1
    t = jnp.einsum("hdj,jck->hdck", rpb.astype(F32).reshape(h, n_dr, n_dc) * LOG2E, jnp.asarray(dc_onehot),
                   precision=lax.Precision.HIGHEST)
    t = jnp.where(jnp.asarray(col_in)[None, None], t, NEG)
    t = jnp.concatenate([t, jnp.full((h, 1, GRID_W, GRID_W), NEG, F32)], axis=1)
    nv, rpq, nw = row_sel.shape
    b = t[:, jnp.asarray(row_sel)]
    return b.transpose(1, 0, 2, 4, 3, 5).reshape(nv, h, rpq * GRID_W, nw * GRID_W)


def _na_kernel(base_ref, var_ref, q_ref, k_ref, v_ref, kc_ref, vc_ref, bias_ref, o_ref):
    j = pl.program_id(1)
    band = NA_WIN_ROWS * GRID_W
    start = pl.multiple_of(base_ref[j] * GRID_W, GRID_W)
    q = q_ref[0]
    kw = k_ref[0, pl.ds(start, band), :]
    vw = v_ref[0, pl.ds(start, band), :]
    nt = (((1,), (1,)), ((), ()))
    s_win = lax.dot_general(q, kw, nt, preferred_element_type=F32) + bias_ref[0, 0]
    s_ctx = lax.dot_general(q, kc_ref[0], nt, preferred_element_type=F32)
    m = jnp.maximum(jnp.max(s_win, axis=-1, keepdims=True), jnp.max(s_ctx, axis=-1, keepdims=True))
    p_win = jnp.exp2(s_win - m)
    p_ctx = jnp.exp2(s_ctx - m)
    l = jnp.sum(p_win, axis=-1, keepdims=True) + jnp.sum(p_ctx, axis=-1, keepdims=True)
    o = (jnp.dot(p_win.astype(BF16), vw, preferred_element_type=F32)
         + jnp.dot(p_ctx.astype(BF16), vc_ref[0], preferred_element_type=F32))
    o_ref[0] = (o / l).astype(o_ref.dtype)


def _na_attention(q, k, v, k_ctx, v_ctx, bias, bases, var_id, n_heads):
    bh, s, d = q.shape
    cl = k_ctx.shape[1]
    band = NA_WIN_ROWS * GRID_W
    nb = s // NA_QB
    grid_spec = pltpu.PrefetchScalarGridSpec(
        num_scalar_prefetch=2,
        grid=(bh, nb),
        in_specs=[pl.BlockSpec((1, NA_QB, d), lambda b, j, bs, vr: (b, j, 0)),
                  pl.BlockSpec((1, s, d), lambda b, j, bs, vr: (b, 0, 0)),
                  pl.BlockSpec((1, s, d), lambda b, j, bs, vr: (b, 0, 0)),
                  pl.BlockSpec((1, cl, d), lambda b, j, bs, vr: (b, 0, 0)),
                  pl.BlockSpec((1, cl, d), lambda b, j, bs, vr: (b, 0, 0)),
                  pl.BlockSpec((1, 1, NA_QB, band), lambda b, j, bs, vr: (vr[j], b % n_heads, 0, 0))],
        out_specs=pl.BlockSpec((1, NA_QB, d), lambda b, j, bs, vr: (b, j, 0)),
    )
    return pl.pallas_call(
        _na_kernel,
        grid_spec=grid_spec,
        out_shape=jax.ShapeDtypeStruct((bh, s, d), BF16),
        compiler_params=_cp(("parallel", "arbitrary")),
    )(bases, var_id, q, k, v, k_ctx, v_ctx, bias)


def _ret_kernel(cdec_ref, qf_ref, kf_ref, vf_ref, qb_ref, kb_ref, vb_ref, dmask_ref, qdec_ref, kdec_ref,
                yf_ref, yb_ref, state_sc):
    @pl.when(pl.program_id(1) == 0)
    def _():
        state_sc[...] = jnp.zeros(state_sc.shape, F32)

    nt = (((1,), (1,)), ((), ()))
    tn = (((0,), (0,)), ((), ()))
    dk = RET_DK
    for d, (q_ref, k_ref, v_ref, y_ref) in enumerate(((qf_ref, kf_ref, vf_ref, yf_ref),
                                                      (qb_ref, kb_ref, vb_ref, yb_ref))):
        for h in range(RET_HEADS):
            sl = slice(h * dk, (h + 1) * dk)
            q = q_ref[:, sl]
            k = k_ref[:, sl]
            v = v_ref[:, sl]
            st = state_sc[d, h]
            a = lax.dot_general(q, k, nt, preferred_element_type=F32) * dmask_ref[d, h]
            inner = jnp.dot(a.astype(BF16), v, preferred_element_type=F32)
            cross = jnp.dot(q, st.astype(BF16), preferred_element_type=F32) * qdec_ref[d, h]
            y_ref[:, sl] = inner + cross
            vs = (v.astype(F32) * kdec_ref[d, h]).astype(BF16)
            state_sc[d, h] = st * cdec_ref[d * RET_HEADS + h] + lax.dot_general(k, vs, tn, preferred_element_type=F32)


def _retention(rq, rk, rv, dmask, qdec, kdec, cdec, n_batch, seq_len, ctx_len):
    n, w = rq.shape
    c = RET_CHUNK
    assert ctx_len == c and seq_len % c == 0
    ncl = seq_len // c
    ctx_blk0 = (n_batch * seq_len) // c

    def fwd(b, s, cd):
        return (jnp.where(s == 0, ctx_blk0 + b, b * ncl + s - 1), 0)

    def bwd(b, s, cd):
        return (jnp.where(s == 0, ctx_blk0 + b, b * ncl + ncl - s), 0)

    full = lambda a: pl.BlockSpec(a.shape, lambda b, s, cd: (0,) * a.ndim)
    grid_spec = pltpu.PrefetchScalarGridSpec(
        num_scalar_prefetch=1,
        grid=(n_batch, ncl + 1),
        in_specs=[pl.BlockSpec((c, w), fwd)] * 3 + [pl.BlockSpec((c, w), bwd)] * 3 + [full(dmask), full(qdec), full(kdec)],
        out_specs=[pl.BlockSpec((c, w), fwd), pl.BlockSpec((c, w), bwd)],
        scratch_shapes=[pltpu.VMEM((2, RET_HEADS, RET_DK, RET_DK), F32)],
    )
    return pl.pallas_call(
        _ret_kernel,
        grid_spec=grid_spec,
        out_shape=[jax.ShapeDtypeStruct((n, w), F32)] * 2,
        compiler_params=_cp(("parallel", "arbitrary")),
    )(cdec, rq, rk, rv, rq, rk, rv, dmask, qdec, kdec)


def _ret_finish_kernel(yf_ref, yb_ref, rg_ref, gn_ref, o_ref):
    y = yf_ref[...] + yb_ref[...]
    gate = rg_ref[...]
    gate = gate * jax.nn.sigmoid(gate)
    for h in range(RET_HEADS):
        sl = slice(h * RET_DK, (h + 1) * RET_DK)
        yh = y[:, sl]
        mu = jnp.mean(yh, axis=-1, keepdims=True)
        var = jnp.mean(jnp.square(yh - mu), axis=-1, keepdims=True)
        o_ref[:, sl] = ((yh - mu) * lax.rsqrt(var + GN_EPS) * gn_ref[:, sl] * gate[:, sl]).astype(o_ref.dtype)


def _ret_finish(yf, yb, rg, gn):
    n, w = yf.shape
    tm = ROW_TILE
    spec = pl.BlockSpec((tm, w), lambda t: (t, 0))
    return pl.pallas_call(
        _ret_finish_kernel,
        grid=(n // tm,),
        in_specs=[spec, spec, spec, pl.BlockSpec((1, w), lambda t: (0, 0))],
        out_specs=spec,
        out_shape=jax.ShapeDtypeStruct((n, w), BF16),
        compiler_params=_cp(("parallel",)),
    )(yf, yb, rg, gn)


def _out_kernel(a1_ref, a2_ref, x_ref, w_ref, g1_ref, ng_ref, sh_ref, sc_ref, wr_ref, br_ref,
                xo_ref, m_ref, e_ref, gt_ref):
    half = a1_ref.shape[1]
    o = (jnp.dot(a1_ref[...], w_ref[:half, :], preferred_element_type=F32)
         + jnp.dot(a2_ref[...], w_ref[half:, :], preferred_element_type=F32))
    x = x_ref[...] + g1_ref[0] * o
    xo_ref[...] = x
    m = _modulated_norm(x, ng_ref[...], sc_ref[0], sh_ref[0])
    _store_token_tiles(m_ref, m)
    logits =jnp.dot(m, wr_ref[...], precision=lax.Precision.HIGHEST, preferred_element_type=F32) + br_ref[...]
    lane = lax.broadcasted_iota(jnp.int32, logits.shape, 1).astype(F32)
    e_out = jnp.zeros(logits.shape, F32)
    g_out = jnp.zeros(logits.shape, F32)
    top0 = None
    denom = None
    for kk in range(TOP_K):
        mx = jnp.max(logits, axis=-1, keepdims=True)
        idx = jnp.min(jnp.where(logits == mx, lane, float(LANES)), axis=-1, keepdims=True)
        if kk == 0:
            top0 = mx
            ex = jnp.ones_like(mx)
            denom = ex
        else:
            ex = jnp.exp(mx - top0)
            denom = denom + ex
        e_out = jnp.where(lane == kk, idx, e_out)
        g_out = jnp.where(lane == kk, ex, g_out)
        logits = jnp.where(lane == idx, NEG * 2.0, logits)
    e_ref[...] = e_out.astype(jnp.int32)
    gt_ref[...] = g_out / denom


def _out_proj(a1, a2, x_all, w_out, mod_l, norm2_g, w_r, b_r, n_rows, tpb, n_batch):
    d = x_all.shape[1]
    half = a1.shape[1]
    tm = ROW_TILE
    n_lat_tiles = tpb * n_batch

    def mod_row(t):
        return jnp.where(t < n_lat_tiles, t // tpb, n_batch)

    row = lambda wd: pl.BlockSpec((tm, wd), lambda t: (t, 0))
    modspec = lambda col: pl.BlockSpec((1, 1, d), lambda t: (mod_row(t), 0, col))
    return pl.pallas_call(
        _out_kernel,
        grid=(n_rows // tm,),
        in_specs=[row(half), row(half), row(d), pl.BlockSpec((2 * half, d), lambda t: (0, 0)),
                  modspec(2), pl.BlockSpec((1, d), lambda t: (0, 0)), modspec(3), modspec(4),
                  pl.BlockSpec((d, LANES), lambda t: (0, 0)), pl.BlockSpec((1, LANES), lambda t: (0, 0))],
        out_specs=[row(d), pl.BlockSpec((tm * TOK_ROWS, LANES), lambda t: (t, 0)), row(LANES), row(LANES)],
        out_shape=[jax.ShapeDtypeStruct((n_rows, d), F32), jax.ShapeDtypeStruct((n_rows * TOK_ROWS, LANES), F32),
                   jax.ShapeDtypeStruct((n_rows, LANES), jnp.int32), jax.ShapeDtypeStruct((n_rows, LANES), F32)],
        compiler_params=_cp(("parallel",)),
    )(a1, a2, x_all, w_out, mod_l, norm2_g.reshape(1, d), mod_l, mod_l, w_r, b_r)


def _token_rows(ref, idx):
    return ref.at[pl.ds(pl.multiple_of(idx * TOK_ROWS, TOK_ROWS), TOK_ROWS)]


def _dispatch_kernel(pos_ref, m_ref, hs_in_ref, hs_ref, sem):
    del hs_in_ref
    n_tok = m_ref.shape[0] // TOK_ROWS

    def issue(r, carry):
        for kk in range(TOP_K):
            pltpu.make_async_copy(_token_rows(m_ref, r), _token_rows(hs_ref, pos_ref[r * TOP_K + kk]), sem).start()
        return carry

    lax.fori_loop(0, n_tok, issue, 0)

    def drain(r, carry):
        for kk in range(TOP_K):
            pltpu.make_async_copy(_token_rows(m_ref, 0), _token_rows(hs_ref, 0), sem).wait()
        return carry

    lax.fori_loop(0, n_tok, drain, 0)


def _dispatch(m_tiles, pos, hs_init):
    n_tok = m_tiles.shape[0] // TOK_ROWS
    tm = ROW_TILE
    return pl.pallas_call(
        _dispatch_kernel,
        grid=(n_tok // tm,),
        in_specs=[pl.BlockSpec((tm * TOP_K,), lambda t: (t,), memory_space=pltpu.SMEM),
                  pl.BlockSpec((tm * TOK_ROWS, LANES), lambda t: (t, 0)),
                  pl.BlockSpec(memory_space=pl.ANY)],
        out_specs=pl.BlockSpec(memory_space=pl.ANY),
        out_shape=jax.ShapeDtypeStruct(hs_init.shape, hs_init.dtype),
        scratch_shapes=[pltpu.SemaphoreType.DMA(())],
        input_output_aliases={2: 0},
        compiler_params=pltpu.CompilerParams(dimension_semantics=("arbitrary",), has_side_effects=True,
                                             vmem_limit_bytes=VMEM_LIMIT),
    )(pos, m_tiles, hs_init)


def _expert_kernel(be_ref, nu_ref, x_ref, w1_ref, b1_ref, w2_ref, b2_ref, o_ref):
    i = pl.program_id(0)

    @pl.when(i < nu_ref[0])
    def _():
        dff = w2_ref.shape[1]
        x = _load_token_tiles(x_ref, MOE_BM).astype(BF16)
        u = jnp.dot(x, w1_ref[0], preferred_element_type=F32) + b1_ref[0]
        gl = jnp.minimum(u[:, :dff], SWIGLU_LIMIT)
        up = jnp.clip(u[:, dff:], -SWIGLU_LIMIT, SWIGLU_LIMIT)
        act = gl * jax.nn.sigmoid(SWIGLU_ALPHA * gl) * (up + 1.0)
        _store_token_tiles(o_ref, jnp.dot(act.astype(BF16), w2_ref[0], preferred_element_type=F32) + b2_ref[0])

    @pl.when(i >= nu_ref[0])
    def _():
        o_ref[...] = jnp.zeros(o_ref.shape, o_ref.dtype)


def _experts(hs, blk_e, n_used, w1, b1, w2, b2):
    ne, d, dff2 = w1.shape
    cap = hs.shape[0] // TOK_ROWS
    dff = dff2 // 2
    bm = MOE_BM
    tile_spec = pl.BlockSpec((bm * TOK_ROWS, LANES), lambda i, be, nu: (i, 0))
    grid_spec = pltpu.PrefetchScalarGridSpec(
        num_scalar_prefetch=2,
        grid=(cap // bm,),
        in_specs=[tile_spec,
                  pl.BlockSpec((1, d, dff2), lambda i, be, nu: (be[i], 0, 0)),
                  pl.BlockSpec((1, 1, dff2), lambda i, be, nu: (be[i], 0, 0)),
                  pl.BlockSpec((1, dff, d), lambda i, be, nu: (be[i], 0, 0)),
                  pl.BlockSpec((1, 1, d), lambda i, be, nu: (be[i], 0, 0))],
        out_specs=tile_spec,
    )
    return pl.pallas_call(
        _expert_kernel,
        grid_spec=grid_spec,
        out_shape=jax.ShapeDtypeStruct(hs.shape, F32),
        compiler_params=_cp(("arbitrary",)),
    )(blk_e, n_used, hs, w1, b1.reshape(ne, 1, dff2), w2, b2.reshape(ne, 1, d))


def _combine_kernel(pos_ref, x_ref, gt_ref, g2_ref, out_ref, o_ref, y_sc, sem):
    n_tok = x_ref.shape[0]

    def issue(r, carry):
        for kk in range(TOP_K):
            pltpu.make_async_copy(_token_rows(out_ref, pos_ref[r * TOP_K + kk]), _token_rows(y_sc.at[kk], r), sem).start()
        return carry

    lax.fori_loop(0, n_tok, issue, 0)

    def drain(r, carry):
        for kk in range(TOP_K):
            pltpu.make_async_copy(_token_rows(out_ref, 0), _token_rows(y_sc.at[kk], 0), sem).wait()
        return carry

    lax.fori_loop(0, n_tok, drain, 0)
    gt = gt_ref[...]
    for j in range(TOK_ROWS):
        sl = slice(j * LANES, (j + 1) * LANES)
        acc = y_sc[0, pl.ds(j, n_tok, stride=TOK_ROWS), :] * gt[:, 0:1]
        for kk in range(1, TOP_K):
            acc = acc + y_sc[kk, pl.ds(j, n_tok, stride=TOK_ROWS), :] * gt[:, kk:kk + 1]
        o_ref[:, sl] = x_ref[:, sl] + g2_ref[0, :, sl] * acc


def _combine(x, out_tiles, pos, gates, mod_l, tpb, n_batch):
    n, d = x.shape
    tm = ROW_TILE
    n_lat_tiles = tpb * n_batch

    def mod_row(t):
        return jnp.where(t < n_lat_tiles, t // tpb, n_batch)

    return pl.pallas_call(
        _combine_kernel,
        grid=(n // tm,),
        in_specs=[pl.BlockSpec((tm * TOP_K,), lambda t: (t,), memory_space=pltpu.SMEM),
                  pl.BlockSpec((tm, d), lambda t: (t, 0)), pl.BlockSpec((tm, LANES), lambda t: (t, 0)),
                  pl.BlockSpec((1, 1, d), lambda t: (mod_row(t), 0, 5)), pl.BlockSpec(memory_space=pl.ANY)],
        out_specs=pl.BlockSpec((tm, d), lambda t: (t, 0)),
        out_shape=jax.ShapeDtypeStruct((n, d), F32),
        scratch_shapes=[pltpu.VMEM((TOP_K, tm * TOK_ROWS, LANES), F32), pltpu.SemaphoreType.DMA(())],
        compiler_params=_cp(("arbitrary",)),
    )(pos, x, gates, mod_l, out_tiles)


def _moe(m_tiles, top_e, gates, x, mod_l, w1, b1, w2, b2, tpb, n_batch):
    n, d = x.shape
    nk = n * TOP_K
    bm = MOE_BM
    n_blk = (nk + N_EXPERTS * (bm - 1)) // bm + 1
    cap = n_blk * bm
    flat_e = top_e[:, :TOP_K].reshape(nk)
    onehot = (flat_e[:, None] == jnp.arange(N_EXPERTS, dtype=jnp.int32)[None, :]).astype(jnp.int32)
    csum = jnp.cumsum(onehot, axis=0)
    rank = jnp.sum(jnp.where(onehot > 0, csum, 0), axis=1) - 1
    counts = csum[-1]
    padded = ((counts + bm - 1) // bm) * bm
    pend = jnp.cumsum(padded)
    pstart = pend - padded
    pos = (pstart[flat_e] + rank).astype(jnp.int32)
    n_used = (pend[-1] // bm).astype(jnp.int32)
    blk = jnp.minimum(jnp.arange(n_blk, dtype=jnp.int32), n_used - 1)
    blk_e = jnp.sum((pend[None, :] <= (blk * bm)[:, None]).astype(jnp.int32), axis=1)
    blk_e = jnp.clip(blk_e, 0, N_EXPERTS - 1).astype(jnp.int32)
    hs = _dispatch(m_tiles, pos, jnp.zeros((cap * TOK_ROWS, LANES), F32))
    out = _experts(hs, blk_e, n_used.reshape(1), w1, b1, w2, b2)
    return _combine(x, out, pos, gates, mod_l, tpb, n_batch)


def _rope_tables(seq_len, d_rot, reps, n_extra):
    t = jnp.arange(seq_len)
    rows = (t // GRID_W).astype(F32)
    cols = (t % GRID_W).astype(F32)
    n_freq = d_rot // 4
    inv = ROPE_THETA ** (-jnp.arange(n_freq, dtype=F32) / n_freq)
    ang = jnp.concatenate([rows[:, None] * inv, cols[:, None] * inv], axis=-1)
    cos = jnp.repeat(jnp.cos(ang), 2, axis=-1)
    sin = jnp.repeat(jnp.sin(ang), 2, axis=-1) * jnp.tile(jnp.asarray([-1.0, 1.0], F32), d_rot // 2)
    cos = jnp.concatenate([jnp.tile(cos, (1, reps)), jnp.ones((n_extra, d_rot * reps), F32)], axis=0)
    sin = jnp.concatenate([jnp.tile(sin, (1, reps)), jnp.zeros((n_extra, d_rot * reps), F32)], axis=0)
    return cos, sin


def _retention_tables(decay_logit):
    log_g = jax.nn.log_sigmoid(decay_logit.astype(F32))
    c = RET_CHUNK
    pos = jnp.arange(c, dtype=F32)
    diff = pos[:, None] - pos[None, :]
    lf = log_g[0][:, None, None]
    lb = log_g[1][:, None, None]
    dm_f = jnp.where(diff >= 0, jnp.exp(lf * jnp.where(diff >= 0, diff, 0.0)), 0.0)
    dm_b = jnp.where(diff < 0, jnp.exp(lb * jnp.where(diff < 0, -diff, 0.0)), 0.0)
    qd_f = jnp.exp(log_g[0][:, None] * (pos + 1.0))
    qd_b = jnp.exp(log_g[1][:, None] * (c - pos))
    kd_f = jnp.exp(log_g[0][:, None] * (c - 1.0 - pos))
    kd_b = jnp.exp(log_g[1][:, None] * pos)
    bc = lambda a: jnp.broadcast_to(a[..., None], a.shape + (LANES,))
    dmask = jnp.stack([dm_f, dm_b])
    qdec = jnp.stack([bc(qd_f), bc(qd_b)])
    kdec = jnp.stack([bc(kd_f), bc(kd_b)])
    cdec = jnp.exp(log_g * c).reshape(-1)
    return dmask, qdec, kdec, cdec


def _heads_major(t, n_batch, length, n_heads):
    return t.reshape(n_batch, length, n_heads, -1).transpose(0, 2, 1, 3)


def _heads_t(t, n_batch, length, n_heads):
    return t.reshape(n_batch, length, n_heads, -1).transpose(0, 2, 3, 1)


def kernel(x, c, ctx, c_ctx, norm1_g, norm2_g, w_mod, b_mod, w_in_even, w_out_even, a_q_norm, a_k_norm, b_q_norm, b_k_norm, b_rpb, w_in_odd, w_out_odd, ret_decay, ret_gn, mla_cq_norm, mla_ckv_norm, w_uq, w_ukv, mla_q_norm, mla_k_norm, w_router, b_router, w_exp1, b_exp1, w_exp2, b_exp2):
    bsz, s, d = x.shape
    cl = ctx.shape[1]
    depth = w_mod.shape[0]
    tm = ROW_TILE
    assert s % tm == 0 and (bsz * cl) == tm and s % NA_QB == 0 and bsz + 1 <= 8 and d == TOK_ROWS * LANES
    tpb = s // tm
    nl = bsz * s
    n_all = nl + bsz * cl

    c_rows = jnp.zeros((8, d), F32).at[:bsz].set(c).at[bsz].set(c_ctx)
    mod = _mod_vectors(c_rows, w_mod, b_mod)

    cos_a, sin_a = _rope_tables(s, HEAD_DIM, LANES // HEAD_DIM, tm)
    cos_c, sin_c = _rope_tables(s, RET_DK, 1, tm)
    cos_d, sin_d = _rope_tables(s, MLA_ROPE, LANES // MLA_ROPE, tm)
    na_bases, na_var, na_row_sel, na_col_in, na_dc_onehot = _na_tables(s)

    x_all = jnp.concatenate([x.reshape(nl, d), ctx.reshape(bsz * cl, d)], axis=0)
    lat = lambda t: t[:nl]
    cx = lambda t: t[nl:]
    att_scale = HEAD_DIM ** -0.5 * LOG2E
    grp = GQA_HEADS // GQA_KV_HEADS
    bkv = bsz * GQA_KV_HEADS

    for l in range(depth):
        need_ctx = l < depth - 1
        i = l // 2
        mod_l = mod[l].reshape(8, 1, 6 * d)
        if l % 2 == 0:
            wq, wk, wv, wn = GQA_HEADS * HEAD_DIM, GQA_KV_HEADS * HEAD_DIM, GQA_KV_HEADS * HEAD_DIM, NA_HEADS * HEAD_DIM
            starts = np.cumsum([0, wq, wk, wv, wn, wn])
            segs = [(int(starts[0]), wq, "norm_rope", 1.0), (int(starts[1]), wk, "norm_rope", 1.0),
                    (int(starts[2]), wv, "plain", 1.0), (int(starts[3]), wn, "norm", 1.0),
                    (int(starts[4]), wn, "norm", 1.0), (int(starts[5]), wn, "plain", 1.0)]
            gain = jnp.concatenate([jnp.tile(a_q_norm[i], GQA_HEADS) * att_scale, jnp.tile(a_k_norm[i], GQA_KV_HEADS),
                                    jnp.ones((wv,), F32), jnp.tile(b_q_norm[i], NA_HEADS) * att_scale,
                                    jnp.tile(b_k_norm[i], NA_HEADS), jnp.ones((wn,), F32)]).reshape(1, -1)
            qa, ka, va, qb, kb, vb = _proj(x_all, mod_l, norm1_g[l], w_in_even[i].astype(BF16), gain, cos_a, sin_a,
                                           segs, [BF16] * 6, tpb, bsz)
            k_lat = _heads_major(lat(ka), bsz, s, GQA_KV_HEADS)
            k_cx = _heads_major(cx(ka), bsz, cl, GQA_KV_HEADS)
            v_lat = _heads_major(lat(va), bsz, s, GQA_KV_HEADS)
            v_cx = _heads_major(cx(va), bsz, cl, GQA_KV_HEADS)
            k_all = jnp.concatenate([k_lat, k_cx], axis=2).reshape(bkv, s + cl, HEAD_DIM)
            v_all = jnp.concatenate([v_lat, v_cx], axis=2).reshape(bkv, s + cl, HEAD_DIM)
            qt = _heads_t(lat(qa), bsz, s, GQA_HEADS).reshape(bkv, grp, HEAD_DIM, s)
            oa_t = _flash(qt, k_all, v_all, Q_SUB)
            oa = oa_t.reshape(bsz, GQA_HEADS, HEAD_DIM, s).transpose(0, 3, 1, 2).reshape(nl, wq)
            bias = _na_bias(b_rpb[i], na_row_sel, na_col_in, na_dc_onehot)
            bh = bsz * NA_HEADS
            qn = _heads_major(lat(qb), bsz, s, NA_HEADS).reshape(bh, s, HEAD_DIM)
            kn = _heads_major(lat(kb), bsz, s, NA_HEADS).reshape(bh, s, HEAD_DIM)
            vn = _heads_major(lat(vb), bsz, s, NA_HEADS).reshape(bh, s, HEAD_DIM)
            kn_c = _heads_major(cx(kb), bsz, cl, NA_HEADS).reshape(bh, cl, HEAD_DIM)
            vn_c = _heads_major(cx(vb), bsz, cl, NA_HEADS).reshape(bh, cl, HEAD_DIM)
            ob = _na_attention(qn, kn, vn, kn_c, vn_c, bias, jnp.asarray(na_bases), jnp.asarray(na_var), NA_HEADS)
            ob = ob.reshape(bsz, NA_HEADS, s, HEAD_DIM).transpose(0, 2, 1, 3).reshape(nl, wn)
            if need_ctx:
                qt_c = _heads_t(cx(qa), bsz, cl, GQA_HEADS).reshape(bkv, grp, HEAD_DIM, cl)
                oa_c = _flash(qt_c, k_cx.reshape(bkv, cl, HEAD_DIM), v_cx.reshape(bkv, cl, HEAD_DIM), Q_SUB)
                oa_c = oa_c.reshape(bsz, GQA_HEADS, HEAD_DIM, cl).transpose(0, 3, 1, 2).reshape(bsz * cl, wq)
                qnt_c = _heads_t(cx(qb), bsz, cl, NA_HEADS).reshape(bh, 1, HEAD_DIM, cl)
                ob_c = _flash(qnt_c, kn_c, vn_c, Q_SUB)
                ob_c = ob_c.reshape(bsz, NA_HEADS, HEAD_DIM, cl).transpose(0, 3, 1, 2).reshape(bsz * cl, wn)
                a1 = jnp.concatenate([oa, oa_c], axis=0)
                a2 = jnp.concatenate([ob, ob_c], axis=0)
            else:
                a1, a2 = oa, ob
            w_out = w_out_even[i].astype(BF16)
        else:
            rw = RET_HEADS * RET_DK
            kr_cols = w_in_odd[i][:, 4 * rw + MLA_Q_LORA + MLA_KV_LORA:]
            w_ext = jnp.concatenate([w_in_odd[i]] + [kr_cols] * (LANES * 2 // MLA_ROPE - 1), axis=1).astype(BF16)
            mla_w = MLA_Q_LORA + MLA_KV_LORA + 2 * LANES
            segs = [(0, rw, "rope", RET_DK ** -0.5), (rw, rw, "rope", 1.0), (2 * rw, rw, "plain", 1.0),
                    (3 * rw, rw, "plain", 1.0), (4 * rw, mla_w, "plain", 1.0)]
            gain = jnp.ones((1, w_ext.shape[1]), F32)
            rq, rk, rv, rg, mla_in = _proj(x_all, mod_l, norm1_g[l], w_ext, gain, cos_c, sin_c, segs,
                                           [BF16, BF16, BF16, F32, F32], tpb, bsz)
            dmask, qdec, kdec, cdec = _retention_tables(ret_decay[i])
            yf, yb = _retention(rq, rk, rv, dmask, qdec, kdec, cdec, bsz, s, cl)
            a1 = _ret_finish(yf, yb, rg, ret_gn[i].reshape(1, rw))
            dqk = MLA_NOPE + MLA_ROPE
            perm_q = np.concatenate([np.arange(h * dqk, h * dqk + MLA_NOPE) for h in range(MLA_HEADS)]
                                    + [np.arange(h * dqk + MLA_NOPE, (h + 1) * dqk) for h in range(MLA_HEADS)])
            dkv = MLA_NOPE + MLA_V
            perm_kv = np.concatenate([np.arange(h * dkv, h * dkv + MLA_NOPE) for h in range(MLA_HEADS)]
                                     + [np.arange(h * dkv + MLA_NOPE, (h + 1) * dkv) for h in range(MLA_HEADS)])
            mla_scale = dqk ** -0.5 * LOG2E
            gq = jnp.concatenate([jnp.tile(mla_q_norm[i][:MLA_NOPE], MLA_HEADS),
                                  jnp.tile(mla_q_norm[i][MLA_NOPE:], MLA_HEADS)]).reshape(1, -1) * mla_scale
            gkn = mla_k_norm[i][:MLA_NOPE].reshape(1, -1)
            gkr = jnp.tile(mla_k_norm[i][MLA_NOPE:], LANES // MLA_ROPE).reshape(1, -1)
            q_m, k_m, v_m = _mla_proj(mla_in, w_uq[i][:, perm_q].astype(BF16), w_ukv[i][:, perm_kv].astype(BF16),
                                      mla_cq_norm[i].reshape(1, -1), mla_ckv_norm[i].reshape(1, -1), gq, gkn, gkr,
                                      cos_d, sin_d, tpb, bsz)
            nw = MLA_HEADS * MLA_NOPE

            def qk_heads(t, length, transposed):
                nope = t[:, :nw].reshape(bsz, length, MLA_HEADS, MLA_NOPE)
                rope = t[:, nw:].reshape(bsz, length, MLA_HEADS, MLA_ROPE)
                full = jnp.concatenate([nope, rope], axis=-1)
                return full.transpose(0, 2, 3, 1) if transposed else full.transpose(0, 2, 1, 3)

            bhm = bsz * MLA_HEADS
            k_lat = qk_heads(lat(k_m), s, False)
            k_cx = qk_heads(cx(k_m), cl, False)
            v_lat = _heads_major(lat(v_m), bsz, s, MLA_HEADS)
            v_cx = _heads_major(cx(v_m), bsz, cl, MLA_HEADS)
            k_all = jnp.concatenate([k_lat, k_cx], axis=2).reshape(bhm, s + cl, dqk)
            v_all = jnp.concatenate([v_lat, v_cx], axis=2).reshape(bhm, s + cl, MLA_V)
            qt = qk_heads(lat(q_m), s, True).reshape(bhm, 1, dqk, s)
            om = _flash(qt, k_all, v_all, 4 * Q_SUB)
            om = om.reshape(bsz, MLA_HEADS, MLA_V, s).transpose(0, 3, 1, 2).reshape(nl, MLA_HEADS * MLA_V)
            if need_ctx:
                qt_c = qk_heads(cx(q_m), cl, True).reshape(bhm, 1, dqk, cl)
                om_c = _flash(qt_c, k_cx.reshape(bhm, cl, dqk), v_cx.reshape(bhm, cl, MLA_V), Q_SUB)
                om_c = om_c.reshape(bsz, MLA_HEADS, MLA_V, cl).transpose(0, 3, 1, 2).reshape(bsz * cl, MLA_HEADS * MLA_V)
                a2 = jnp.concatenate([om, om_c], axis=0)
            else:
                a1 = a1[:nl]
                a2 = om
            w_out = w_out_odd[i].astype(BF16)

        n_rows = n_all if need_ctx else nl
        w_r = jnp.zeros((d, LANES), F32).at[:, :N_EXPERTS].set(w_router[l])
        b_r = jnp.full((1, LANES), NEG, F32).at[0, :N_EXPERTS].set(b_router[l])
        x_new, m, top_e, gates = _out_proj(a1, a2, x_all, w_out, mod_l, norm2_g[l], w_r, b_r, n_rows, tpb, bsz)
        x_all = _moe(m, top_e, gates, x_new, mod_l, w_exp1[l].astype(BF16), b_exp1[l], w_exp2[l].astype(BF16),
                     b_exp2[l], tpb, bsz)
    return x_all[:nl].reshape(bsz, s, d)
```

```python
import functools
import math

import numpy as np
import jax
import jax.numpy as jnp
from jax import lax
from jax.experimental import pallas as pl
from jax.experimental.pallas import tpu as pltpu

F32 = jnp.float32
BF16 = jnp.bfloat16

GRID_W = 64
HEAD_DIM = 64
GQA_HEADS = 8
GQA_KV_HEADS = 2
NA_HEADS = 8
NA_ROWS = 8
NA_COLS = 16
RET_HEADS = 4
RET_DK = 128
MLA_HEADS = 4
MLA_Q_LORA = 256
MLA_KV_LORA = 128
MLA_NOPE = 128
MLA_ROPE = 64
MLA_V = 128
N_EXPERTS = 32
TOP_K = 4
SWIGLU_LIMIT = 7.0
SWIGLU_ALPHA = 1.702
ROPE_THETA = 10000.0
EPS = 1e-6
GN_EPS = 1e-5
LOG2E = math.log2(math.e)
NEG = -1e30

LANES = 128
ROW_TILE = 512
Q_SUB = 256
ONES_ROWS = 16
NA_QB = 256
NA_WIN_ROWS = 12
RET_CHUNK = 256
MOE_BM = 512
TOK_ROWS = 8
VMEM_LIMIT = 56 * 1024 * 1024


def _cp(sem):
    return pltpu.CompilerParams(dimension_semantics=sem, vmem_limit_bytes=VMEM_LIMIT)


def _mod_kernel(c_ref, w_ref, b_ref, o_ref):
    c = c_ref[...]
    s = c * jax.nn.sigmoid(c)
    o_ref[0] = jnp.dot(s, w_ref[0], precision=lax.Precision.HIGHEST, preferred_element_type=F32) + b_ref[0]


def _mod_vectors(c_rows, w_mod, b_mod):
    depth, d, d6 = w_mod.shape
    tn = 1536
    return pl.pallas_call(
        _mod_kernel,
        grid=(depth, d6 // tn),
        in_specs=[pl.BlockSpec((8, d), lambda l, j: (0, 0)),
                  pl.BlockSpec((1, d, tn), lambda l, j: (l, 0, j)),
                  pl.BlockSpec((1, 1, tn), lambda l, j: (l, 0, j))],
        out_specs=pl.BlockSpec((1, 8, tn), lambda l, j: (l, 0, j)),
        out_shape=jax.ShapeDtypeStruct((depth, 8, d6), F32),
        compiler_params=_cp(("parallel", "parallel")),
    )(c_rows, w_mod, b_mod.reshape(depth, 1, d6))


def _modulated_norm(x, g, sc, sh):
    ms = jnp.mean(x * x, axis=-1, keepdims=True)
    return x * lax.rsqrt(ms + EPS) * g * (1.0 + sc) + sh


def _pair_rope(y, cos, sin_signed):
    lane = lax.broadcasted_iota(jnp.int32, y.shape, 1)
    partner = jnp.where((lane & 1) == 0, pltpu.roll(y, LANES - 1, 1), pltpu.roll(y, 1, 1))
    return y * cos + partner * sin_signed


def _store_token_tiles(ref, val):
    n = val.shape[0]
    for j in range(TOK_ROWS):
        ref[pl.ds(j, n, stride=TOK_ROWS), :] = val[:, j * LANES:(j + 1) * LANES]


def _load_token_tiles(ref, n):
    return jnp.concatenate([ref[pl.ds(j, n, stride=TOK_ROWS), :] for j in range(TOK_ROWS)], axis=-1)


def _split_dot(a_f32, w_bf16):
    hi = a_f32.astype(BF16)
    lo = (a_f32 - hi.astype(F32)).astype(BF16)
    return (jnp.dot(hi, w_bf16, preferred_element_type=F32) + jnp.dot(lo, w_bf16, preferred_element_type=F32))


def _proj_kernel(x_ref, sh_ref, sc_ref, g_ref, w_ref, gain_ref, bd_ref, cos_ref, sin_ref, *out_refs, segs):
    a = _modulated_norm(x_ref[...], g_ref[...], sc_ref[0], sh_ref[0]).astype(BF16)
    cos = cos_ref[...]
    sin = sin_ref[...]
    for (start, width, mode, scale), o_ref in zip(segs, out_refs):
        y_seg = jnp.dot(a, w_ref[:, start:start + width], preferred_element_type=F32)
        if mode == "plain":
            o_ref[...] = y_seg.astype(o_ref.dtype)
            continue
        for j in range(width // LANES):
            y = y_seg[:, j * LANES:(j + 1) * LANES]
            if "norm" in mode:
                ms = _split_dot(y * y, bd_ref[...])
                y = y * lax.rsqrt(ms + EPS) * gain_ref[:, start + j * LANES:start + (j + 1) * LANES]
            if scale != 1.0:
                y = y * scale
            if "rope" in mode:
                y = _pair_rope(y, cos, sin)
            o_ref[:, j * LANES:(j + 1) * LANES] = y.astype(o_ref.dtype)


def _proj(x_all, mod_l, norm_g, w, gain, cos_t, sin_t, segs, out_dtypes, n_lat_tiles_per_batch, n_batch):
    n, d = x_all.shape
    tm = ROW_TILE
    n_tiles = n // tm
    wtot = w.shape[1]
    tpb = n_lat_tiles_per_batch
    n_lat_tiles = tpb * n_batch

    def mod_row(t):
        return jnp.where(t < n_lat_tiles, t // tpb, n_batch)

    def rope_row(t):
        return jnp.where(t < n_lat_tiles, t % tpb, tpb)

    bd = np.kron(np.eye(2, dtype=np.float32), np.full((HEAD_DIM, HEAD_DIM), 1.0 / HEAD_DIM, np.float32))
    in_specs = [
        pl.BlockSpec((tm, d), lambda t: (t, 0)),
        pl.BlockSpec((1, 1, d), lambda t: (mod_row(t), 0, 0)),
        pl.BlockSpec((1, 1, d), lambda t: (mod_row(t), 0, 1)),
        pl.BlockSpec((1, d), lambda t: (0, 0)),
        pl.BlockSpec((d, wtot), lambda t: (0, 0)),
        pl.BlockSpec((1, wtot), lambda t: (0, 0)),
        pl.BlockSpec((LANES, LANES), lambda t: (0, 0)),
        pl.BlockSpec((tm, LANES), lambda t: (rope_row(t), 0)),
        pl.BlockSpec((tm, LANES), lambda t: (rope_row(t), 0)),
    ]
    out_specs = [pl.BlockSpec((tm, s[1]), lambda t: (t, 0)) for s in segs]
    out_shape = [jax.ShapeDtypeStruct((n, s[1]), dt) for s, dt in zip(segs, out_dtypes)]
    return pl.pallas_call(
        functools.partial(_proj_kernel, segs=tuple(segs)),
        grid=(n_tiles,),
        in_specs=in_specs,
        out_specs=out_specs,
        out_shape=out_shape,
        compiler_params=_cp(("parallel",)),
    )(x_all, mod_l, mod_l, norm_g.reshape(1, d), w, gain, jnp.asarray(bd, BF16), cos_t, sin_t)


def _mla_proj_kernel(x_ref, wuq_ref, wukv_ref, gcq_ref, gckv_ref, gq_ref, gkn_ref, gkr_ref, cos_ref, sin_ref,
                     q_ref, k_ref, v_ref):
    x = x_ref[...]
    cq = x[:, :MLA_Q_LORA]
    ckv = x[:, MLA_Q_LORA:MLA_Q_LORA + MLA_KV_LORA]
    kr = x[:, MLA_Q_LORA + MLA_KV_LORA:MLA_Q_LORA + MLA_KV_LORA + LANES]
    cos = cos_ref[...]
    sin = sin_ref[...]
    cqn = cq * lax.rsqrt(jnp.mean(cq * cq, axis=-1, keepdims=True) + EPS) * gcq_ref[...]
    ckvn = ckv * lax.rsqrt(jnp.mean(ckv * ckv, axis=-1, keepdims=True) + EPS) * gckv_ref[...]
    q = jnp.dot(cqn.astype(BF16), wuq_ref[...], preferred_element_type=F32)
    kv = jnp.dot(ckvn.astype(BF16), wukv_ref[...], preferred_element_type=F32)
    nh = MLA_HEADS
    d_qk = float(MLA_NOPE + MLA_ROPE)
    lane = lax.broadcasted_iota(jnp.int32, (1, LANES), 1)
    low = lane < MLA_ROPE

    def half_sums(slab):
        sq = slab * slab
        a = jnp.sum(jnp.where(low, sq, 0.0), axis=-1, keepdims=True)
        return a, jnp.sum(sq, axis=-1, keepdims=True) - a

    rope_w = nh * MLA_NOPE
    q_rope_ss = []
    for r in range(nh // 2):
        q_rope_ss.extend(half_sums(q[:, rope_w + r * LANES:rope_w + (r + 1) * LANES]))
    kr_ss, _ = half_sums(kr)
    rs_q, rs_k = [], []
    for h in range(nh):
        qn = q[:, h * MLA_NOPE:(h + 1) * MLA_NOPE]
        kn = kv[:, h * MLA_NOPE:(h + 1) * MLA_NOPE]
        rs_q.append(lax.rsqrt((jnp.sum(qn * qn, axis=-1, keepdims=True) + q_rope_ss[h]) / d_qk + EPS))
        rs_k.append(lax.rsqrt((jnp.sum(kn * kn, axis=-1, keepdims=True) + kr_ss) / d_qk + EPS))
        q_ref[:, h * MLA_NOPE:(h + 1) * MLA_NOPE] = (qn * rs_q[h] * gq_ref[:, h * MLA_NOPE:(h + 1) * MLA_NOPE]).astype(q_ref.dtype)
        k_ref[:, h * MLA_NOPE:(h + 1) * MLA_NOPE] = (kn * rs_k[h] * gkn_ref[...]).astype(k_ref.dtype)
    kr_rot = _pair_rope(kr * gkr_ref[...], cos, sin)
    for r in range(nh // 2):
        sl = slice(rope_w + r * LANES, rope_w + (r + 1) * LANES)
        yq = q[:, sl] * jnp.where(low, rs_q[2 * r], rs_q[2 * r + 1]) * gq_ref[:, sl]
        q_ref[:, sl] = _pair_rope(yq, cos, sin).astype(q_ref.dtype)
        k_ref[:, sl] = (kr_rot * jnp.where(low, rs_k[2 * r], rs_k[2 * r + 1])).astype(k_ref.dtype)
    v_ref[...] = kv[:, nh * MLA_NOPE:].astype(v_ref.dtype)


def _mla_proj(mla_in, wuq, wukv, gcq, gckv, gq, gkn, gkr, cos_t, sin_t, tpb, n_batch):
    n, win = mla_in.shape
    tm = ROW_TILE
    n_lat_tiles = tpb * n_batch

    def rope_row(t):
        return jnp.where(t < n_lat_tiles, t % tpb, tpb)

    qk_w = MLA_HEADS * (MLA_NOPE + MLA_ROPE)
    v_w = MLA_HEADS * MLA_V
    full = lambda a: pl.BlockSpec(a.shape, lambda t: (0,) * a.ndim)
    return pl.pallas_call(
        _mla_proj_kernel,
        grid=(n // tm,),
        in_specs=[pl.BlockSpec((tm, win), lambda t: (t, 0)), full(wuq), full(wukv), full(gcq), full(gckv),
                  full(gq), full(gkn), full(gkr),
                  pl.BlockSpec((tm, LANES), lambda t: (rope_row(t), 0)),
                  pl.BlockSpec((tm, LANES), lambda t: (rope_row(t), 0))],
        out_specs=[pl.BlockSpec((tm, qk_w), lambda t: (t, 0)), pl.BlockSpec((tm, qk_w), lambda t: (t, 0)),
                   pl.BlockSpec((tm, v_w), lambda t: (t, 0))],
        out_shape=[jax.ShapeDtypeStruct((n, qk_w), BF16), jax.ShapeDtypeStruct((n, qk_w), BF16),
                   jax.ShapeDtypeStruct((n, v_w), BF16)],
        compiler_params=_cp(("parallel",)),
    )(mla_in, wuq, wukv, gcq, gckv, gq, gkn, gkr, cos_t, sin_t)


def _flash_kernel(qt_ref, k_ref, vt_ref, ot_ref, q_sc, s_a, s_b, mx_a, mx_b, m_sc, acc_sc, *, g, n_chunks, dv):
    tq = qt_ref.shape[3]
    for gi in range(g):
        q_sc[:, gi * tq:(gi + 1) * tq] = qt_ref[0, gi]
    m_sc[...] = jnp.full(m_sc.shape, NEG, F32)
    acc_sc[...] = jnp.zeros(acc_sc.shape, F32)

    def scores(ci, s_ref, mx_ref):
        s = jnp.dot(k_ref[0, ci], q_sc[...], preferred_element_type=F32)
        s_ref[...] = s
        mx_ref[...] = jnp.max(s, axis=0, keepdims=True)

    def accumulate(ci, s_ref, mx_ref):
        m_old = m_sc[...]
        m_new = jnp.maximum(m_old, mx_ref[...])
        alpha = jnp.exp2(m_old - m_new)
        p = jnp.exp2(s_ref[...] - m_new).astype(BF16)
        acc_sc[...] = alpha * acc_sc[...] + jnp.dot(vt_ref[0, ci], p, preferred_element_type=F32)
        m_sc[...] = m_new

    scores(0, s_a, mx_a)
    n_pairs = (n_chunks - 1) // 2

    def pair(j, carry):
        ci = 2 * j
        scores(ci + 1, s_b, mx_b)
        accumulate(ci, s_a, mx_a)
        scores(ci + 2, s_a, mx_a)
        accumulate(ci + 1, s_b, mx_b)
        return carry

    lax.fori_loop(0, n_pairs, pair, 0)
    if (n_chunks - 1) % 2 == 0:
        accumulate(n_chunks - 1, s_a, mx_a)
    else:
        scores(n_chunks - 1, s_b, mx_b)
        accumulate(n_chunks - 2, s_a, mx_a)
        accumulate(n_chunks - 1, s_b, mx_b)
    acc = acc_sc[...]
    o = acc[:dv] / acc[dv:dv + 1]
    for gi in range(g):
        ot_ref[0, gi] = o[:, gi * tq:(gi + 1) * tq].astype(ot_ref.dtype)


def _key_chunk(lk):
    for tk in (1280, 1024, 768, 512, 256):
        if lk % tk == 0:
            return tk
    raise ValueError(f"key length {lk} must be a multiple of 256")


def _flash(qt, k, v, tq_blk):
    bk, g, dq, lq = qt.shape
    lk, dv = v.shape[1], v.shape[2]
    tk = _key_chunk(lk)
    nch = lk // tk
    kc = k.reshape(bk, nch, tk, dq)
    dve = dv + ONES_ROWS
    vt = jnp.concatenate([v, jnp.ones((bk, lk, ONES_ROWS), v.dtype)], axis=-1)
    vt = vt.reshape(bk, nch, tk, dve).transpose(0, 1, 3, 2)
    tq_blk = min(tq_blk, lq)
    assert lq % tq_blk == 0 and tq_blk % LANES == 0
    w = g * tq_blk
    return pl.pallas_call(
        functools.partial(_flash_kernel, g=g, n_chunks=nch, dv=dv),
        grid=(bk, lq // tq_blk),
        in_specs=[pl.BlockSpec((1, g, dq, tq_blk), lambda b, i: (b, 0, 0, i)),
                  pl.BlockSpec((1, nch, tk, dq), lambda b, i: (b, 0, 0, 0)),
                  pl.BlockSpec((1, nch, dve, tk), lambda b, i: (b, 0, 0, 0))],
        out_specs=pl.BlockSpec((1, g, dv, tq_blk), lambda b, i: (b, 0, 0, i)),
        out_shape=jax.ShapeDtypeStruct((bk, g, dv, lq), BF16),
        scratch_shapes=[pltpu.VMEM((dq, w), BF16), pltpu.VMEM((tk, w), F32), pltpu.VMEM((tk, w), F32),
                        pltpu.VMEM((1, w), F32), pltpu.VMEM((1, w), F32), pltpu.VMEM((1, w), F32),
                        pltpu.VMEM((dve, w), F32)],
        compiler_params=_cp(("parallel", "parallel")),
    )(qt, kc, vt)


def _na_tables(seq_len):
    rows_n = seq_len // GRID_W
    assert rows_n >= NA_WIN_ROWS and NA_ROWS <= rows_n
    nb = seq_len // NA_QB
    rpq = NA_QB // GRID_W
    band = NA_WIN_ROWS * GRID_W
    variants, var_id, bases = {}, [], []
    for j in range(nb):
        base = int(np.clip(rpq * j - NA_ROWS // 2, 0, rows_n - NA_WIN_ROWS))
        bases.append(base)
        t = np.arange(NA_QB) + j * NA_QB
        r, col = t // GRID_W, t % GRID_W
        r0 = np.clip(r - NA_ROWS // 2, 0, rows_n - NA_ROWS)
        c0 = np.clip(col - NA_COLS // 2, 0, GRID_W - NA_COLS)
        kk = np.arange(band)
        kr = base + kk // GRID_W
        kc = kk % GRID_W
        inside = ((kr[None] >= r0[:, None]) & (kr[None] < r0[:, None] + NA_ROWS)
                  & (kc[None] >= c0[:, None]) & (kc[None] < c0[:, None] + NA_COLS))
        rel = (kr[None] - r[:, None] + NA_ROWS - 1) * (2 * NA_COLS - 1) + (kc[None] - col[:, None] + NA_COLS - 1)
        tab = np.where(inside, rel, -1).astype(np.int32)
        assert (inside.sum(axis=1) == NA_ROWS * NA_COLS).all()
        key = tab.tobytes()
        if key not in variants:
            variants[key] = (len(variants), tab)
        var_id.append(variants[key][0])
    tabs = np.stack([v[1] for v in sorted(variants.values(), key=lambda kv: kv[0])])
    n_dr = 2 * NA_ROWS - 1
    n_dc = 2 * NA_COLS - 1
    col = np.arange(GRID_W)
    c0 = np.clip(col - NA_COLS // 2, 0, GRID_W - NA_COLS)
    col_in = (col[None] >= c0[:, None]) & (col[None] < c0[:, None] + NA_COLS)
    dc = col[None] - col[:, None] + NA_COLS - 1
    t5 = tabs.reshape(len(tabs), rpq, GRID_W, NA_WIN_ROWS, GRID_W)
    row_sel = np.full((len(tabs), rpq, NA_WIN_ROWS), n_dr, np.int32)
    for v in range(len(tabs)):
        for a in range(rpq):
            for i in range(NA_WIN_ROWS):
                blk = t5[v, a, :, i, :]
                if (blk >= 0).any():
                    dr = int(blk[blk >= 0][0]) // n_dc
                    assert (np.where(col_in, dr * n_dc + dc, -1) == blk).all()
                    row_sel[v, a, i] = dr
                else:
                    assert (blk < 0).all()
    dc_onehot = (dc[None] == np.arange(n_dc)[:, None, None]).astype(np.float32)
    return np.asarray(bases, np.int32), np.asarray(var_id, np.int32), row_sel, col_in, dc_onehot


def _na_bias(rpb, row_sel, col_in, dc_onehot):
    h = rpb.shape[0]
    n_dr, n_dc = 2 * NA_ROWS - 1, 2 * NA_COLS - 1
    t = jnp.einsum("hdj,jck->hdck", rpb.astype(F32).reshape(h, n_dr, n_dc) * LOG2E, jnp.asarray(dc_onehot),
                   precision=lax.Precision.HIGHEST)
    t = jnp.where(jnp.asarray(col_in)[None, None], t, NEG)
    t = jnp.concatenate([t, jnp.full((h, 1, GRID_W, GRID_W), NEG, F32)], axis=1)
    nv, rpq, nw = row_sel.shape
    b = t[:, jnp.asarray(row_sel)]
    return b.transpose(1, 0, 2, 4, 3, 5).reshape(nv, h, rpq * GRID_W, nw * GRID_W)


def _na_kernel(base_ref, var_ref, q_ref, k_ref, v_ref, kc_ref, vc_ref, bias_ref, o_ref):
    j = pl.program_id(1)
    band = NA_WIN_ROWS * GRID_W
    start = pl.multiple_of(base_ref[j] * GRID_W, GRID_W)
    q = q_ref[0]
    kw = k_ref[0, pl.ds(start, band), :]
    vw = v_ref[0, pl.ds(start, band), :]
    nt = (((1,), (1,)), ((), ()))
    s_win = lax.dot_general(q, kw, nt, preferred_element_type=F32) + bias_ref[0, 0]
    s_ctx = lax.dot_general(q, kc_ref[0], nt, preferred_element_type=F32)
    m = jnp.maximum(jnp.max(s_win, axis=-1, keepdims=True), jnp.max(s_ctx, axis=-1, keepdims=True))
    p_win = jnp.exp2(s_win - m)
    p_ctx = jnp.exp2(s_ctx - m)
    l = jnp.sum(p_win, axis=-1, keepdims=True) + jnp.sum(p_ctx, axis=-1, keepdims=True)
    o = (jnp.dot(p_win.astype(BF16), vw, preferred_element_type=F32)
         + jnp.dot(p_ctx.astype(BF16), vc_ref[0], preferred_element_type=F32))
    o_ref[0] = (o / l).astype(o_ref.dtype)


def _na_attention(q, k, v, k_ctx, v_ctx, bias, bases, var_id, n_heads):
    bh, s, d = q.shape
    cl = k_ctx.shape[1]
    band = NA_WIN_ROWS * GRID_W
    nb = s // NA_QB
    grid_spec = pltpu.PrefetchScalarGridSpec(
        num_scalar_prefetch=2,
        grid=(bh, nb),
        in_specs=[pl.BlockSpec((1, NA_QB, d), lambda b, j, bs, vr: (b, j, 0)),
                  pl.BlockSpec((1, s, d), lambda b, j, bs, vr: (b, 0, 0)),
                  pl.BlockSpec((1, s, d), lambda b, j, bs, vr: (b, 0, 0)),
                  pl.BlockSpec((1, cl, d), lambda b, j, bs, vr: (b, 0, 0)),
                  pl.BlockSpec((1, cl, d), lambda b, j, bs, vr: (b, 0, 0)),
                  pl.BlockSpec((1, 1, NA_QB, band), lambda b, j, bs, vr: (vr[j], b % n_heads, 0, 0))],
        out_specs=pl.BlockSpec((1, NA_QB, d), lambda b, j, bs, vr: (b, j, 0)),
    )
    return pl.pallas_call(
        _na_kernel,
        grid_spec=grid_spec,
        out_shape=jax.ShapeDtypeStruct((bh, s, d), BF16),
        compiler_params=_cp(("parallel", "arbitrary")),
    )(bases, var_id, q, k, v, k_ctx, v_ctx, bias)


def _ret_kernel(cdec_ref, qf_ref, kf_ref, vf_ref, qb_ref, kb_ref, vb_ref, dmask_ref, qdec_ref, kdec_ref,
                yf_ref, yb_ref, state_sc):
    @pl.when(pl.program_id(1) == 0)
    def _():
        state_sc[...] = jnp.zeros(state_sc.shape, F32)

    nt = (((1,), (1,)), ((), ()))
    tn = (((0,), (0,)), ((), ()))
    dk = RET_DK
    for d, (q_ref, k_ref, v_ref, y_ref) in enumerate(((qf_ref, kf_ref, vf_ref, yf_ref),
                                                      (qb_ref, kb_ref, vb_ref, yb_ref))):
        for h in range(RET_HEADS):
            sl = slice(h * dk, (h + 1) * dk)
            q = q_ref[:, sl]
            k = k_ref[:, sl]
            v = v_ref[:, sl]
            st = state_sc[d, h]
            a = lax.dot_general(q, k, nt, preferred_element_type=F32) * dmask_ref[d, h]
            inner = jnp.dot(a.astype(BF16), v, preferred_element_type=F32)
            cross = jnp.dot(q, st.astype(BF16), preferred_element_type=F32) * qdec_ref[d, h]
            y_ref[:, sl] = inner + cross
            vs = (v.astype(F32) * kdec_ref[d, h]).astype(BF16)
            state_sc[d, h] = st * cdec_ref[d * RET_HEADS + h] + lax.dot_general(k, vs, tn, preferred_element_type=F32)


def _retention(rq, rk, rv, dmask, qdec, kdec, cdec, n_batch, seq_len, ctx_len):
    n, w = rq.shape
    c = RET_CHUNK
    assert ctx_len == c and seq_len % c == 0
    ncl = seq_len // c
    ctx_blk0 = (n_batch * seq_len) // c

    def fwd(b, s, cd):
        return (jnp.where(s == 0, ctx_blk0 + b, b * ncl + s - 1), 0)

    def bwd(b, s, cd):
        return (jnp.where(s == 0, ctx_blk0 + b, b * ncl + ncl - s), 0)

    full = lambda a: pl.BlockSpec(a.shape, lambda b, s, cd: (0,) * a.ndim)
    grid_spec = pltpu.PrefetchScalarGridSpec(
        num_scalar_prefetch=1,
        grid=(n_batch, ncl + 1),
        in_specs=[pl.BlockSpec((c, w), fwd)] * 3 + [pl.BlockSpec((c, w), bwd)] * 3 + [full(dmask), full(qdec), full(kdec)],
        out_specs=[pl.BlockSpec((c, w), fwd), pl.BlockSpec((c, w), bwd)],
        scratch_shapes=[pltpu.VMEM((2, RET_HEADS, RET_DK, RET_DK), F32)],
    )
    return pl.pallas_call(
        _ret_kernel,
        grid_spec=grid_spec,
        out_shape=[jax.ShapeDtypeStruct((n, w), F32)] * 2,
        compiler_params=_cp(("parallel", "arbitrary")),
    )(cdec, rq, rk, rv, rq, rk, rv, dmask, qdec, kdec)


def _ret_finish_kernel(yf_ref, yb_ref, rg_ref, gn_ref, o_ref):
    y = yf_ref[...] + yb_ref[...]
    gate = rg_ref[...]
    gate = gate * jax.nn.sigmoid(gate)
    for h in range(RET_HEADS):
        sl = slice(h * RET_DK, (h + 1) * RET_DK)
        yh = y[:, sl]
        mu = jnp.mean(yh, axis=-1, keepdims=True)
        var = jnp.mean(jnp.square(yh - mu), axis=-1, keepdims=True)
        o_ref[:, sl] = ((yh - mu) * lax.rsqrt(var + GN_EPS) * gn_ref[:, sl] * gate[:, sl]).astype(o_ref.dtype)


def _ret_finish(yf, yb, rg, gn):
    n, w = yf.shape
    tm = ROW_TILE
    spec = pl.BlockSpec((tm, w), lambda t: (t, 0))
    return pl.pallas_call(
        _ret_finish_kernel,
        grid=(n // tm,),
        in_specs=[spec, spec, spec, pl.BlockSpec((1, w), lambda t: (0, 0))],
        out_specs=spec,
        out_shape=jax.ShapeDtypeStruct((n, w), BF16),
        compiler_params=_cp(("parallel",)),
    )(yf, yb, rg, gn)


def _out_kernel(a1_ref, a2_ref, x_ref, w_ref, g1_ref, ng_ref, sh_ref, sc_ref, wr_ref, br_ref,
                xo_ref, m_ref, e_ref, gt_ref):
    half = a1_ref.shape[1]
    o = (jnp.dot(a1_ref[...], w_ref[:half, :], preferred_element_type=F32)
         + jnp.dot(a2_ref[...], w_ref[half:, :], preferred_element_type=F32))
    x = x_ref[...] + g1_ref[0] * o
    xo_ref[...] = x
    m = _modulated_norm(x, ng_ref[...], sc_ref[0], sh_ref[0])
    _store_token_tiles(m_ref, m)
    logits =jnp.dot(m, wr_ref[...], precision=lax.Precision.HIGHEST, preferred_element_type=F32) + br_ref[...]
    lane = lax.broadcasted_iota(jnp.int32, logits.shape, 1).astype(F32)
    e_out = jnp.zeros(logits.shape, F32)
    g_out = jnp.zeros(logits.shape, F32)
    top0 = None
    denom = None
    for kk in range(TOP_K):
        mx = jnp.max(logits, axis=-1, keepdims=True)
        idx = jnp.min(jnp.where(logits == mx, lane, float(LANES)), axis=-1, keepdims=True)
        if kk == 0:
            top0 = mx
            ex = jnp.ones_like(mx)
            denom = ex
        else:
            ex = jnp.exp(mx - top0)
            denom = denom + ex
        e_out = jnp.where(lane == kk, idx, e_out)
        g_out = jnp.where(lane == kk, ex, g_out)
        logits = jnp.where(lane == idx, NEG * 2.0, logits)
    e_ref[...] = e_out.astype(jnp.int32)
    gt_ref[...] = g_out / denom


def _out_proj(a1, a2, x_all, w_out, mod_l, norm2_g, w_r, b_r, n_rows, tpb, n_batch):
    d = x_all.shape[1]
    half = a1.shape[1]
    tm = ROW_TILE
    n_lat_tiles = tpb * n_batch

    def mod_row(t):
        return jnp.where(t < n_lat_tiles, t // tpb, n_batch)

    row = lambda wd: pl.BlockSpec((tm, wd), lambda t: (t, 0))
    modspec = lambda col: pl.BlockSpec((1, 1, d), lambda t: (mod_row(t), 0, col))
    return pl.pallas_call(
        _out_kernel,
        grid=(n_rows // tm,),
        in_specs=[row(half), row(half), row(d), pl.BlockSpec((2 * half, d), lambda t: (0, 0)),
                  modspec(2), pl.BlockSpec((1, d), lambda t: (0, 0)), modspec(3), modspec(4),
                  pl.BlockSpec((d, LANES), lambda t: (0, 0)), pl.BlockSpec((1, LANES), lambda t: (0, 0))],
        out_specs=[row(d), pl.BlockSpec((tm * TOK_ROWS, LANES), lambda t: (t, 0)), row(LANES), row(LANES)],
        out_shape=[jax.ShapeDtypeStruct((n_rows, d), F32), jax.ShapeDtypeStruct((n_rows * TOK_ROWS, LANES), F32),
                   jax.ShapeDtypeStruct((n_rows, LANES), jnp.int32), jax.ShapeDtypeStruct((n_rows, LANES), F32)],
        compiler_params=_cp(("parallel",)),
    )(a1, a2, x_all, w_out, mod_l, norm2_g.reshape(1, d), mod_l, mod_l, w_r, b_r)


def _token_rows(ref, idx):
    return ref.at[pl.ds(pl.multiple_of(idx * TOK_ROWS, TOK_ROWS), TOK_ROWS)]


def _dispatch_kernel(pad_start_ref, pad_cnt_ref, pos_ref, m_ref, hs_ref, zero_sc, sem, pad_sem):
    n_tok = m_ref.shape[0] // TOK_ROWS

    @pl.when(pl.program_id(0) == 0)
    def _():
        zero_sc[...] = jnp.zeros(zero_sc.shape, zero_sc.dtype)

        def per_expert(e, carry):
            def fill(r, c):
                pltpu.make_async_copy(zero_sc, _token_rows(hs_ref, pad_start_ref[e] + r), pad_sem).start()
                return c

            lax.fori_loop(0, pad_cnt_ref[e], fill, 0)

            def fill_done(r, c):
                pltpu.make_async_copy(zero_sc, _token_rows(hs_ref, 0), pad_sem).wait()
                return c

            lax.fori_loop(0, pad_cnt_ref[e], fill_done, 0)
            return carry

        lax.fori_loop(0, N_EXPERTS, per_expert, 0)

    def issue(r, carry):
        for kk in range(TOP_K):
            pltpu.make_async_copy(_token_rows(m_ref, r), _token_rows(hs_ref, pos_ref[r * TOP_K + kk]), sem).start()
        return carry

    lax.fori_loop(0, n_tok, issue, 0)

    def drain(r, carry):
        for kk in range(TOP_K):
            pltpu.make_async_copy(_token_rows(m_ref, 0), _token_rows(hs_ref, 0), sem).wait()
        return carry

    lax.fori_loop(0, n_tok, drain, 0)


def _dispatch(m_tiles, pos, pad_start, pad_cnt, cap):
    n_tok = m_tiles.shape[0] // TOK_ROWS
    tm = ROW_TILE
    grid_spec = pltpu.PrefetchScalarGridSpec(
        num_scalar_prefetch=2,
        grid=(n_tok // tm,),
        in_specs=[pl.BlockSpec((tm * TOP_K,), lambda t, ps, pc: (t,), memory_space=pltpu.SMEM),
                  pl.BlockSpec((tm * TOK_ROWS, LANES), lambda t, ps, pc: (t, 0))],
        out_specs=pl.BlockSpec(memory_space=pl.ANY),
        scratch_shapes=[pltpu.VMEM((TOK_ROWS, LANES), F32), pltpu.SemaphoreType.DMA(()), pltpu.SemaphoreType.DMA(())],
    )
    return pl.pallas_call(
        _dispatch_kernel,
        grid_spec=grid_spec,
        out_shape=jax.ShapeDtypeStruct((cap * TOK_ROWS, LANES), F32),
        compiler_params=pltpu.CompilerParams(dimension_semantics=("arbitrary",), has_side_effects=True,
                                             vmem_limit_bytes=VMEM_LIMIT),
    )(pad_start, pad_cnt, pos, m_tiles)


def _expert_kernel(be_ref, nu_ref, x_ref, w1_ref, b1_ref, w2_ref, b2_ref, o_ref, w1_sc, w2_sc):
    i = pl.program_id(0)

    @pl.when((i == 0) | (be_ref[i] != be_ref[jnp.maximum(i - 1, 0)]))
    def _():
        w1_sc[...] = w1_ref[0].astype(BF16)
        w2_sc[...] = w2_ref[0].astype(BF16)

    @pl.when(i < nu_ref[0])
    def _():
        dff = w2_ref.shape[1]
        x = _load_token_tiles(x_ref, MOE_BM).astype(BF16)
        u = jnp.dot(x, w1_sc[...], preferred_element_type=F32) + b1_ref[0]
        gl = jnp.minimum(u[:, :dff], SWIGLU_LIMIT)
        up = jnp.clip(u[:, dff:], -SWIGLU_LIMIT, SWIGLU_LIMIT)
        act = gl * jax.nn.sigmoid(SWIGLU_ALPHA * gl) * (up + 1.0)
        _store_token_tiles(o_ref, jnp.dot(act.astype(BF16), w2_sc[...], preferred_element_type=F32) + b2_ref[0])

    @pl.when(i >= nu_ref[0])
    def _():
        o_ref[...] = jnp.zeros(o_ref.shape, o_ref.dtype)


def _experts(hs, blk_e, n_used, w1_all, b1, w2_all, b2, layer):
    _, ne, d, dff2 = w1_all.shape
    cap = hs.shape[0] // TOK_ROWS
    dff = dff2 // 2
    bm = MOE_BM
    tile_spec = pl.BlockSpec((bm * TOK_ROWS, LANES), lambda i, be, nu: (i, 0))
    grid_spec = pltpu.PrefetchScalarGridSpec(
        num_scalar_prefetch=2,
        grid=(cap // bm,),
        in_specs=[tile_spec,
                  pl.BlockSpec((None, 1, d, dff2), lambda i, be, nu: (layer, be[i], 0, 0)),
                  pl.BlockSpec((1, 1, dff2), lambda i, be, nu: (be[i], 0, 0)),
                  pl.BlockSpec((None, 1, dff, d), lambda i, be, nu: (layer, be[i], 0, 0)),
                  pl.BlockSpec((1, 1, d), lambda i, be, nu: (be[i], 0, 0))],
        out_specs=tile_spec,
        scratch_shapes=[pltpu.VMEM((d, dff2), BF16), pltpu.VMEM((dff, d), BF16)],
    )
    return pl.pallas_call(
        _expert_kernel,
        grid_spec=grid_spec,
        out_shape=jax.ShapeDtypeStruct(hs.shape, F32),
        compiler_params=_cp(("arbitrary",)),
    )(blk_e, n_used, hs, w1_all, b1.reshape(ne, 1, dff2), w2_all, b2.reshape(ne, 1, d))


def _combine_kernel(pos_ref, pos_next_ref, x_ref, gt_ref, g2_ref, out_ref, o_ref, y_sc, sem):
    t = pl.program_id(0)
    n_tok = x_ref.shape[0]
    slot = t % 2

    def gather(p_ref, sl):
        def issue(r, carry):
            for kk in range(TOP_K):
                pltpu.make_async_copy(_token_rows(out_ref, p_ref[r * TOP_K + kk]),
                                      _token_rows(y_sc.at[sl * TOP_K + kk], r), sem.at[sl]).start()
            return carry

        lax.fori_loop(0, n_tok, issue, 0)

    @pl.when(t == 0)
    def _():
        gather(pos_ref, 0)

    @pl.when(t + 1 < pl.num_programs(0))
    def _():
        gather(pos_next_ref, 1 - slot)

    def drain(r, carry):
        for kk in range(TOP_K):
            pltpu.make_async_copy(_token_rows(out_ref, 0), _token_rows(y_sc.at[slot * TOP_K + kk], 0), sem.at[slot]).wait()
        return carry

    lax.fori_loop(0, n_tok, drain, 0)
    gt = gt_ref[...]
    for j in range(TOK_ROWS):
        sl = slice(j * LANES, (j + 1) * LANES)
        acc = y_sc[slot * TOP_K, pl.ds(j, n_tok, stride=TOK_ROWS), :] * gt[:, 0:1]
        for kk in range(1, TOP_K):
            acc = acc + y_sc[slot * TOP_K + kk, pl.ds(j, n_tok, stride=TOK_ROWS), :] * gt[:, kk:kk + 1]
        o_ref[:, sl] = x_ref[:, sl] + g2_ref[0, :, sl] * acc


def _combine(x, out_tiles, pos, gates, mod_l, tpb, n_batch):
    n, d = x.shape
    tm = ROW_TILE
    n_lat_tiles = tpb * n_batch

    def mod_row(t):
        return jnp.where(t < n_lat_tiles, t // tpb, n_batch)

    n_tiles = n // tm
    return pl.pallas_call(
        _combine_kernel,
        grid=(n_tiles,),
        in_specs=[pl.BlockSpec((tm * TOP_K,), lambda t: (t,), memory_space=pltpu.SMEM),
                  pl.BlockSpec((tm * TOP_K,), lambda t: (jnp.minimum(t + 1, n_tiles - 1),), memory_space=pltpu.SMEM),
                  pl.BlockSpec((tm, d), lambda t: (t, 0)), pl.BlockSpec((tm, LANES), lambda t: (t, 0)),
                  pl.BlockSpec((1, 1, d), lambda t: (mod_row(t), 0, 5)), pl.BlockSpec(memory_space=pl.ANY)],
        out_specs=pl.BlockSpec((tm, d), lambda t: (t, 0)),
        out_shape=jax.ShapeDtypeStruct((n, d), F32),
        scratch_shapes=[pltpu.VMEM((2 * TOP_K, tm * TOK_ROWS, LANES), F32), pltpu.SemaphoreType.DMA((2,))],
        compiler_params=_cp(("arbitrary",)),
    )(pos, pos, x, gates, mod_l, out_tiles)


def _moe(m_tiles, top_e, gates, x, mod_l, w1_all, b1, w2_all, b2, layer, tpb, n_batch):
    n, d = x.shape
    nk = n * TOP_K
    bm = MOE_BM
    n_blk = (nk + N_EXPERTS * (bm - 1)) // bm + 1
    cap = n_blk * bm
    flat_e = top_e[:, :TOP_K].reshape(nk)
    onehot = (flat_e[:, None] == jnp.arange(N_EXPERTS, dtype=jnp.int32)[None, :]).astype(jnp.int32)
    csum = jnp.cumsum(onehot, axis=0)
    rank = jnp.sum(jnp.where(onehot > 0, csum, 0), axis=1) - 1
    counts = csum[-1]
    padded = ((counts + bm - 1) // bm) * bm
    pend = jnp.cumsum(padded)
    pstart = pend - padded
    pos = (pstart[flat_e] + rank).astype(jnp.int32)
    n_used = (pend[-1] // bm).astype(jnp.int32)
    blk = jnp.minimum(jnp.arange(n_blk, dtype=jnp.int32), n_used - 1)
    blk_e = jnp.sum((pend[None, :] <= (blk * bm)[:, None]).astype(jnp.int32), axis=1)
    blk_e = jnp.clip(blk_e, 0, N_EXPERTS - 1).astype(jnp.int32)
    hs = _dispatch(m_tiles, pos, (pstart + counts).astype(jnp.int32), (padded - counts).astype(jnp.int32), cap)
    out = _experts(hs, blk_e, n_used.reshape(1), w1_all, b1, w2_all, b2, layer)
    return _combine(x, out, pos, gates, mod_l, tpb, n_batch)


def _rope_tables(seq_len, d_rot, reps, n_extra):
    t = jnp.arange(seq_len)
    rows = (t // GRID_W).astype(F32)
    cols = (t % GRID_W).astype(F32)
    n_freq = d_rot // 4
    inv = ROPE_THETA ** (-jnp.arange(n_freq, dtype=F32) / n_freq)
    ang = jnp.concatenate([rows[:, None] * inv, cols[:, None] * inv], axis=-1)
    cos = jnp.repeat(jnp.cos(ang), 2, axis=-1)
    sin = jnp.repeat(jnp.sin(ang), 2, axis=-1) * jnp.tile(jnp.asarray([-1.0, 1.0], F32), d_rot // 2)
    cos = jnp.concatenate([jnp.tile(cos, (1, reps)), jnp.ones((n_extra, d_rot * reps), F32)], axis=0)
    sin = jnp.concatenate([jnp.tile(sin, (1, reps)), jnp.zeros((n_extra, d_rot * reps), F32)], axis=0)
    return cos, sin


def _retention_tables(decay_logit):
    log_g = jax.nn.log_sigmoid(decay_logit.astype(F32))
    c = RET_CHUNK
    pos = jnp.arange(c, dtype=F32)
    diff = pos[:, None] - pos[None, :]
    lf = log_g[0][:, None, None]
    lb = log_g[1][:, None, None]
    dm_f = jnp.where(diff >= 0, jnp.exp(lf * jnp.where(diff >= 0, diff, 0.0)), 0.0)
    dm_b = jnp.where(diff < 0, jnp.exp(lb * jnp.where(diff < 0, -diff, 0.0)), 0.0)
    qd_f = jnp.exp(log_g[0][:, None] * (pos + 1.0))
    qd_b = jnp.exp(log_g[1][:, None] * (c - pos))
    kd_f = jnp.exp(log_g[0][:, None] * (c - 1.0 - pos))
    kd_b = jnp.exp(log_g[1][:, None] * pos)
    bc = lambda a: jnp.broadcast_to(a[..., None], a.shape + (LANES,))
    dmask = jnp.stack([dm_f, dm_b])
    qdec = jnp.stack([bc(qd_f), bc(qd_b)])
    kdec = jnp.stack([bc(kd_f), bc(kd_b)])
    cdec = jnp.exp(log_g * c).reshape(-1)
    return dmask, qdec, kdec, cdec


def _heads_major(t, n_batch, length, n_heads):
    return t.reshape(n_batch, length, n_heads, -1).transpose(0, 2, 1, 3)


def _heads_t(t, n_batch, length, n_heads):
    return t.reshape(n_batch, length, n_heads, -1).transpose(0, 2, 3, 1)


def kernel(x, c, ctx, c_ctx, norm1_g, norm2_g, w_mod, b_mod, w_in_even, w_out_even, a_q_norm, a_k_norm, b_q_norm, b_k_norm, b_rpb, w_in_odd, w_out_odd, ret_decay, ret_gn, mla_cq_norm, mla_ckv_norm, w_uq, w_ukv, mla_q_norm, mla_k_norm, w_router, b_router, w_exp1, b_exp1, w_exp2, b_exp2):
    bsz, s, d = x.shape
    cl = ctx.shape[1]
    depth = w_mod.shape[0]
    tm = ROW_TILE
    assert s % tm == 0 and (bsz * cl) == tm and s % NA_QB == 0 and bsz + 1 <= 8 and d == TOK_ROWS * LANES
    tpb = s // tm
    nl = bsz * s
    n_all = nl + bsz * cl

    c_rows = jnp.zeros((8, d), F32).at[:bsz].set(c).at[bsz].set(c_ctx)
    mod = _mod_vectors(c_rows, w_mod, b_mod)

    cos_a, sin_a = _rope_tables(s, HEAD_DIM, LANES // HEAD_DIM, tm)
    cos_c, sin_c = _rope_tables(s, RET_DK, 1, tm)
    cos_d, sin_d = _rope_tables(s, MLA_ROPE, LANES // MLA_ROPE, tm)
    na_bases, na_var, na_row_sel, na_col_in, na_dc_onehot = _na_tables(s)

    x_all = jnp.concatenate([x.reshape(nl, d), ctx.reshape(bsz * cl, d)], axis=0)
    lat = lambda t: t[:nl]
    cx = lambda t: t[nl:]
    att_scale = HEAD_DIM ** -0.5 * LOG2E
    grp = GQA_HEADS // GQA_KV_HEADS
    bkv = bsz * GQA_KV_HEADS

    for l in range(depth):
        need_ctx = l < depth - 1
        i = l // 2
        mod_l = mod[l].reshape(8, 1, 6 * d)
        if l % 2 == 0:
            wq, wk, wv, wn = GQA_HEADS * HEAD_DIM, GQA_KV_HEADS * HEAD_DIM, GQA_KV_HEADS * HEAD_DIM, NA_HEADS * HEAD_DIM
            starts = np.cumsum([0, wq, wk, wv, wn, wn])
            segs = [(int(starts[0]), wq, "norm_rope", 1.0), (int(starts[1]), wk, "norm_rope", 1.0),
                    (int(starts[2]), wv, "plain", 1.0), (int(starts[3]), wn, "norm", 1.0),
                    (int(starts[4]), wn, "norm", 1.0), (int(starts[5]), wn, "plain", 1.0)]
            gain = jnp.concatenate([jnp.tile(a_q_norm[i], GQA_HEADS) * att_scale, jnp.tile(a_k_norm[i], GQA_KV_HEADS),
                                    jnp.ones((wv,), F32), jnp.tile(b_q_norm[i], NA_HEADS) * att_scale,
                                    jnp.tile(b_k_norm[i], NA_HEADS), jnp.ones((wn,), F32)]).reshape(1, -1)
            qa, ka, va, qb, kb, vb = _proj(x_all, mod_l, norm1_g[l], w_in_even[i].astype(BF16), gain, cos_a, sin_a,
                                           segs, [BF16] * 6, tpb, bsz)
            k_lat = _heads_major(lat(ka), bsz, s, GQA_KV_HEADS)
            k_cx = _heads_major(cx(ka), bsz, cl, GQA_KV_HEADS)
            v_lat = _heads_major(lat(va), bsz, s, GQA_KV_HEADS)
            v_cx = _heads_major(cx(va), bsz, cl, GQA_KV_HEADS)
            k_all = jnp.concatenate([k_lat, k_cx], axis=2).reshape(bkv, s + cl, HEAD_DIM)
            v_all = jnp.concatenate([v_lat, v_cx], axis=2).reshape(bkv, s + cl, HEAD_DIM)
            qt = _heads_t(lat(qa), bsz, s, GQA_HEADS).reshape(bkv, grp, HEAD_DIM, s)
            oa_t = _flash(qt, k_all, v_all, Q_SUB)
            oa = oa_t.reshape(bsz, GQA_HEADS, HEAD_DIM, s).transpose(0, 3, 1, 2).reshape(nl, wq)
            bias = _na_bias(b_rpb[i], na_row_sel, na_col_in, na_dc_onehot)
            bh = bsz * NA_HEADS
            qn = _heads_major(lat(qb), bsz, s, NA_HEADS).reshape(bh, s, HEAD_DIM)
            kn = _heads_major(lat(kb), bsz, s, NA_HEADS).reshape(bh, s, HEAD_DIM)
            vn = _heads_major(lat(vb), bsz, s, NA_HEADS).reshape(bh, s, HEAD_DIM)
            kn_c = _heads_major(cx(kb), bsz, cl, NA_HEADS).reshape(bh, cl, HEAD_DIM)
            vn_c = _heads_major(cx(vb), bsz, cl, NA_HEADS).reshape(bh, cl, HEAD_DIM)
            ob = _na_attention(qn, kn, vn, kn_c, vn_c, bias, jnp.asarray(na_bases), jnp.asarray(na_var), NA_HEADS)
            ob = ob.reshape(bsz, NA_HEADS, s, HEAD_DIM).transpose(0, 2, 1, 3).reshape(nl, wn)
            if need_ctx:
                qt_c = _heads_t(cx(qa), bsz, cl, GQA_HEADS).reshape(bkv, grp, HEAD_DIM, cl)
                oa_c = _flash(qt_c, k_cx.reshape(bkv, cl, HEAD_DIM), v_cx.reshape(bkv, cl, HEAD_DIM), Q_SUB)
                oa_c = oa_c.reshape(bsz, GQA_HEADS, HEAD_DIM, cl).transpose(0, 3, 1, 2).reshape(bsz * cl, wq)
                qnt_c = _heads_t(cx(qb), bsz, cl, NA_HEADS).reshape(bh, 1, HEAD_DIM, cl)
                ob_c = _flash(qnt_c, kn_c, vn_c, Q_SUB)
                ob_c = ob_c.reshape(bsz, NA_HEADS, HEAD_DIM, cl).transpose(0, 3, 1, 2).reshape(bsz * cl, wn)
                a1 = jnp.concatenate([oa, oa_c], axis=0)
                a2 = jnp.concatenate([ob, ob_c], axis=0)
            else:
                a1, a2 = oa, ob
            w_out = w_out_even[i].astype(BF16)
        else:
            rw = RET_HEADS * RET_DK
            kr_cols = w_in_odd[i][:, 4 * rw + MLA_Q_LORA + MLA_KV_LORA:]
            w_ext = jnp.concatenate([w_in_odd[i]] + [kr_cols] * (LANES * 2 // MLA_ROPE - 1), axis=1).astype(BF16)
            mla_w = MLA_Q_LORA + MLA_KV_LORA + 2 * LANES
            segs = [(0, rw, "rope", RET_DK ** -0.5), (rw, rw, "rope", 1.0), (2 * rw, rw, "plain", 1.0),
                    (3 * rw, rw, "plain", 1.0), (4 * rw, mla_w, "plain", 1.0)]
            gain = jnp.ones((1, w_ext.shape[1]), F32)
            rq, rk, rv, rg, mla_in = _proj(x_all, mod_l, norm1_g[l], w_ext, gain, cos_c, sin_c, segs,
                                           [BF16, BF16, BF16, F32, F32], tpb, bsz)
            dmask, qdec, kdec, cdec = _retention_tables(ret_decay[i])
            yf, yb = _retention(rq, rk, rv, dmask, qdec, kdec, cdec, bsz, s, cl)
            a1 = _ret_finish(yf, yb, rg, ret_gn[i].reshape(1, rw))
            dqk = MLA_NOPE + MLA_ROPE
            perm_q = np.concatenate([np.arange(h * dqk, h * dqk + MLA_NOPE) for h in range(MLA_HEADS)]
                                    + [np.arange(h * dqk + MLA_NOPE, (h + 1) * dqk) for h in range(MLA_HEADS)])
            dkv = MLA_NOPE + MLA_V
            perm_kv = np.concatenate([np.arange(h * dkv, h * dkv + MLA_NOPE) for h in range(MLA_HEADS)]
                                     + [np.arange(h * dkv + MLA_NOPE, (h + 1) * dkv) for h in range(MLA_HEADS)])
            mla_scale = dqk ** -0.5 * LOG2E
            gq = jnp.concatenate([jnp.tile(mla_q_norm[i][:MLA_NOPE], MLA_HEADS),
                                  jnp.tile(mla_q_norm[i][MLA_NOPE:], MLA_HEADS)]).reshape(1, -1) * mla_scale
            gkn = mla_k_norm[i][:MLA_NOPE].reshape(1, -1)
            gkr = jnp.tile(mla_k_norm[i][MLA_NOPE:], LANES // MLA_ROPE).reshape(1, -1)
            q_m, k_m, v_m = _mla_proj(mla_in, w_uq[i][:, perm_q].astype(BF16), w_ukv[i][:, perm_kv].astype(BF16),
                                      mla_cq_norm[i].reshape(1, -1), mla_ckv_norm[i].reshape(1, -1), gq, gkn, gkr,
                                      cos_d, sin_d, tpb, bsz)
            nw = MLA_HEADS * MLA_NOPE

            def qk_heads(t, length, transposed):
                nope = t[:, :nw].reshape(bsz, length, MLA_HEADS, MLA_NOPE)
                rope = t[:, nw:].reshape(bsz, length, MLA_HEADS, MLA_ROPE)
                full = jnp.concatenate([nope, rope], axis=-1)
                return full.transpose(0, 2, 3, 1) if transposed else full.transpose(0, 2, 1, 3)

            bhm = bsz * MLA_HEADS
            k_lat = qk_heads(lat(k_m), s, False)
            k_cx = qk_heads(cx(k_m), cl, False)
            v_lat = _heads_major(lat(v_m), bsz, s, MLA_HEADS)
            v_cx = _heads_major(cx(v_m), bsz, cl, MLA_HEADS)
            k_all = jnp.concatenate([k_lat, k_cx], axis=2).reshape(bhm, s + cl, dqk)
            v_all = jnp.concatenate([v_lat, v_cx], axis=2).reshape(bhm, s + cl, MLA_V)
            qt = qk_heads(lat(q_m), s, True).reshape(bhm, 1, dqk, s)
            om = _flash(qt, k_all, v_all, 4 * Q_SUB)
            om = om.reshape(bsz, MLA_HEADS, MLA_V, s).transpose(0, 3, 1, 2).reshape(nl, MLA_HEADS * MLA_V)
            if need_ctx:
                qt_c = qk_heads(cx(q_m), cl, True).reshape(bhm, 1, dqk, cl)
                om_c = _flash(qt_c, k_cx.reshape(bhm, cl, dqk), v_cx.reshape(bhm, cl, MLA_V), Q_SUB)
                om_c = om_c.reshape(bsz, MLA_HEADS, MLA_V, cl).transpose(0, 3, 1, 2).reshape(bsz * cl, MLA_HEADS * MLA_V)
                a2 = jnp.concatenate([om, om_c], axis=0)
            else:
                a1 = a1[:nl]
                a2 = om
            w_out = w_out_odd[i].astype(BF16)

        n_rows = n_all if need_ctx else nl
        w_r = jnp.zeros((d, LANES), F32).at[:, :N_EXPERTS].set(w_router[l])
        b_r = jnp.full((1, LANES), NEG, F32).at[0, :N_EXPERTS].set(b_router[l])
        x_new, m, top_e, gates = _out_proj(a1, a2, x_all, w_out, mod_l, norm2_g[l], w_r, b_r, n_rows, tpb, bsz)
        x_all = _moe(m, top_e, gates, x_new, mod_l, w_exp1, b_exp1[l], w_exp2, b_exp2[l], l, tpb, bsz)
    return x_all[:nl].reshape(bsz, s, d)
```

```python
import functools
import math

import numpy as np
import jax
import jax.numpy as jnp
from jax import lax
from jax.experimental import pallas as pl
from jax.experimental.pallas import tpu as pltpu

F32 = jnp.float32
BF16 = jnp.bfloat16

GRID_W = 64
HEAD_DIM = 64
GQA_HEADS = 8
GQA_KV_HEADS = 2
NA_HEADS = 8
NA_ROWS = 8
NA_COLS = 16
RET_HEADS = 4
RET_DK = 128
MLA_HEADS = 4
MLA_Q_LORA = 256
MLA_KV_LORA = 128
MLA_NOPE = 128
MLA_ROPE = 64
MLA_V = 128
N_EXPERTS = 32
TOP_K = 4
SWIGLU_LIMIT = 7.0
SWIGLU_ALPHA = 1.702
ROPE_THETA = 10000.0
EPS = 1e-6
GN_EPS = 1e-5
LOG2E = math.log2(math.e)
NEG = -1e30

LANES = 128
ROW_TILE = 512
Q_SUB = 256
ONES_ROWS = 16
NA_QB = 256
NA_WIN_ROWS = 12
RET_CHUNK = 256
MOE_BM = 512
TOK_ROWS = 8
VMEM_LIMIT = 56 * 1024 * 1024


def _cp(sem):
    return pltpu.CompilerParams(dimension_semantics=sem, vmem_limit_bytes=VMEM_LIMIT)


def _mod_kernel(c_ref, w_ref, b_ref, o_ref):
    c = c_ref[...]
    s = c * jax.nn.sigmoid(c)
    o_ref[0] = jnp.dot(s, w_ref[0], precision=lax.Precision.HIGHEST, preferred_element_type=F32) + b_ref[0]


def _mod_vectors(c_rows, w_mod, b_mod):
    depth, d, d6 = w_mod.shape
    tn = 1536
    return pl.pallas_call(
        _mod_kernel,
        grid=(depth, d6 // tn),
        in_specs=[pl.BlockSpec((8, d), lambda l, j: (0, 0)),
                  pl.BlockSpec((1, d, tn), lambda l, j: (l, 0, j)),
                  pl.BlockSpec((1, 1, tn), lambda l, j: (l, 0, j))],
        out_specs=pl.BlockSpec((1, 8, tn), lambda l, j: (l, 0, j)),
        out_shape=jax.ShapeDtypeStruct((depth, 8, d6), F32),
        compiler_params=_cp(("parallel", "parallel")),
    )(c_rows, w_mod, b_mod.reshape(depth, 1, d6))


def _modulated_norm(x, g, sc, sh):
    ms = jnp.mean(x * x, axis=-1, keepdims=True)
    return x * lax.rsqrt(ms + EPS) * g * (1.0 + sc) + sh


def _pair_rope(y, cos, sin_signed):
    lane = lax.broadcasted_iota(jnp.int32, y.shape, 1)
    partner = jnp.where((lane & 1) == 0, pltpu.roll(y, LANES - 1, 1), pltpu.roll(y, 1, 1))
    return y * cos + partner * sin_signed


def _store_token_tiles(ref, val):
    n = val.shape[0]
    for j in range(TOK_ROWS):
        ref[pl.ds(j, n, stride=TOK_ROWS), :] = val[:, j * LANES:(j + 1) * LANES]


def _load_token_tiles(ref, n):
    return jnp.concatenate([ref[pl.ds(j, n, stride=TOK_ROWS), :] for j in range(TOK_ROWS)], axis=-1)


def _split_dot(a_f32, w_bf16):
    hi = a_f32.astype(BF16)
    lo = (a_f32 - hi.astype(F32)).astype(BF16)
    return (jnp.dot(hi, w_bf16, preferred_element_type=F32) + jnp.dot(lo, w_bf16, preferred_element_type=F32))


def _proj_kernel(x_ref, sh_ref, sc_ref, g_ref, w_ref, gain_ref, bd_ref, cos_ref, sin_ref, *out_refs, segs):
    a = _modulated_norm(x_ref[...], g_ref[...], sc_ref[0], sh_ref[0]).astype(BF16)
    cos = cos_ref[...]
    sin = sin_ref[...]
    for (start, width, mode, scale), o_ref in zip(segs, out_refs):
        y_seg = jnp.dot(a, w_ref[:, start:start + width], preferred_element_type=F32)
        if mode == "plain":
            o_ref[...] = y_seg.astype(o_ref.dtype)
            continue
        for j in range(width // LANES):
            y = y_seg[:, j * LANES:(j + 1) * LANES]
            if "norm" in mode:
                ms = _split_dot(y * y, bd_ref[...])
                y = y * lax.rsqrt(ms + EPS) * gain_ref[:, start + j * LANES:start + (j + 1) * LANES]
            if scale != 1.0:
                y = y * scale
            if "rope" in mode:
                y = _pair_rope(y, cos, sin)
            o_ref[:, j * LANES:(j + 1) * LANES] = y.astype(o_ref.dtype)


def _proj(x_all, mod_l, norm_g, w, gain, cos_t, sin_t, segs, out_dtypes, n_lat_tiles_per_batch, n_batch):
    n, d = x_all.shape
    tm = ROW_TILE
    n_tiles = n // tm
    wtot = w.shape[1]
    tpb = n_lat_tiles_per_batch
    n_lat_tiles = tpb * n_batch

    def mod_row(t):
        return jnp.where(t < n_lat_tiles, t // tpb, n_batch)

    def rope_row(t):
        return jnp.where(t < n_lat_tiles, t % tpb, tpb)

    bd = np.kron(np.eye(2, dtype=np.float32), np.full((HEAD_DIM, HEAD_DIM), 1.0 / HEAD_DIM, np.float32))
    in_specs = [
        pl.BlockSpec((tm, d), lambda t: (t, 0)),
        pl.BlockSpec((1, 1, d), lambda t: (mod_row(t), 0, 0)),
        pl.BlockSpec((1, 1, d), lambda t: (mod_row(t), 0, 1)),
        pl.BlockSpec((1, d), lambda t: (0, 0)),
        pl.BlockSpec((d, wtot), lambda t: (0, 0)),
        pl.BlockSpec((1, wtot), lambda t: (0, 0)),
        pl.BlockSpec((LANES, LANES), lambda t: (0, 0)),
        pl.BlockSpec((tm, LANES), lambda t: (rope_row(t), 0)),
        pl.BlockSpec((tm, LANES), lambda t: (rope_row(t), 0)),
    ]
    out_specs = [pl.BlockSpec((tm, s[1]), lambda t: (t, 0)) for s in segs]
    out_shape = [jax.ShapeDtypeStruct((n, s[1]), dt) for s, dt in zip(segs, out_dtypes)]
    return pl.pallas_call(
        functools.partial(_proj_kernel, segs=tuple(segs)),
        grid=(n_tiles,),
        in_specs=in_specs,
        out_specs=out_specs,
        out_shape=out_shape,
        compiler_params=_cp(("parallel",)),
    )(x_all, mod_l, mod_l, norm_g.reshape(1, d), w, gain, jnp.asarray(bd, BF16), cos_t, sin_t)


def _mla_proj_kernel(x_ref, wuq_ref, wukv_ref, gcq_ref, gckv_ref, gq_ref, gkn_ref, gkr_ref, cos_ref, sin_ref,
                     q_ref, k_ref, v_ref):
    x = x_ref[...]
    cq = x[:, :MLA_Q_LORA]
    ckv = x[:, MLA_Q_LORA:MLA_Q_LORA + MLA_KV_LORA]
    kr = x[:, MLA_Q_LORA + MLA_KV_LORA:MLA_Q_LORA + MLA_KV_LORA + LANES]
    cos = cos_ref[...]
    sin = sin_ref[...]
    cqn = cq * lax.rsqrt(jnp.mean(cq * cq, axis=-1, keepdims=True) + EPS) * gcq_ref[...]
    ckvn = ckv * lax.rsqrt(jnp.mean(ckv * ckv, axis=-1, keepdims=True) + EPS) * gckv_ref[...]
    q = jnp.dot(cqn.astype(BF16), wuq_ref[...], preferred_element_type=F32)
    kv = jnp.dot(ckvn.astype(BF16), wukv_ref[...], preferred_element_type=F32)
    nh = MLA_HEADS
    d_qk = float(MLA_NOPE + MLA_ROPE)
    lane = lax.broadcasted_iota(jnp.int32, (1, LANES), 1)
    low = lane < MLA_ROPE

    def half_sums(slab):
        sq = slab * slab
        a = jnp.sum(jnp.where(low, sq, 0.0), axis=-1, keepdims=True)
        return a, jnp.sum(sq, axis=-1, keepdims=True) - a

    rope_w = nh * MLA_NOPE
    q_rope_ss = []
    for r in range(nh // 2):
        q_rope_ss.extend(half_sums(q[:, rope_w + r * LANES:rope_w + (r + 1) * LANES]))
    kr_ss, _ = half_sums(kr)
    rs_q, rs_k = [], []
    for h in range(nh):
        qn = q[:, h * MLA_NOPE:(h + 1) * MLA_NOPE]
        kn = kv[:, h * MLA_NOPE:(h + 1) * MLA_NOPE]
        rs_q.append(lax.rsqrt((jnp.sum(qn * qn, axis=-1, keepdims=True) + q_rope_ss[h]) / d_qk + EPS))
        rs_k.append(lax.rsqrt((jnp.sum(kn * kn, axis=-1, keepdims=True) + kr_ss) / d_qk + EPS))
        q_ref[:, h * MLA_NOPE:(h + 1) * MLA_NOPE] = (qn * rs_q[h] * gq_ref[:, h * MLA_NOPE:(h + 1) * MLA_NOPE]).astype(q_ref.dtype)
        k_ref[:, h * MLA_NOPE:(h + 1) * MLA_NOPE] = (kn * rs_k[h] * gkn_ref[...]).astype(k_ref.dtype)
    kr_rot = _pair_rope(kr * gkr_ref[...], cos, sin)
    for r in range(nh // 2):
        sl = slice(rope_w + r * LANES, rope_w + (r + 1) * LANES)
        yq = q[:, sl] * jnp.where(low, rs_q[2 * r], rs_q[2 * r + 1]) * gq_ref[:, sl]
        q_ref[:, sl] = _pair_rope(yq, cos, sin).astype(q_ref.dtype)
        k_ref[:, sl] = (kr_rot * jnp.where(low, rs_k[2 * r], rs_k[2 * r + 1])).astype(k_ref.dtype)
    v_ref[...] = kv[:, nh * MLA_NOPE:].astype(v_ref.dtype)


def _mla_proj(mla_in, wuq, wukv, gcq, gckv, gq, gkn, gkr, cos_t, sin_t, tpb, n_batch):
    n, win = mla_in.shape
    tm = ROW_TILE
    n_lat_tiles = tpb * n_batch

    def rope_row(t):
        return jnp.where(t < n_lat_tiles, t % tpb, tpb)

    qk_w = MLA_HEADS * (MLA_NOPE + MLA_ROPE)
    v_w = MLA_HEADS * MLA_V
    full = lambda a: pl.BlockSpec(a.shape, lambda t: (0,) * a.ndim)
    return pl.pallas_call(
        _mla_proj_kernel,
        grid=(n // tm,),
        in_specs=[pl.BlockSpec((tm, win), lambda t: (t, 0)), full(wuq), full(wukv), full(gcq), full(gckv),
                  full(gq), full(gkn), full(gkr),
                  pl.BlockSpec((tm, LANES), lambda t: (rope_row(t), 0)),
                  pl.BlockSpec((tm, LANES), lambda t: (rope_row(t), 0))],
        out_specs=[pl.BlockSpec((tm, qk_w), lambda t: (t, 0)), pl.BlockSpec((tm, qk_w), lambda t: (t, 0)),
                   pl.BlockSpec((tm, v_w), lambda t: (t, 0))],
        out_shape=[jax.ShapeDtypeStruct((n, qk_w), BF16), jax.ShapeDtypeStruct((n, qk_w), BF16),
                   jax.ShapeDtypeStruct((n, v_w), BF16)],
        compiler_params=_cp(("parallel",)),
    )(mla_in, wuq, wukv, gcq, gckv, gq, gkn, gkr, cos_t, sin_t)


def _flash_kernel(qt_ref, k_ref, vt_ref, ot_ref, q_sc, s_0, s_1, s_2, mx_0, mx_1, mx_2, m_sc, acc_sc, *,
                  g, n_chunks, dv):
    bufs = ((s_0, mx_0), (s_1, mx_1), (s_2, mx_2))
    tq = qt_ref.shape[3]
    for gi in range(g):
        q_sc[:, gi * tq:(gi + 1) * tq] = qt_ref[0, gi]
    m_sc[...] = jnp.full(m_sc.shape, NEG, F32)
    acc_sc[...] = jnp.zeros(acc_sc.shape, F32)

    def scores(ci, s_ref, mx_ref):
        s = jnp.dot(k_ref[0, ci], q_sc[...], preferred_element_type=F32)
        s_ref[...] = s
        mx_ref[...] = jnp.max(s, axis=0, keepdims=True)

    def accumulate(ci, s_ref, mx_ref):
        m_old = m_sc[...]
        m_new = jnp.maximum(m_old, mx_ref[...])
        alpha = jnp.exp2(m_old - m_new)
        p = jnp.exp2(s_ref[...] - m_new).astype(BF16)
        acc_sc[...] = alpha * acc_sc[...] + jnp.dot(vt_ref[0, ci], p, preferred_element_type=F32)
        m_sc[...] = m_new

    def fused(ci_n, s_n, mx_n, ci_c, s_c, mx_c):
        m_old = m_sc[...]
        m_new = jnp.maximum(m_old, mx_c[...])
        alpha = jnp.exp2(m_old - m_new)
        pv = None
        mx = None
        for k0 in range(0, s_c.shape[0], 256):
            s = jnp.dot(k_ref[0, ci_n, k0:k0 + 256, :], q_sc[...], preferred_element_type=F32)
            s_n[k0:k0 + 256, :] = s
            mxj = jnp.max(s, axis=0, keepdims=True)
            mx = mxj if mx is None else jnp.maximum(mx, mxj)
            p = jnp.exp2(s_c[k0:k0 + 256, :] - m_new).astype(BF16)
            part = jnp.dot(vt_ref[0, ci_c, :, k0:k0 + 256], p, preferred_element_type=F32)
            pv = part if pv is None else pv + part
        mx_n[...] = mx
        acc_sc[...] = alpha * acc_sc[...] + pv
        m_sc[...] = m_new

    scores(0, *bufs[0])
    if n_chunks == 1:
        accumulate(0, *bufs[0])
    else:
        scores(1, *bufs[1])
        n_fused = n_chunks - 2

        def triple(j, carry):
            c = 3 * j
            for r in range(3):
                fused(c + r + 2, *bufs[(r + 2) % 3], c + r, *bufs[r])
            return carry

        lax.fori_loop(0, n_fused // 3, triple, 0)
        for c in range(3 * (n_fused // 3), n_fused):
            fused(c + 2, *bufs[(c + 2) % 3], c, *bufs[c % 3])
        accumulate(n_chunks - 2, *bufs[(n_chunks - 2) % 3])
        accumulate(n_chunks - 1, *bufs[(n_chunks - 1) % 3])
    acc = acc_sc[...]
    o = acc[:dv] / acc[dv:dv + 1]
    for gi in range(g):
        ot_ref[0, gi] = o[:, gi * tq:(gi + 1) * tq].astype(ot_ref.dtype)


def _key_chunk(lk):
    for tk in (1280, 1024, 768, 512, 256):
        if lk % tk == 0:
            return tk
    raise ValueError(f"key length {lk} must be a multiple of 256")


def _flash(qt, k, v, tq_blk):
    bk, g, dq, lq = qt.shape
    lk, dv = v.shape[1], v.shape[2]
    tk = _key_chunk(lk)
    nch = lk // tk
    kc = k.reshape(bk, nch, tk, dq)
    dve = dv + ONES_ROWS
    vt = jnp.concatenate([v, jnp.ones((bk, lk, ONES_ROWS), v.dtype)], axis=-1)
    vt = vt.reshape(bk, nch, tk, dve).transpose(0, 1, 3, 2)
    tq_blk = min(tq_blk, lq)
    assert lq % tq_blk == 0 and tq_blk % LANES == 0
    w = g * tq_blk
    return pl.pallas_call(
        functools.partial(_flash_kernel, g=g, n_chunks=nch, dv=dv),
        grid=(bk, lq // tq_blk),
        in_specs=[pl.BlockSpec((1, g, dq, tq_blk), lambda b, i: (b, 0, 0, i)),
                  pl.BlockSpec((1, nch, tk, dq), lambda b, i: (b, 0, 0, 0)),
                  pl.BlockSpec((1, nch, dve, tk), lambda b, i: (b, 0, 0, 0))],
        out_specs=pl.BlockSpec((1, g, dv, tq_blk), lambda b, i: (b, 0, 0, i)),
        out_shape=jax.ShapeDtypeStruct((bk, g, dv, lq), BF16),
        scratch_shapes=[pltpu.VMEM((dq, w), BF16)] + [pltpu.VMEM((tk, w), F32)] * 3 + [pltpu.VMEM((1, w), F32)] * 4
                       + [pltpu.VMEM((dve, w), F32)],
        compiler_params=_cp(("parallel", "parallel")),
    )(qt, kc, vt)


def _na_tables(seq_len):
    rows_n = seq_len // GRID_W
    assert rows_n >= NA_WIN_ROWS and NA_ROWS <= rows_n
    nb = seq_len // NA_QB
    rpq = NA_QB // GRID_W
    band = NA_WIN_ROWS * GRID_W
    variants, var_id, bases = {}, [], []
    for j in range(nb):
        base = int(np.clip(rpq * j - NA_ROWS // 2, 0, rows_n - NA_WIN_ROWS))
        bases.append(base)
        t = np.arange(NA_QB) + j * NA_QB
        r, col = t // GRID_W, t % GRID_W
        r0 = np.clip(r - NA_ROWS // 2, 0, rows_n - NA_ROWS)
        c0 = np.clip(col - NA_COLS // 2, 0, GRID_W - NA_COLS)
        kk = np.arange(band)
        kr = base + kk // GRID_W
        kc = kk % GRID_W
        inside = ((kr[None] >= r0[:, None]) & (kr[None] < r0[:, None] + NA_ROWS)
                  & (kc[None] >= c0[:, None]) & (kc[None] < c0[:, None] + NA_COLS))
        rel = (kr[None] - r[:, None] + NA_ROWS - 1) * (2 * NA_COLS - 1) + (kc[None] - col[:, None] + NA_COLS - 1)
        tab = np.where(inside, rel, -1).astype(np.int32)
        assert (inside.sum(axis=1) == NA_ROWS * NA_COLS).all()
        key = tab.tobytes()
        if key not in variants:
            variants[key] = (len(variants), tab)
        var_id.append(variants[key][0])
    tabs = np.stack([v[1] for v in sorted(variants.values(), key=lambda kv: kv[0])])
    n_dr = 2 * NA_ROWS - 1
    n_dc = 2 * NA_COLS - 1
    col = np.arange(GRID_W)
    c0 = np.clip(col - NA_COLS // 2, 0, GRID_W - NA_COLS)
    col_in = (col[None] >= c0[:, None]) & (col[None] < c0[:, None] + NA_COLS)
    dc = col[None] - col[:, None] + NA_COLS - 1
    t5 = tabs.reshape(len(tabs), rpq, GRID_W, NA_WIN_ROWS, GRID_W)
    row_sel = np.full((len(tabs), rpq, NA_WIN_ROWS), n_dr, np.int32)
    for v in range(len(tabs)):
        for a in range(rpq):
            for i in range(NA_WIN_ROWS):
                blk = t5[v, a, :, i, :]
                if (blk >= 0).any():
                    dr = int(blk[blk >= 0][0]) // n_dc
                    assert (np.where(col_in, dr * n_dc + dc, -1) == blk).all()
                    row_sel[v, a, i] = dr
                else:
                    assert (blk < 0).all()
    dc_onehot = (dc[None] == np.arange(n_dc)[:, None, None]).astype(np.float32)
    return np.asarray(bases, np.int32), np.asarray(var_id, np.int32), row_sel, col_in, dc_onehot


def _na_bias(rpb, row_sel, col_in, dc_onehot):
    h = rpb.shape[0]
    n_dr, n_dc = 2 * NA_ROWS - 1, 2 * NA_COLS - 1
    t = jnp.einsum("hdj,jck->hdck", rpb.astype(F32).reshape(h, n_dr, n_dc) * LOG2E, jnp.asarray(dc_onehot),
                   precision=lax.Precision.HIGHEST)
    t = jnp.where(jnp.asarray(col_in)[None, None], t, NEG)
    t = jnp.concatenate([t, jnp.full((h, 1, GRID_W, GRID_W), NEG, F32)], axis=1)
    nv, rpq, nw = row_sel.shape
    b = t[:, jnp.asarray(row_sel)]
    return b.transpose(1, 0, 2, 4, 3, 5).reshape(nv, h, rpq * GRID_W, nw * GRID_W)


def _na_kernel(base_ref, var_ref, q_ref, k_ref, v_ref, kc_ref, vc_ref, bias_ref, o_ref):
    j = pl.program_id(1)
    band = NA_WIN_ROWS * GRID_W
    start = pl.multiple_of(base_ref[j] * GRID_W, GRID_W)
    q = q_ref[0]
    kw = k_ref[0, pl.ds(start, band), :]
    vw = v_ref[0, pl.ds(start, band), :]
    nt = (((1,), (1,)), ((), ()))
    s_win = lax.dot_general(q, kw, nt, preferred_element_type=F32) + bias_ref[0, 0]
    s_ctx = lax.dot_general(q, kc_ref[0], nt, preferred_element_type=F32)
    m = jnp.maximum(jnp.max(s_win, axis=-1, keepdims=True), jnp.max(s_ctx, axis=-1, keepdims=True))
    p_win = jnp.exp2(s_win - m)
    p_ctx = jnp.exp2(s_ctx - m)
    l = jnp.sum(p_win, axis=-1, keepdims=True) + jnp.sum(p_ctx, axis=-1, keepdims=True)
    o = (jnp.dot(p_win.astype(BF16), vw, preferred_element_type=F32)
         + jnp.dot(p_ctx.astype(BF16), vc_ref[0], preferred_element_type=F32))
    o_ref[0] = (o / l).astype(o_ref.dtype)


def _na_attention(q, k, v, k_ctx, v_ctx, bias, bases, var_id, n_heads):
    bh, s, d = q.shape
    cl = k_ctx.shape[1]
    band = NA_WIN_ROWS * GRID_W
    nb = s // NA_QB
    grid_spec = pltpu.PrefetchScalarGridSpec(
        num_scalar_prefetch=2,
        grid=(bh, nb),
        in_specs=[pl.BlockSpec((1, NA_QB, d), lambda b, j, bs, vr: (b, j, 0)),
                  pl.BlockSpec((1, s, d), lambda b, j, bs, vr: (b, 0, 0)),
                  pl.BlockSpec((1, s, d), lambda b, j, bs, vr: (b, 0, 0)),
                  pl.BlockSpec((1, cl, d), lambda b, j, bs, vr: (b, 0, 0)),
                  pl.BlockSpec((1, cl, d), lambda b, j, bs, vr: (b, 0, 0)),
                  pl.BlockSpec((1, 1, NA_QB, band), lambda b, j, bs, vr: (vr[j], b % n_heads, 0, 0))],
        out_specs=pl.BlockSpec((1, NA_QB, d), lambda b, j, bs, vr: (b, j, 0)),
    )
    return pl.pallas_call(
        _na_kernel,
        grid_spec=grid_spec,
        out_shape=jax.ShapeDtypeStruct((bh, s, d), BF16),
        compiler_params=_cp(("parallel", "arbitrary")),
    )(bases, var_id, q, k, v, k_ctx, v_ctx, bias)


def _ret_kernel(cdec_ref, qf_ref, kf_ref, vf_ref, qb_ref, kb_ref, vb_ref, dmask_ref, qdec_ref, kdec_ref,
                yf_ref, yb_ref, state_sc):
    @pl.when(pl.program_id(1) == 0)
    def _():
        state_sc[...] = jnp.zeros(state_sc.shape, F32)

    nt = (((1,), (1,)), ((), ()))
    tn = (((0,), (0,)), ((), ()))
    dk = RET_DK
    for d, (q_ref, k_ref, v_ref, y_ref) in enumerate(((qf_ref, kf_ref, vf_ref, yf_ref),
                                                      (qb_ref, kb_ref, vb_ref, yb_ref))):
        for h in range(RET_HEADS):
            sl = slice(h * dk, (h + 1) * dk)
            q = q_ref[:, sl]
            k = k_ref[:, sl]
            v = v_ref[:, sl]
            st = state_sc[d, h]
            a = lax.dot_general(q, k, nt, preferred_element_type=F32) * dmask_ref[d, h]
            inner = jnp.dot(a.astype(BF16), v, preferred_element_type=F32)
            cross = jnp.dot(q, st.astype(BF16), preferred_element_type=F32) * qdec_ref[d, h]
            y_ref[:, sl] = inner + cross
            vs = (v.astype(F32) * kdec_ref[d, h]).astype(BF16)
            state_sc[d, h] = st * cdec_ref[d * RET_HEADS + h] + lax.dot_general(k, vs, tn, preferred_element_type=F32)


def _retention(rq, rk, rv, dmask, qdec, kdec, cdec, n_batch, seq_len, ctx_len):
    n, w = rq.shape
    c = RET_CHUNK
    assert ctx_len == c and seq_len % c == 0
    ncl = seq_len // c
    ctx_blk0 = (n_batch * seq_len) // c

    def fwd(b, s, cd):
        return (jnp.where(s == 0, ctx_blk0 + b, b * ncl + s - 1), 0)

    def bwd(b, s, cd):
        return (jnp.where(s == 0, ctx_blk0 + b, b * ncl + ncl - s), 0)

    full = lambda a: pl.BlockSpec(a.shape, lambda b, s, cd: (0,) * a.ndim)
    grid_spec = pltpu.PrefetchScalarGridSpec(
        num_scalar_prefetch=1,
        grid=(n_batch, ncl + 1),
        in_specs=[pl.BlockSpec((c, w), fwd)] * 3 + [pl.BlockSpec((c, w), bwd)] * 3 + [full(dmask), full(qdec), full(kdec)],
        out_specs=[pl.BlockSpec((c, w), fwd), pl.BlockSpec((c, w), bwd)],
        scratch_shapes=[pltpu.VMEM((2, RET_HEADS, RET_DK, RET_DK), F32)],
    )
    return pl.pallas_call(
        _ret_kernel,
        grid_spec=grid_spec,
        out_shape=[jax.ShapeDtypeStruct((n, w), F32)] * 2,
        compiler_params=_cp(("parallel", "arbitrary")),
    )(cdec, rq, rk, rv, rq, rk, rv, dmask, qdec, kdec)


def _ret_finish_kernel(yf_ref, yb_ref, rg_ref, gn_ref, o_ref):
    y = yf_ref[...] + yb_ref[...]
    gate = rg_ref[...]
    gate = gate * jax.nn.sigmoid(gate)
    for h in range(RET_HEADS):
        sl = slice(h * RET_DK, (h + 1) * RET_DK)
        yh = y[:, sl]
        mu = jnp.mean(yh, axis=-1, keepdims=True)
        var = jnp.mean(jnp.square(yh - mu), axis=-1, keepdims=True)
        o_ref[:, sl] = ((yh - mu) * lax.rsqrt(var + GN_EPS) * gn_ref[:, sl] * gate[:, sl]).astype(o_ref.dtype)


def _ret_finish(yf, yb, rg, gn):
    n, w = yf.shape
    tm = ROW_TILE
    spec = pl.BlockSpec((tm, w), lambda t: (t, 0))
    return pl.pallas_call(
        _ret_finish_kernel,
        grid=(n // tm,),
        in_specs=[spec, spec, spec, pl.BlockSpec((1, w), lambda t: (0, 0))],
        out_specs=spec,
        out_shape=jax.ShapeDtypeStruct((n, w), BF16),
        compiler_params=_cp(("parallel",)),
    )(yf, yb, rg, gn)


def _out_kernel(a1_ref, a2_ref, x_ref, w_ref, g1_ref, ng_ref, sh_ref, sc_ref, wr_ref, br_ref,
                xo_ref, m_ref, e_ref, gt_ref):
    half = a1_ref.shape[1]
    o = (jnp.dot(a1_ref[...], w_ref[:half, :], preferred_element_type=F32)
         + jnp.dot(a2_ref[...], w_ref[half:, :], preferred_element_type=F32))
    x = x_ref[...] + g1_ref[0] * o
    xo_ref[...] = x
    m = _modulated_norm(x, ng_ref[...], sc_ref[0], sh_ref[0])
    _store_token_tiles(m_ref, m)
    m_hi = m.astype(BF16)
    m_lo = (m - m_hi.astype(F32)).astype(BF16)
    hi_prod = jnp.dot(m_hi, wr_ref[...], preferred_element_type=F32)
    logits = (hi_prod[:, :LANES] + hi_prod[:, LANES:]
              + jnp.dot(m_lo, wr_ref[:, :LANES], preferred_element_type=F32) + br_ref[...])
    lane = lax.broadcasted_iota(jnp.int32, logits.shape, 1).astype(F32)
    e_out = jnp.zeros(logits.shape, F32)
    g_out = jnp.zeros(logits.shape, F32)
    top0 = None
    denom = None
    for kk in range(TOP_K):
        mx = jnp.max(logits, axis=-1, keepdims=True)
        idx = jnp.min(jnp.where(logits == mx, lane, float(LANES)), axis=-1, keepdims=True)
        if kk == 0:
            top0 = mx
            ex = jnp.ones_like(mx)
            denom = ex
        else:
            ex = jnp.exp(mx - top0)
            denom = denom + ex
        e_out = jnp.where(lane == kk, idx, e_out)
        g_out = jnp.where(lane == kk, ex, g_out)
        logits = jnp.where(lane == idx, NEG * 2.0, logits)
    e_ref[...] = e_out.astype(jnp.int32)
    gt_ref[...] = g_out / denom


def _out_proj(a1, a2, x_all, w_out, mod_l, norm2_g, w_r, b_r, n_rows, tpb, n_batch):
    d = x_all.shape[1]
    half = a1.shape[1]
    tm = ROW_TILE
    n_lat_tiles = tpb * n_batch

    def mod_row(t):
        return jnp.where(t < n_lat_tiles, t // tpb, n_batch)

    row = lambda wd: pl.BlockSpec((tm, wd), lambda t: (t, 0))
    modspec = lambda col: pl.BlockSpec((1, 1, d), lambda t: (mod_row(t), 0, col))
    return pl.pallas_call(
        _out_kernel,
        grid=(n_rows // tm,),
        in_specs=[row(half), row(half), row(d), pl.BlockSpec((2 * half, d), lambda t: (0, 0)),
                  modspec(2), pl.BlockSpec((1, d), lambda t: (0, 0)), modspec(3), modspec(4),
                  pl.BlockSpec((d, 2 * LANES), lambda t: (0, 0)), pl.BlockSpec((1, LANES), lambda t: (0, 0))],
        out_specs=[row(d), pl.BlockSpec((tm * TOK_ROWS, LANES), lambda t: (t, 0)), row(LANES), row(LANES)],
        out_shape=[jax.ShapeDtypeStruct((n_rows, d), F32), jax.ShapeDtypeStruct((n_rows * TOK_ROWS, LANES), F32),
                   jax.ShapeDtypeStruct((n_rows, LANES), jnp.int32), jax.ShapeDtypeStruct((n_rows, LANES), F32)],
        compiler_params=_cp(("parallel",)),
    )(a1, a2, x_all, w_out, mod_l, norm2_g.reshape(1, d), mod_l, mod_l, w_r, b_r)


def _token_rows(ref, idx):
    return ref.at[pl.ds(pl.multiple_of(idx * TOK_ROWS, TOK_ROWS), TOK_ROWS)]


def _dispatch_kernel(pad_start_ref, pad_cnt_ref, pos_ref, m_ref, hs_ref, zero_sc, sem, pad_sem):
    n_tok = m_ref.shape[0] // TOK_ROWS

    @pl.when(pl.program_id(0) == 0)
    def _():
        zero_sc[...] = jnp.zeros(zero_sc.shape, zero_sc.dtype)

        def per_expert(e, carry):
            def fill(r, c):
                pltpu.make_async_copy(zero_sc, _token_rows(hs_ref, pad_start_ref[e] + r), pad_sem).start()
                return c

            lax.fori_loop(0, pad_cnt_ref[e], fill, 0)

            def fill_done(r, c):
                pltpu.make_async_copy(zero_sc, _token_rows(hs_ref, 0), pad_sem).wait()
                return c

            lax.fori_loop(0, pad_cnt_ref[e], fill_done, 0)
            return carry

        lax.fori_loop(0, N_EXPERTS, per_expert, 0)

    def issue(r, carry):
        for kk in range(TOP_K):
            pltpu.make_async_copy(_token_rows(m_ref, r), _token_rows(hs_ref, pos_ref[r * TOP_K + kk]),
                                  sem).start(priority=kk % 2)
        return carry

    lax.fori_loop(0, n_tok, issue, 0)

    def drain(r, carry):
        for kk in range(TOP_K):
            pltpu.make_async_copy(_token_rows(m_ref, 0), _token_rows(hs_ref, 0), sem).wait()
        return carry

    lax.fori_loop(0, n_tok, drain, 0)


def _dispatch(m_tiles, pos, pad_start, pad_cnt, cap):
    n_tok = m_tiles.shape[0] // TOK_ROWS
    tm = ROW_TILE
    grid_spec = pltpu.PrefetchScalarGridSpec(
        num_scalar_prefetch=2,
        grid=(n_tok // tm,),
        in_specs=[pl.BlockSpec((tm * TOP_K,), lambda t, ps, pc: (t,), memory_space=pltpu.SMEM),
                  pl.BlockSpec((tm * TOK_ROWS, LANES), lambda t, ps, pc: (t, 0))],
        out_specs=pl.BlockSpec(memory_space=pl.ANY),
        scratch_shapes=[pltpu.VMEM((TOK_ROWS, LANES), F32), pltpu.SemaphoreType.DMA(()), pltpu.SemaphoreType.DMA(())],
    )
    return pl.pallas_call(
        _dispatch_kernel,
        grid_spec=grid_spec,
        out_shape=jax.ShapeDtypeStruct((cap * TOK_ROWS, LANES), F32),
        compiler_params=pltpu.CompilerParams(dimension_semantics=("arbitrary",), has_side_effects=True,
                                             vmem_limit_bytes=VMEM_LIMIT),
    )(pad_start, pad_cnt, pos, m_tiles)


def _expert_kernel(be_ref, nu_ref, x_ref, w1_ref, b1_ref, w2_ref, b2_ref, o_ref, w1_sc, w2_sc):
    i = pl.program_id(0)

    @pl.when((i == 0) | (be_ref[i] != be_ref[jnp.maximum(i - 1, 0)]))
    def _():
        w1_sc[...] = w1_ref[0].astype(BF16)
        w2_sc[...] = w2_ref[0].astype(BF16)

    @pl.when(i < nu_ref[0])
    def _():
        dff = w2_ref.shape[1]
        x = _load_token_tiles(x_ref, MOE_BM).astype(BF16)
        u = jnp.dot(x, w1_sc[...], preferred_element_type=F32) + b1_ref[0]
        gl = jnp.minimum(u[:, :dff], SWIGLU_LIMIT)
        up = jnp.clip(u[:, dff:], -SWIGLU_LIMIT, SWIGLU_LIMIT)
        act = gl * jax.nn.sigmoid(SWIGLU_ALPHA * gl) * (up + 1.0)
        _store_token_tiles(o_ref, jnp.dot(act.astype(BF16), w2_sc[...], preferred_element_type=F32) + b2_ref[0])

    @pl.when(i >= nu_ref[0])
    def _():
        o_ref[...] = jnp.zeros(o_ref.shape, o_ref.dtype)


def _experts(hs, blk_e, n_used, w1_all, b1, w2_all, b2, layer):
    _, ne, d, dff2 = w1_all.shape
    cap = hs.shape[0] // TOK_ROWS
    dff = dff2 // 2
    bm = MOE_BM
    tile_spec = pl.BlockSpec((bm * TOK_ROWS, LANES), lambda i, be, nu: (i, 0))
    grid_spec = pltpu.PrefetchScalarGridSpec(
        num_scalar_prefetch=2,
        grid=(cap // bm,),
        in_specs=[tile_spec,
                  pl.BlockSpec((None, 1, d, dff2), lambda i, be, nu: (layer, be[i], 0, 0)),
                  pl.BlockSpec((1, 1, dff2), lambda i, be, nu: (be[i], 0, 0)),
                  pl.BlockSpec((None, 1, dff, d), lambda i, be, nu: (layer, be[i], 0, 0)),
                  pl.BlockSpec((1, 1, d), lambda i, be, nu: (be[i], 0, 0))],
        out_specs=tile_spec,
        scratch_shapes=[pltpu.VMEM((d, dff2), BF16), pltpu.VMEM((dff, d), BF16)],
    )
    return pl.pallas_call(
        _expert_kernel,
        grid_spec=grid_spec,
        out_shape=jax.ShapeDtypeStruct(hs.shape, F32),
        compiler_params=_cp(("arbitrary",)),
    )(blk_e, n_used, hs, w1_all, b1.reshape(ne, 1, dff2), w2_all, b2.reshape(ne, 1, d))


def _combine_kernel(pos_ref, pos_next_ref, x_ref, gt_ref, g2_ref, out_ref, o_ref, y_sc, sem):
    t = pl.program_id(0)
    n_tok = x_ref.shape[0]
    slot = t % 2

    def gather(p_ref, sl):
        def issue(r, carry):
            for kk in range(TOP_K):
                pltpu.make_async_copy(_token_rows(out_ref, p_ref[r * TOP_K + kk]),
                                      _token_rows(y_sc.at[sl * TOP_K + kk], r), sem.at[sl]).start(priority=kk % 2)
            return carry

        lax.fori_loop(0, n_tok, issue, 0)

    @pl.when(t == 0)
    def _():
        gather(pos_ref, 0)

    @pl.when(t + 1 < pl.num_programs(0))
    def _():
        gather(pos_next_ref, 1 - slot)

    def drain(r, carry):
        for kk in range(TOP_K):
            pltpu.make_async_copy(_token_rows(out_ref, 0), _token_rows(y_sc.at[slot * TOP_K + kk], 0), sem.at[slot]).wait()
        return carry

    lax.fori_loop(0, n_tok, drain, 0)
    gt = gt_ref[...]
    for j in range(TOK_ROWS):
        sl = slice(j * LANES, (j + 1) * LANES)
        acc = y_sc[slot * TOP_K, pl.ds(j, n_tok, stride=TOK_ROWS), :] * gt[:, 0:1]
        for kk in range(1, TOP_K):
            acc = acc + y_sc[slot * TOP_K + kk, pl.ds(j, n_tok, stride=TOK_ROWS), :] * gt[:, kk:kk + 1]
        o_ref[:, sl] = x_ref[:, sl] + g2_ref[0, :, sl] * acc


def _combine(x, out_tiles, pos, gates, mod_l, tpb, n_batch):
    n, d = x.shape
    tm = ROW_TILE
    n_lat_tiles = tpb * n_batch

    def mod_row(t):
        return jnp.where(t < n_lat_tiles, t // tpb, n_batch)

    n_tiles = n // tm
    return pl.pallas_call(
        _combine_kernel,
        grid=(n_tiles,),
        in_specs=[pl.BlockSpec((tm * TOP_K,), lambda t: (t,), memory_space=pltpu.SMEM),
                  pl.BlockSpec((tm * TOP_K,), lambda t: (jnp.minimum(t + 1, n_tiles - 1),), memory_space=pltpu.SMEM),
                  pl.BlockSpec((tm, d), lambda t: (t, 0)), pl.BlockSpec((tm, LANES), lambda t: (t, 0)),
                  pl.BlockSpec((1, 1, d), lambda t: (mod_row(t), 0, 5)), pl.BlockSpec(memory_space=pl.ANY)],
        out_specs=pl.BlockSpec((tm, d), lambda t: (t, 0)),
        out_shape=jax.ShapeDtypeStruct((n, d), F32),
        scratch_shapes=[pltpu.VMEM((2 * TOP_K, tm * TOK_ROWS, LANES), F32), pltpu.SemaphoreType.DMA((2,))],
        compiler_params=_cp(("arbitrary",)),
    )(pos, pos, x, gates, mod_l, out_tiles)


def _moe(m_tiles, top_e, gates, x, mod_l, w1_all, b1, w2_all, b2, layer, tpb, n_batch):
    n, d = x.shape
    nk = n * TOP_K
    bm = MOE_BM
    n_blk = (nk + N_EXPERTS * (bm - 1)) // bm + 1
    cap = n_blk * bm
    flat_e = top_e[:, :TOP_K].reshape(nk)
    onehot = (flat_e[:, None] == jnp.arange(N_EXPERTS, dtype=jnp.int32)[None, :]).astype(jnp.int32)
    csum = jnp.cumsum(onehot, axis=0)
    rank = jnp.sum(jnp.where(onehot > 0, csum, 0), axis=1) - 1
    counts = csum[-1]
    padded = ((counts + bm - 1) // bm) * bm
    pend = jnp.cumsum(padded)
    pstart = pend - padded
    pos = (pstart[flat_e] + rank).astype(jnp.int32)
    n_used = (pend[-1] // bm).astype(jnp.int32)
    blk = jnp.minimum(jnp.arange(n_blk, dtype=jnp.int32), n_used - 1)
    blk_e = jnp.sum((pend[None, :] <= (blk * bm)[:, None]).astype(jnp.int32), axis=1)
    blk_e = jnp.clip(blk_e, 0, N_EXPERTS - 1).astype(jnp.int32)
    hs = _dispatch(m_tiles, pos, (pstart + counts).astype(jnp.int32), (padded - counts).astype(jnp.int32), cap)
    out = _experts(hs, blk_e, n_used.reshape(1), w1_all, b1, w2_all, b2, layer)
    return _combine(x, out, pos, gates, mod_l, tpb, n_batch)


def _rope_tables(seq_len, d_rot, reps, n_extra):
    t = jnp.arange(seq_len)
    rows = (t // GRID_W).astype(F32)
    cols = (t % GRID_W).astype(F32)
    n_freq = d_rot // 4
    inv = ROPE_THETA ** (-jnp.arange(n_freq, dtype=F32) / n_freq)
    ang = jnp.concatenate([rows[:, None] * inv, cols[:, None] * inv], axis=-1)
    cos = jnp.repeat(jnp.cos(ang), 2, axis=-1)
    sin = jnp.repeat(jnp.sin(ang), 2, axis=-1) * jnp.tile(jnp.asarray([-1.0, 1.0], F32), d_rot // 2)
    cos = jnp.concatenate([jnp.tile(cos, (1, reps)), jnp.ones((n_extra, d_rot * reps), F32)], axis=0)
    sin = jnp.concatenate([jnp.tile(sin, (1, reps)), jnp.zeros((n_extra, d_rot * reps), F32)], axis=0)
    return cos, sin


def _retention_tables(decay_logit):
    log_g = jax.nn.log_sigmoid(decay_logit.astype(F32))
    c = RET_CHUNK
    pos = jnp.arange(c, dtype=F32)
    diff = pos[:, None] - pos[None, :]
    lf = log_g[0][:, None, None]
    lb = log_g[1][:, None, None]
    dm_f = jnp.where(diff >= 0, jnp.exp(lf * jnp.where(diff >= 0, diff, 0.0)), 0.0)
    dm_b = jnp.where(diff < 0, jnp.exp(lb * jnp.where(diff < 0, -diff, 0.0)), 0.0)
    qd_f = jnp.exp(log_g[0][:, None] * (pos + 1.0))
    qd_b = jnp.exp(log_g[1][:, None] * (c - pos))
    kd_f = jnp.exp(log_g[0][:, None] * (c - 1.0 - pos))
    kd_b = jnp.exp(log_g[1][:, None] * pos)
    bc = lambda a: jnp.broadcast_to(a[..., None], a.shape + (LANES,))
    dmask = jnp.stack([dm_f, dm_b])
    qdec = jnp.stack([bc(qd_f), bc(qd_b)])
    kdec = jnp.stack([bc(kd_f), bc(kd_b)])
    cdec = jnp.exp(log_g * c).reshape(-1)
    return dmask, qdec, kdec, cdec


def _heads_major(t, n_batch, length, n_heads):
    return t.reshape(n_batch, length, n_heads, -1).transpose(0, 2, 1, 3)


def _heads_t(t, n_batch, length, n_heads):
    return t.reshape(n_batch, length, n_heads, -1).transpose(0, 2, 3, 1)


def kernel(x, c, ctx, c_ctx, norm1_g, norm2_g, w_mod, b_mod, w_in_even, w_out_even, a_q_norm, a_k_norm, b_q_norm, b_k_norm, b_rpb, w_in_odd, w_out_odd, ret_decay, ret_gn, mla_cq_norm, mla_ckv_norm, w_uq, w_ukv, mla_q_norm, mla_k_norm, w_router, b_router, w_exp1, b_exp1, w_exp2, b_exp2):
    bsz, s, d = x.shape
    cl = ctx.shape[1]
    depth = w_mod.shape[0]
    tm = ROW_TILE
    assert s % tm == 0 and (bsz * cl) == tm and s % NA_QB == 0 and bsz + 1 <= 8 and d == TOK_ROWS * LANES
    tpb = s // tm
    nl = bsz * s
    n_all = nl + bsz * cl

    c_rows = jnp.zeros((8, d), F32).at[:bsz].set(c).at[bsz].set(c_ctx)
    mod = _mod_vectors(c_rows, w_mod, b_mod)

    cos_a, sin_a = _rope_tables(s, HEAD_DIM, LANES // HEAD_DIM, tm)
    cos_c, sin_c = _rope_tables(s, RET_DK, 1, tm)
    cos_d, sin_d = _rope_tables(s, MLA_ROPE, LANES // MLA_ROPE, tm)
    na_bases, na_var, na_row_sel, na_col_in, na_dc_onehot = _na_tables(s)

    x_all = jnp.concatenate([x.reshape(nl, d), ctx.reshape(bsz * cl, d)], axis=0)
    lat = lambda t: t[:nl]
    cx = lambda t: t[nl:]
    att_scale = HEAD_DIM ** -0.5 * LOG2E
    grp = GQA_HEADS // GQA_KV_HEADS
    bkv = bsz * GQA_KV_HEADS

    for l in range(depth):
        need_ctx = l < depth - 1
        i = l // 2
        mod_l = mod[l].reshape(8, 1, 6 * d)
        if l % 2 == 0:
            wq, wk, wv, wn = GQA_HEADS * HEAD_DIM, GQA_KV_HEADS * HEAD_DIM, GQA_KV_HEADS * HEAD_DIM, NA_HEADS * HEAD_DIM
            starts = np.cumsum([0, wq, wk, wv, wn, wn])
            segs = [(int(starts[0]), wq, "norm_rope", 1.0), (int(starts[1]), wk, "norm_rope", 1.0),
                    (int(starts[2]), wv, "plain", 1.0), (int(starts[3]), wn, "norm", 1.0),
                    (int(starts[4]), wn, "norm", 1.0), (int(starts[5]), wn, "plain", 1.0)]
            gain = jnp.concatenate([jnp.tile(a_q_norm[i], GQA_HEADS) * att_scale, jnp.tile(a_k_norm[i], GQA_KV_HEADS),
                                    jnp.ones((wv,), F32), jnp.tile(b_q_norm[i], NA_HEADS) * att_scale,
                                    jnp.tile(b_k_norm[i], NA_HEADS), jnp.ones((wn,), F32)]).reshape(1, -1)
            qa, ka, va, qb, kb, vb = _proj(x_all, mod_l, norm1_g[l], w_in_even[i].astype(BF16), gain, cos_a, sin_a,
                                           segs, [BF16] * 6, tpb, bsz)
            k_lat = _heads_major(lat(ka), bsz, s, GQA_KV_HEADS)
            k_cx = _heads_major(cx(ka), bsz, cl, GQA_KV_HEADS)
            v_lat = _heads_major(lat(va), bsz, s, GQA_KV_HEADS)
            v_cx = _heads_major(cx(va), bsz, cl, GQA_KV_HEADS)
            k_all = jnp.concatenate([k_lat, k_cx], axis=2).reshape(bkv, s + cl, HEAD_DIM)
            v_all = jnp.concatenate([v_lat, v_cx], axis=2).reshape(bkv, s + cl, HEAD_DIM)
            qt = _heads_t(lat(qa), bsz, s, GQA_HEADS).reshape(bkv, grp, HEAD_DIM, s)
            oa_t = _flash(qt, k_all, v_all, Q_SUB)
            oa = oa_t.reshape(bsz, GQA_HEADS, HEAD_DIM, s).transpose(0, 3, 1, 2).reshape(nl, wq)
            bias = _na_bias(b_rpb[i], na_row_sel, na_col_in, na_dc_onehot)
            bh = bsz * NA_HEADS
            qn = _heads_major(lat(qb), bsz, s, NA_HEADS).reshape(bh, s, HEAD_DIM)
            kn = _heads_major(lat(kb), bsz, s, NA_HEADS).reshape(bh, s, HEAD_DIM)
            vn = _heads_major(lat(vb), bsz, s, NA_HEADS).reshape(bh, s, HEAD_DIM)
            kn_c = _heads_major(cx(kb), bsz, cl, NA_HEADS).reshape(bh, cl, HEAD_DIM)
            vn_c = _heads_major(cx(vb), bsz, cl, NA_HEADS).reshape(bh, cl, HEAD_DIM)
            ob = _na_attention(qn, kn, vn, kn_c, vn_c, bias, jnp.asarray(na_bases), jnp.asarray(na_var), NA_HEADS)
            ob = ob.reshape(bsz, NA_HEADS, s, HEAD_DIM).transpose(0, 2, 1, 3).reshape(nl, wn)
            if need_ctx:
                qt_c = _heads_t(cx(qa), bsz, cl, GQA_HEADS).reshape(bkv, grp, HEAD_DIM, cl)
                oa_c = _flash(qt_c, k_cx.reshape(bkv, cl, HEAD_DIM), v_cx.reshape(bkv, cl, HEAD_DIM), Q_SUB)
                oa_c = oa_c.reshape(bsz, GQA_HEADS, HEAD_DIM, cl).transpose(0, 3, 1, 2).reshape(bsz * cl, wq)
                qnt_c = _heads_t(cx(qb), bsz, cl, NA_HEADS).reshape(bh, 1, HEAD_DIM, cl)
                ob_c = _flash(qnt_c, kn_c, vn_c, Q_SUB)
                ob_c = ob_c.reshape(bsz, NA_HEADS, HEAD_DIM, cl).transpose(0, 3, 1, 2).reshape(bsz * cl, wn)
                a1 = jnp.concatenate([oa, oa_c], axis=0)
                a2 = jnp.concatenate([ob, ob_c], axis=0)
            else:
                a1, a2 = oa, ob
            w_out = w_out_even[i].astype(BF16)
        else:
            rw = RET_HEADS * RET_DK
            kr_cols = w_in_odd[i][:, 4 * rw + MLA_Q_LORA + MLA_KV_LORA:]
            w_ext = jnp.concatenate([w_in_odd[i]] + [kr_cols] * (LANES * 2 // MLA_ROPE - 1), axis=1).astype(BF16)
            mla_w = MLA_Q_LORA + MLA_KV_LORA + 2 * LANES
            segs = [(0, rw, "rope", RET_DK ** -0.5), (rw, rw, "rope", 1.0), (2 * rw, rw, "plain", 1.0),
                    (3 * rw, rw, "plain", 1.0), (4 * rw, mla_w, "plain", 1.0)]
            gain = jnp.ones((1, w_ext.shape[1]), F32)
            rq, rk, rv, rg, mla_in = _proj(x_all, mod_l, norm1_g[l], w_ext, gain, cos_c, sin_c, segs,
                                           [BF16, BF16, BF16, F32, F32], tpb, bsz)
            dmask, qdec, kdec, cdec = _retention_tables(ret_decay[i])
            yf, yb = _retention(rq, rk, rv, dmask, qdec, kdec, cdec, bsz, s, cl)
            a1 = _ret_finish(yf, yb, rg, ret_gn[i].reshape(1, rw))
            dqk = MLA_NOPE + MLA_ROPE
            perm_q = np.concatenate([np.arange(h * dqk, h * dqk + MLA_NOPE) for h in range(MLA_HEADS)]
                                    + [np.arange(h * dqk + MLA_NOPE, (h + 1) * dqk) for h in range(MLA_HEADS)])
            dkv = MLA_NOPE + MLA_V
            perm_kv = np.concatenate([np.arange(h * dkv, h * dkv + MLA_NOPE) for h in range(MLA_HEADS)]
                                     + [np.arange(h * dkv + MLA_NOPE, (h + 1) * dkv) for h in range(MLA_HEADS)])
            mla_scale = dqk ** -0.5 * LOG2E
            gq = jnp.concatenate([jnp.tile(mla_q_norm[i][:MLA_NOPE], MLA_HEADS),
                                  jnp.tile(mla_q_norm[i][MLA_NOPE:], MLA_HEADS)]).reshape(1, -1) * mla_scale
            gkn = mla_k_norm[i][:MLA_NOPE].reshape(1, -1)
            gkr = jnp.tile(mla_k_norm[i][MLA_NOPE:], LANES // MLA_ROPE).reshape(1, -1)
            q_m, k_m, v_m = _mla_proj(mla_in, w_uq[i][:, perm_q].astype(BF16), w_ukv[i][:, perm_kv].astype(BF16),
                                      mla_cq_norm[i].reshape(1, -1), mla_ckv_norm[i].reshape(1, -1), gq, gkn, gkr,
                                      cos_d, sin_d, tpb, bsz)
            nw = MLA_HEADS * MLA_NOPE

            def qk_heads(t, length, transposed):
                nope = t[:, :nw].reshape(bsz, length, MLA_HEADS, MLA_NOPE)
                rope = t[:, nw:].reshape(bsz, length, MLA_HEADS, MLA_ROPE)
                full = jnp.concatenate([nope, rope], axis=-1)
                return full.transpose(0, 2, 3, 1) if transposed else full.transpose(0, 2, 1, 3)

            bhm = bsz * MLA_HEADS
            k_lat = qk_heads(lat(k_m), s, False)
            k_cx = qk_heads(cx(k_m), cl, False)
            v_lat = _heads_major(lat(v_m), bsz, s, MLA_HEADS)
            v_cx = _heads_major(cx(v_m), bsz, cl, MLA_HEADS)
            k_all = jnp.concatenate([k_lat, k_cx], axis=2).reshape(bhm, s + cl, dqk)
            v_all = jnp.concatenate([v_lat, v_cx], axis=2).reshape(bhm, s + cl, MLA_V)
            qt = qk_heads(lat(q_m), s, True).reshape(bhm, 1, dqk, s)
            om = _flash(qt, k_all, v_all, 4 * Q_SUB)
            om = om.reshape(bsz, MLA_HEADS, MLA_V, s).transpose(0, 3, 1, 2).reshape(nl, MLA_HEADS * MLA_V)
            if need_ctx:
                qt_c = qk_heads(cx(q_m), cl, True).reshape(bhm, 1, dqk, cl)
                om_c = _flash(qt_c, k_cx.reshape(bhm, cl, dqk), v_cx.reshape(bhm, cl, MLA_V), Q_SUB)
                om_c = om_c.reshape(bsz, MLA_HEADS, MLA_V, cl).transpose(0, 3, 1, 2).reshape(bsz * cl, MLA_HEADS * MLA_V)
                a2 = jnp.concatenate([om, om_c], axis=0)
            else:
                a1 = a1[:nl]
                a2 = om
            w_out = w_out_odd[i].astype(BF16)

        n_rows = n_all if need_ctx else nl
        w_r_hi = w_router[l].astype(BF16)
        w_r_lo = (w_router[l] - w_r_hi.astype(F32)).astype(BF16)
        w_r = (jnp.zeros((d, 2 * LANES), BF16).at[:, :N_EXPERTS].set(w_r_hi)
               .at[:, LANES:LANES + N_EXPERTS].set(w_r_lo))
        b_r = jnp.full((1, LANES), NEG, F32).at[0, :N_EXPERTS].set(b_router[l])
        x_new, m, top_e, gates = _out_proj(a1, a2, x_all, w_out, mod_l, norm2_g[l], w_r, b_r, n_rows, tpb, bsz)
        x_all = _moe(m, top_e, gates, x_new, mod_l, w_exp1, b_exp1[l], w_exp2, b_exp2[l], l, tpb, bsz)
    return x_all[:nl].reshape(bsz, s, d)
```

```python
import functools
import math

import numpy as np
import jax
import jax.numpy as jnp
from jax import lax
from jax.experimental import pallas as pl
from jax.experimental.pallas import tpu as pltpu

F32 = jnp.float32
BF16 = jnp.bfloat16

GRID_W = 64
HEAD_DIM = 64
GQA_HEADS = 8
GQA_KV_HEADS = 2
NA_HEADS = 8
NA_ROWS = 8
NA_COLS = 16
RET_HEADS = 4
RET_DK = 128
MLA_HEADS = 4
MLA_Q_LORA = 256
MLA_KV_LORA = 128
MLA_NOPE = 128
MLA_ROPE = 64
MLA_V = 128
N_EXPERTS = 32
TOP_K = 4
SWIGLU_LIMIT = 7.0
SWIGLU_ALPHA = 1.702
ROPE_THETA = 10000.0
EPS = 1e-6
GN_EPS = 1e-5
LOG2E = math.log2(math.e)
NEG = -1e30

LANES = 128
ROW_TILE = 512
Q_SUB = 256
ONES_ROWS = 16
NA_QB = 256
NA_WIN_ROWS = 12
RET_CHUNK = 256
MOE_BM = 512
TOK_ROWS = 8
VMEM_LIMIT = 56 * 1024 * 1024


def _cp(sem):
    return pltpu.CompilerParams(dimension_semantics=sem, vmem_limit_bytes=VMEM_LIMIT)


def _mod_kernel(c_ref, w_ref, b_ref, o_ref):
    c = c_ref[...]
    s = c * jax.nn.sigmoid(c)
    o_ref[0] = jnp.dot(s, w_ref[0], precision=lax.Precision.HIGHEST, preferred_element_type=F32) + b_ref[0]


def _mod_vectors(c_rows, w_mod, b_mod):
    depth, d, d6 = w_mod.shape
    tn = 1536
    return pl.pallas_call(
        _mod_kernel,
        grid=(depth, d6 // tn),
        in_specs=[pl.BlockSpec((8, d), lambda l, j: (0, 0)),
                  pl.BlockSpec((1, d, tn), lambda l, j: (l, 0, j)),
                  pl.BlockSpec((1, 1, tn), lambda l, j: (l, 0, j))],
        out_specs=pl.BlockSpec((1, 8, tn), lambda l, j: (l, 0, j)),
        out_shape=jax.ShapeDtypeStruct((depth, 8, d6), F32),
        compiler_params=_cp(("parallel", "parallel")),
    )(c_rows, w_mod, b_mod.reshape(depth, 1, d6))


def _modulated_norm(x, g, sc, sh):
    ms = jnp.mean(x * x, axis=-1, keepdims=True)
    return x * lax.rsqrt(ms + EPS) * g * (1.0 + sc) + sh


def _pair_rope(y, cos, sin_signed):
    lane = lax.broadcasted_iota(jnp.int32, y.shape, 1)
    partner = jnp.where((lane & 1) == 0, pltpu.roll(y, LANES - 1, 1), pltpu.roll(y, 1, 1))
    return y * cos + partner * sin_signed


def _store_token_tiles(ref, val):
    n = val.shape[0]
    for j in range(TOK_ROWS):
        ref[pl.ds(j, n, stride=TOK_ROWS), :] = val[:, j * LANES:(j + 1) * LANES]


def _load_token_tiles(ref, n):
    return jnp.concatenate([ref[pl.ds(j, n, stride=TOK_ROWS), :] for j in range(TOK_ROWS)], axis=-1)


def _split_dot(a_f32, w_bf16):
    hi = a_f32.astype(BF16)
    lo = (a_f32 - hi.astype(F32)).astype(BF16)
    return (jnp.dot(hi, w_bf16, preferred_element_type=F32) + jnp.dot(lo, w_bf16, preferred_element_type=F32))


def _proj_kernel(x_ref, sh_ref, sc_ref, g_ref, w_ref, gain_ref, bd_ref, cos_ref, sin_ref, *out_refs, segs):
    a = _modulated_norm(x_ref[...], g_ref[...], sc_ref[0], sh_ref[0]).astype(BF16)
    cos = cos_ref[...]
    sin = sin_ref[...]
    for (start, width, mode, scale), o_ref in zip(segs, out_refs):
        y_seg = jnp.dot(a, w_ref[:, start:start + width], preferred_element_type=F32)
        if mode == "plain":
            o_ref[...] = y_seg.astype(o_ref.dtype)
            continue
        for j in range(width // LANES):
            y = y_seg[:, j * LANES:(j + 1) * LANES]
            if "norm" in mode:
                ms = _split_dot(y * y, bd_ref[...])
                y = y * lax.rsqrt(ms + EPS) * gain_ref[:, start + j * LANES:start + (j + 1) * LANES]
            if scale != 1.0:
                y = y * scale
            if "rope" in mode:
                y = _pair_rope(y, cos, sin)
            o_ref[:, j * LANES:(j + 1) * LANES] = y.astype(o_ref.dtype)


def _proj(x_all, mod_l, norm_g, w, gain, cos_t, sin_t, segs, out_dtypes, n_lat_tiles_per_batch, n_batch):
    n, d = x_all.shape
    tm = ROW_TILE
    n_tiles = n // tm
    wtot = w.shape[1]
    tpb = n_lat_tiles_per_batch
    n_lat_tiles = tpb * n_batch

    def mod_row(t):
        return jnp.where(t < n_lat_tiles, t // tpb, n_batch)

    def rope_row(t):
        return jnp.where(t < n_lat_tiles, t % tpb, tpb)

    bd = np.kron(np.eye(2, dtype=np.float32), np.full((HEAD_DIM, HEAD_DIM), 1.0 / HEAD_DIM, np.float32))
    in_specs = [
        pl.BlockSpec((tm, d), lambda t: (t, 0)),
        pl.BlockSpec((1, 1, d), lambda t: (mod_row(t), 0, 0)),
        pl.BlockSpec((1, 1, d), lambda t: (mod_row(t), 0, 1)),
        pl.BlockSpec((1, d), lambda t: (0, 0)),
        pl.BlockSpec((d, wtot), lambda t: (0, 0)),
        pl.BlockSpec((1, wtot), lambda t: (0, 0)),
        pl.BlockSpec((LANES, LANES), lambda t: (0, 0)),
        pl.BlockSpec((tm, LANES), lambda t: (rope_row(t), 0)),
        pl.BlockSpec((tm, LANES), lambda t: (rope_row(t), 0)),
    ]
    out_specs = [pl.BlockSpec((tm, s[1]), lambda t: (t, 0)) for s in segs]
    out_shape = [jax.ShapeDtypeStruct((n, s[1]), dt) for s, dt in zip(segs, out_dtypes)]
    return pl.pallas_call(
        functools.partial(_proj_kernel, segs=tuple(segs)),
        grid=(n_tiles,),
        in_specs=in_specs,
        out_specs=out_specs,
        out_shape=out_shape,
        compiler_params=_cp(("parallel",)),
    )(x_all, mod_l, mod_l, norm_g.reshape(1, d), w, gain, jnp.asarray(bd, BF16), cos_t, sin_t)


def _mla_proj_kernel(x_ref, wuq_ref, wukv_ref, gcq_ref, gckv_ref, gq_ref, gkn_ref, gkr_ref, cos_ref, sin_ref,
                     q_ref, k_ref, v_ref):
    x = x_ref[...]
    cq = x[:, :MLA_Q_LORA]
    ckv = x[:, MLA_Q_LORA:MLA_Q_LORA + MLA_KV_LORA]
    kr = x[:, MLA_Q_LORA + MLA_KV_LORA:MLA_Q_LORA + MLA_KV_LORA + LANES]
    cos = cos_ref[...]
    sin = sin_ref[...]
    cqn = cq * lax.rsqrt(jnp.mean(cq * cq, axis=-1, keepdims=True) + EPS) * gcq_ref[...]
    ckvn = ckv * lax.rsqrt(jnp.mean(ckv * ckv, axis=-1, keepdims=True) + EPS) * gckv_ref[...]
    q = jnp.dot(cqn.astype(BF16), wuq_ref[...], preferred_element_type=F32)
    kv = jnp.dot(ckvn.astype(BF16), wukv_ref[...], preferred_element_type=F32)
    nh = MLA_HEADS
    d_qk = float(MLA_NOPE + MLA_ROPE)
    lane = lax.broadcasted_iota(jnp.int32, (1, LANES), 1)
    low = lane < MLA_ROPE

    def half_sums(slab):
        sq = slab * slab
        a = jnp.sum(jnp.where(low, sq, 0.0), axis=-1, keepdims=True)
        return a, jnp.sum(sq, axis=-1, keepdims=True) - a

    rope_w = nh * MLA_NOPE
    q_rope_ss = []
    for r in range(nh // 2):
        q_rope_ss.extend(half_sums(q[:, rope_w + r * LANES:rope_w + (r + 1) * LANES]))
    kr_ss, _ = half_sums(kr)
    rs_q, rs_k = [], []
    for h in range(nh):
        qn = q[:, h * MLA_NOPE:(h + 1) * MLA_NOPE]
        kn = kv[:, h * MLA_NOPE:(h + 1) * MLA_NOPE]
        rs_q.append(lax.rsqrt((jnp.sum(qn * qn, axis=-1, keepdims=True) + q_rope_ss[h]) / d_qk + EPS))
        rs_k.append(lax.rsqrt((jnp.sum(kn * kn, axis=-1, keepdims=True) + kr_ss) / d_qk + EPS))
        q_ref[:, h * MLA_NOPE:(h + 1) * MLA_NOPE] = (qn * rs_q[h] * gq_ref[:, h * MLA_NOPE:(h + 1) * MLA_NOPE]).astype(q_ref.dtype)
        k_ref[:, h * MLA_NOPE:(h + 1) * MLA_NOPE] = (kn * rs_k[h] * gkn_ref[...]).astype(k_ref.dtype)
    kr_rot = _pair_rope(kr * gkr_ref[...], cos, sin)
    for r in range(nh // 2):
        sl = slice(rope_w + r * LANES, rope_w + (r + 1) * LANES)
        yq = q[:, sl] * jnp.where(low, rs_q[2 * r], rs_q[2 * r + 1]) * gq_ref[:, sl]
        q_ref[:, sl] = _pair_rope(yq, cos, sin).astype(q_ref.dtype)
        k_ref[:, sl] = (kr_rot * jnp.where(low, rs_k[2 * r], rs_k[2 * r + 1])).astype(k_ref.dtype)
    v_ref[...] = kv[:, nh * MLA_NOPE:].astype(v_ref.dtype)


def _mla_proj(mla_in, wuq, wukv, gcq, gckv, gq, gkn, gkr, cos_t, sin_t, tpb, n_batch):
    n, win = mla_in.shape
    tm = ROW_TILE
    n_lat_tiles = tpb * n_batch

    def rope_row(t):
        return jnp.where(t < n_lat_tiles, t % tpb, tpb)

    qk_w = MLA_HEADS * (MLA_NOPE + MLA_ROPE)
    v_w = MLA_HEADS * MLA_V
    full = lambda a: pl.BlockSpec(a.shape, lambda t: (0,) * a.ndim)
    return pl.pallas_call(
        _mla_proj_kernel,
        grid=(n // tm,),
        in_specs=[pl.BlockSpec((tm, win), lambda t: (t, 0)), full(wuq), full(wukv), full(gcq), full(gckv),
                  full(gq), full(gkn), full(gkr),
                  pl.BlockSpec((tm, LANES), lambda t: (rope_row(t), 0)),
                  pl.BlockSpec((tm, LANES), lambda t: (rope_row(t), 0))],
        out_specs=[pl.BlockSpec((tm, qk_w), lambda t: (t, 0)), pl.BlockSpec((tm, qk_w), lambda t: (t, 0)),
                   pl.BlockSpec((tm, v_w), lambda t: (t, 0))],
        out_shape=[jax.ShapeDtypeStruct((n, qk_w), BF16), jax.ShapeDtypeStruct((n, qk_w), BF16),
                   jax.ShapeDtypeStruct((n, v_w), BF16)],
        compiler_params=_cp(("parallel",)),
    )(mla_in, wuq, wukv, gcq, gckv, gq, gkn, gkr, cos_t, sin_t)


def _flash_kernel(qt_ref, k_ref, vt_ref, ot_ref, q_sc, s_0, s_1, s_2, mx_0, mx_1, mx_2, m_sc, acc_sc, *,
                  g, n_chunks, dv):
    bufs = ((s_0, mx_0), (s_1, mx_1), (s_2, mx_2))
    tq = qt_ref.shape[3]
    for gi in range(g):
        q_sc[:, gi * tq:(gi + 1) * tq] = qt_ref[0, gi]
    m_sc[...] = jnp.full(m_sc.shape, NEG, F32)
    acc_sc[...] = jnp.zeros(acc_sc.shape, F32)

    def scores(ci, s_ref, mx_ref):
        s = jnp.dot(k_ref[0, ci], q_sc[...], preferred_element_type=F32)
        s_ref[...] = s
        mx_ref[...] = jnp.max(s, axis=0, keepdims=True)

    def accumulate(ci, s_ref, mx_ref):
        m_old = m_sc[...]
        m_new = jnp.maximum(m_old, mx_ref[...])
        alpha = jnp.exp2(m_old - m_new)
        p = jnp.exp2(s_ref[...] - m_new).astype(BF16)
        acc_sc[...] = alpha * acc_sc[...] + jnp.dot(vt_ref[0, ci], p, preferred_element_type=F32)
        m_sc[...] = m_new

    def fused(ci_n, s_n, mx_n, ci_c, s_c, mx_c):
        m_old = m_sc[...]
        m_new = jnp.maximum(m_old, mx_c[...])
        alpha = jnp.exp2(m_old - m_new)
        pv = None
        mx = None
        for k0 in range(0, s_c.shape[0], 256):
            s = jnp.dot(k_ref[0, ci_n, k0:k0 + 256, :], q_sc[...], preferred_element_type=F32)
            s_n[k0:k0 + 256, :] = s
            mxj = jnp.max(s, axis=0, keepdims=True)
            mx = mxj if mx is None else jnp.maximum(mx, mxj)
            p = jnp.exp2(s_c[k0:k0 + 256, :] - m_new).astype(BF16)
            part = jnp.dot(vt_ref[0, ci_c, :, k0:k0 + 256], p, preferred_element_type=F32)
            pv = part if pv is None else pv + part
        mx_n[...] = mx
        acc_sc[...] = alpha * acc_sc[...] + pv
        m_sc[...] = m_new

    scores(0, *bufs[0])
    if n_chunks == 1:
        accumulate(0, *bufs[0])
    else:
        scores(1, *bufs[1])
        n_fused = n_chunks - 2

        def triple(j, carry):
            c = 3 * j
            for r in range(3):
                fused(c + r + 2, *bufs[(r + 2) % 3], c + r, *bufs[r])
            return carry

        lax.fori_loop(0, n_fused // 3, triple, 0)
        for c in range(3 * (n_fused // 3), n_fused):
            fused(c + 2, *bufs[(c + 2) % 3], c, *bufs[c % 3])
        accumulate(n_chunks - 2, *bufs[(n_chunks - 2) % 3])
        accumulate(n_chunks - 1, *bufs[(n_chunks - 1) % 3])
    acc = acc_sc[...]
    o = acc[:dv] / acc[dv:dv + 1]
    for gi in range(g):
        ot_ref[0, gi] = o[:, gi * tq:(gi + 1) * tq].astype(ot_ref.dtype)


def _key_chunk(lk):
    for tk in (1280, 1024, 768, 512, 256):
        if lk % tk == 0:
            return tk
    raise ValueError(f"key length {lk} must be a multiple of 256")


def _flash(qt, k, v, tq_blk):
    bk, g, dq, lq = qt.shape
    lk, dv = v.shape[1], v.shape[2]
    tk = _key_chunk(lk)
    nch = lk // tk
    kc = k.reshape(bk, nch, tk, dq)
    dve = dv + ONES_ROWS
    vt = jnp.concatenate([v, jnp.ones((bk, lk, ONES_ROWS), v.dtype)], axis=-1)
    vt = vt.reshape(bk, nch, tk, dve).transpose(0, 1, 3, 2)
    tq_blk = min(tq_blk, lq)
    assert lq % tq_blk == 0 and tq_blk % LANES == 0
    w = g * tq_blk
    return pl.pallas_call(
        functools.partial(_flash_kernel, g=g, n_chunks=nch, dv=dv),
        grid=(bk, lq // tq_blk),
        in_specs=[pl.BlockSpec((1, g, dq, tq_blk), lambda b, i: (b, 0, 0, i)),
                  pl.BlockSpec((1, nch, tk, dq), lambda b, i: (b, 0, 0, 0)),
                  pl.BlockSpec((1, nch, dve, tk), lambda b, i: (b, 0, 0, 0))],
        out_specs=pl.BlockSpec((1, g, dv, tq_blk), lambda b, i: (b, 0, 0, i)),
        out_shape=jax.ShapeDtypeStruct((bk, g, dv, lq), BF16),
        scratch_shapes=[pltpu.VMEM((dq, w), BF16)] + [pltpu.VMEM((tk, w), F32)] * 3 + [pltpu.VMEM((1, w), F32)] * 4
                       + [pltpu.VMEM((dve, w), F32)],
        compiler_params=_cp(("parallel", "parallel")),
    )(qt, kc, vt)


def _na_tables(seq_len):
    rows_n = seq_len // GRID_W
    assert rows_n >= NA_WIN_ROWS and NA_ROWS <= rows_n
    nb = seq_len // NA_QB
    rpq = NA_QB // GRID_W
    band = NA_WIN_ROWS * GRID_W
    variants, var_id, bases = {}, [], []
    for j in range(nb):
        base = int(np.clip(rpq * j - NA_ROWS // 2, 0, rows_n - NA_WIN_ROWS))
        bases.append(base)
        t = np.arange(NA_QB) + j * NA_QB
        r, col = t // GRID_W, t % GRID_W
        r0 = np.clip(r - NA_ROWS // 2, 0, rows_n - NA_ROWS)
        c0 = np.clip(col - NA_COLS // 2, 0, GRID_W - NA_COLS)
        kk = np.arange(band)
        kr = base + kk // GRID_W
        kc = kk % GRID_W
        inside = ((kr[None] >= r0[:, None]) & (kr[None] < r0[:, None] + NA_ROWS)
                  & (kc[None] >= c0[:, None]) & (kc[None] < c0[:, None] + NA_COLS))
        rel = (kr[None] - r[:, None] + NA_ROWS - 1) * (2 * NA_COLS - 1) + (kc[None] - col[:, None] + NA_COLS - 1)
        tab = np.where(inside, rel, -1).astype(np.int32)
        assert (inside.sum(axis=1) == NA_ROWS * NA_COLS).all()
        key = tab.tobytes()
        if key not in variants:
            variants[key] = (len(variants), tab)
        var_id.append(variants[key][0])
    tabs = np.stack([v[1] for v in sorted(variants.values(), key=lambda kv: kv[0])])
    n_dr = 2 * NA_ROWS - 1
    n_dc = 2 * NA_COLS - 1
    col = np.arange(GRID_W)
    c0 = np.clip(col - NA_COLS // 2, 0, GRID_W - NA_COLS)
    col_in = (col[None] >= c0[:, None]) & (col[None] < c0[:, None] + NA_COLS)
    dc = col[None] - col[:, None] + NA_COLS - 1
    t5 = tabs.reshape(len(tabs), rpq, GRID_W, NA_WIN_ROWS, GRID_W)
    row_sel = np.full((len(tabs), rpq, NA_WIN_ROWS), n_dr, np.int32)
    for v in range(len(tabs)):
        for a in range(rpq):
            for i in range(NA_WIN_ROWS):
                blk = t5[v, a, :, i, :]
                if (blk >= 0).any():
                    dr = int(blk[blk >= 0][0]) // n_dc
                    assert (np.where(col_in, dr * n_dc + dc, -1) == blk).all()
                    row_sel[v, a, i] = dr
                else:
                    assert (blk < 0).all()
    dc_onehot = (dc[None] == np.arange(n_dc)[:, None, None]).astype(np.float32)
    return np.asarray(bases, np.int32), np.asarray(var_id, np.int32), row_sel, col_in, dc_onehot


def _na_bias(rpb, row_sel, col_in, dc_onehot):
    h = rpb.shape[0]
    n_dr, n_dc = 2 * NA_ROWS - 1, 2 * NA_COLS - 1
    t = jnp.einsum("hdj,jck->hdck", rpb.astype(F32).reshape(h, n_dr, n_dc) * LOG2E, jnp.asarray(dc_onehot),
                   precision=lax.Precision.HIGHEST)
    t = jnp.where(jnp.asarray(col_in)[None, None], t, NEG)
    t = jnp.concatenate([t, jnp.full((h, 1, GRID_W, GRID_W), NEG, F32)], axis=1)
    nv, rpq, nw = row_sel.shape
    b = t[:, jnp.asarray(row_sel)]
    return b.transpose(1, 0, 2, 4, 3, 5).reshape(nv, h, rpq * GRID_W, nw * GRID_W)


def _na_kernel(base_ref, var_ref, q_ref, k_ref, v_ref, kc_ref, vc_ref, bias_ref, o_ref):
    j = pl.program_id(1)
    band = NA_WIN_ROWS * GRID_W
    start = pl.multiple_of(base_ref[j] * GRID_W, GRID_W)
    q = q_ref[0]
    kw = k_ref[0, pl.ds(start, band), :]
    vw = v_ref[0, pl.ds(start, band), :]
    nt = (((1,), (1,)), ((), ()))
    s_win = lax.dot_general(q, kw, nt, preferred_element_type=F32) + bias_ref[0, 0]
    s_ctx = lax.dot_general(q, kc_ref[0], nt, preferred_element_type=F32)
    m = jnp.maximum(jnp.max(s_win, axis=-1, keepdims=True), jnp.max(s_ctx, axis=-1, keepdims=True))
    p_win = jnp.exp2(s_win - m)
    p_ctx = jnp.exp2(s_ctx - m)
    l = jnp.sum(p_win, axis=-1, keepdims=True) + jnp.sum(p_ctx, axis=-1, keepdims=True)
    o = (jnp.dot(p_win.astype(BF16), vw, preferred_element_type=F32)
         + jnp.dot(p_ctx.astype(BF16), vc_ref[0], preferred_element_type=F32))
    o_ref[0] = (o / l).astype(o_ref.dtype)


def _na_attention(q, k, v, k_ctx, v_ctx, bias, bases, var_id, n_heads):
    bh, s, d = q.shape
    cl = k_ctx.shape[1]
    band = NA_WIN_ROWS * GRID_W
    nb = s // NA_QB
    grid_spec = pltpu.PrefetchScalarGridSpec(
        num_scalar_prefetch=2,
        grid=(bh, nb),
        in_specs=[pl.BlockSpec((1, NA_QB, d), lambda b, j, bs, vr: (b, j, 0)),
                  pl.BlockSpec((1, s, d), lambda b, j, bs, vr: (b, 0, 0)),
                  pl.BlockSpec((1, s, d), lambda b, j, bs, vr: (b, 0, 0)),
                  pl.BlockSpec((1, cl, d), lambda b, j, bs, vr: (b, 0, 0)),
                  pl.BlockSpec((1, cl, d), lambda b, j, bs, vr: (b, 0, 0)),
                  pl.BlockSpec((1, 1, NA_QB, band), lambda b, j, bs, vr: (vr[j], b % n_heads, 0, 0))],
        out_specs=pl.BlockSpec((1, NA_QB, d), lambda b, j, bs, vr: (b, j, 0)),
    )
    return pl.pallas_call(
        _na_kernel,
        grid_spec=grid_spec,
        out_shape=jax.ShapeDtypeStruct((bh, s, d), BF16),
        compiler_params=_cp(("parallel", "arbitrary")),
    )(bases, var_id, q, k, v, k_ctx, v_ctx, bias)


def _ret_kernel(cdec_ref, qf_ref, kf_ref, vf_ref, qb_ref, kb_ref, vb_ref, dmask_ref, qdec_ref, kdec_ref,
                yf_ref, yb_ref, state_sc):
    @pl.when(pl.program_id(1) == 0)
    def _():
        state_sc[...] = jnp.zeros(state_sc.shape, F32)

    nt = (((1,), (1,)), ((), ()))
    tn = (((0,), (0,)), ((), ()))
    dk = RET_DK
    for d, (q_ref, k_ref, v_ref, y_ref) in enumerate(((qf_ref, kf_ref, vf_ref, yf_ref),
                                                      (qb_ref, kb_ref, vb_ref, yb_ref))):
        for h in range(RET_HEADS):
            sl = slice(h * dk, (h + 1) * dk)
            q = q_ref[:, sl]
            k = k_ref[:, sl]
            v = v_ref[:, sl]
            st = state_sc[d, h]
            a = lax.dot_general(q, k, nt, preferred_element_type=F32) * dmask_ref[d, h]
            inner = jnp.dot(a.astype(BF16), v, preferred_element_type=F32)
            cross = jnp.dot(q, st.astype(BF16), preferred_element_type=F32) * qdec_ref[d, h]
            y_ref[:, sl] = inner + cross
            vs = (v.astype(F32) * kdec_ref[d, h]).astype(BF16)
            state_sc[d, h] = st * cdec_ref[d * RET_HEADS + h] + lax.dot_general(k, vs, tn, preferred_element_type=F32)


def _retention(rq, rk, rv, dmask, qdec, kdec, cdec, n_batch, seq_len, ctx_len):
    n, w = rq.shape
    c = RET_CHUNK
    assert ctx_len == c and seq_len % c == 0
    ncl = seq_len // c
    ctx_blk0 = (n_batch * seq_len) // c

    def fwd(b, s, cd):
        return (jnp.where(s == 0, ctx_blk0 + b, b * ncl + s - 1), 0)

    def bwd(b, s, cd):
        return (jnp.where(s == 0, ctx_blk0 + b, b * ncl + ncl - s), 0)

    full = lambda a: pl.BlockSpec(a.shape, lambda b, s, cd: (0,) * a.ndim)
    grid_spec = pltpu.PrefetchScalarGridSpec(
        num_scalar_prefetch=1,
        grid=(n_batch, ncl + 1),
        in_specs=[pl.BlockSpec((c, w), fwd)] * 3 + [pl.BlockSpec((c, w), bwd)] * 3 + [full(dmask), full(qdec), full(kdec)],
        out_specs=[pl.BlockSpec((c, w), fwd), pl.BlockSpec((c, w), bwd)],
        scratch_shapes=[pltpu.VMEM((2, RET_HEADS, RET_DK, RET_DK), F32)],
    )
    return pl.pallas_call(
        _ret_kernel,
        grid_spec=grid_spec,
        out_shape=[jax.ShapeDtypeStruct((n, w), F32)] * 2,
        compiler_params=_cp(("parallel", "arbitrary")),
    )(cdec, rq, rk, rv, rq, rk, rv, dmask, qdec, kdec)


def _ret_finish_kernel(yf_ref, yb_ref, rg_ref, gn_ref, o_ref):
    y = yf_ref[...] + yb_ref[...]
    gate = rg_ref[...]
    gate = gate * jax.nn.sigmoid(gate)
    for h in range(RET_HEADS):
        sl = slice(h * RET_DK, (h + 1) * RET_DK)
        yh = y[:, sl]
        mu = jnp.mean(yh, axis=-1, keepdims=True)
        var = jnp.mean(jnp.square(yh - mu), axis=-1, keepdims=True)
        o_ref[:, sl] = ((yh - mu) * lax.rsqrt(var + GN_EPS) * gn_ref[:, sl] * gate[:, sl]).astype(o_ref.dtype)


def _ret_finish(yf, yb, rg, gn):
    n, w = yf.shape
    tm = ROW_TILE
    spec = pl.BlockSpec((tm, w), lambda t: (t, 0))
    return pl.pallas_call(
        _ret_finish_kernel,
        grid=(n // tm,),
        in_specs=[spec, spec, spec, pl.BlockSpec((1, w), lambda t: (0, 0))],
        out_specs=spec,
        out_shape=jax.ShapeDtypeStruct((n, w), BF16),
        compiler_params=_cp(("parallel",)),
    )(yf, yb, rg, gn)


def _out_kernel(a1_ref, a2_ref, x_ref, w_ref, g1_ref, ng_ref, sh_ref, sc_ref, wr_ref, br_ref,
                xo_ref, m_ref, e_ref, gt_ref):
    half = a1_ref.shape[1]
    o = (jnp.dot(a1_ref[...], w_ref[:half, :], preferred_element_type=F32)
         + jnp.dot(a2_ref[...], w_ref[half:, :], preferred_element_type=F32))
    x = x_ref[...] + g1_ref[0] * o
    xo_ref[...] = x
    m = _modulated_norm(x, ng_ref[...], sc_ref[0], sh_ref[0])
    _store_token_tiles(m_ref, m)
    m_hi = m.astype(BF16)
    m_lo = (m - m_hi.astype(F32)).astype(BF16)
    hi_prod = jnp.dot(m_hi, wr_ref[...], preferred_element_type=F32)
    logits = (hi_prod[:, :LANES] + hi_prod[:, LANES:]
              + jnp.dot(m_lo, wr_ref[:, :LANES], preferred_element_type=F32) + br_ref[...])
    lane = lax.broadcasted_iota(jnp.int32, logits.shape, 1).astype(F32)
    e_out = jnp.zeros(logits.shape, F32)
    g_out = jnp.zeros(logits.shape, F32)
    top0 = None
    denom = None
    for kk in range(TOP_K):
        mx = jnp.max(logits, axis=-1, keepdims=True)
        idx = jnp.min(jnp.where(logits == mx, lane, float(LANES)), axis=-1, keepdims=True)
        if kk == 0:
            top0 = mx
            ex = jnp.ones_like(mx)
            denom = ex
        else:
            ex = jnp.exp(mx - top0)
            denom = denom + ex
        e_out = jnp.where(lane == kk, idx, e_out)
        g_out = jnp.where(lane == kk, ex, g_out)
        logits = jnp.where(lane == idx, NEG * 2.0, logits)
    e_ref[...] = e_out.astype(jnp.int32)
    gt_ref[...] = g_out / denom


def _out_proj(a1, a2, x_all, w_out, mod_l, norm2_g, w_r, b_r, n_rows, tpb, n_batch):
    d = x_all.shape[1]
    half = a1.shape[1]
    tm = ROW_TILE
    n_lat_tiles = tpb * n_batch

    def mod_row(t):
        return jnp.where(t < n_lat_tiles, t // tpb, n_batch)

    row = lambda wd: pl.BlockSpec((tm, wd), lambda t: (t, 0))
    modspec = lambda col: pl.BlockSpec((1, 1, d), lambda t: (mod_row(t), 0, col))
    return pl.pallas_call(
        _out_kernel,
        grid=(n_rows // tm,),
        in_specs=[row(half), row(half), row(d), pl.BlockSpec((2 * half, d), lambda t: (0, 0)),
                  modspec(2), pl.BlockSpec((1, d), lambda t: (0, 0)), modspec(3), modspec(4),
                  pl.BlockSpec((d, 2 * LANES), lambda t: (0, 0)), pl.BlockSpec((1, LANES), lambda t: (0, 0))],
        out_specs=[row(d), pl.BlockSpec((tm * TOK_ROWS, LANES), lambda t: (t, 0)), row(LANES), row(LANES)],
        out_shape=[jax.ShapeDtypeStruct((n_rows, d), F32), jax.ShapeDtypeStruct((n_rows * TOK_ROWS, LANES), F32),
                   jax.ShapeDtypeStruct((n_rows, LANES), jnp.int32), jax.ShapeDtypeStruct((n_rows, LANES), F32)],
        compiler_params=_cp(("parallel",)),
    )(a1, a2, x_all, w_out, mod_l, norm2_g.reshape(1, d), mod_l, mod_l, w_r, b_r)


def _token_rows(ref, idx):
    return ref.at[pl.ds(pl.multiple_of(idx * TOK_ROWS, TOK_ROWS), TOK_ROWS)]


def _expert_kernel(be_ref, nv_ref, nu_ref, pair_ref, m_ref, w1_ref, b1_ref, w2_ref, b2_ref, y_ref,
                   x_sc, o_sc, w1_sc, w2_sc, gsem, ssem, *, n_tok):
    i = pl.program_id(0)
    nu = nu_ref[0]

    def scatter_done(count):
        def body(r, carry):
            pltpu.make_async_copy(_token_rows(o_sc, 0), _token_rows(y_ref, 0), ssem).wait()
            return carry

        lax.fori_loop(0, count, body, 0)

    @pl.when(i == 0)
    def _():
        x_sc[...] = jnp.zeros(x_sc.shape, x_sc.dtype)

    @pl.when((i == 0) | (be_ref[i] != be_ref[jnp.maximum(i - 1, 0)]))
    def _():
        w1_sc[...] = w1_ref[0].astype(BF16)
        w2_sc[...] = w2_ref[0].astype(BF16)

    @pl.when(i < nu)
    def _():
        nv = nv_ref[i]

        def gather(r, carry):
            pltpu.make_async_copy(_token_rows(m_ref, pair_ref[r] // TOP_K), _token_rows(x_sc, r), gsem).start()
            return carry

        lax.fori_loop(0, nv, gather, 0)

        def gather_done(r, carry):
            pltpu.make_async_copy(_token_rows(m_ref, 0), _token_rows(x_sc, 0), gsem).wait()
            return carry

        lax.fori_loop(0, nv, gather_done, 0)
        dff = w2_ref.shape[1]
        x = _load_token_tiles(x_sc, MOE_BM).astype(BF16)
        u = jnp.dot(x, w1_sc[...], preferred_element_type=F32) + b1_ref[0]
        gl = jnp.minimum(u[:, :dff], SWIGLU_LIMIT)
        up = jnp.clip(u[:, dff:], -SWIGLU_LIMIT, SWIGLU_LIMIT)
        act = gl * jax.nn.sigmoid(SWIGLU_ALPHA * gl) * (up + 1.0)
        o = jnp.dot(act.astype(BF16), w2_sc[...], preferred_element_type=F32) + b2_ref[0]

        @pl.when(i > 0)
        def _():
            scatter_done(nv_ref[jnp.maximum(i - 1, 0)])

        _store_token_tiles(o_sc, o)

        def scatter(r, carry):
            p = pair_ref[r]
            pltpu.make_async_copy(_token_rows(o_sc, r), _token_rows(y_ref, (p % TOP_K) * n_tok + p // TOP_K), ssem).start()
            return carry

        lax.fori_loop(0, nv, scatter, 0)

    @pl.when(i == nu)
    def _():
        scatter_done(nv_ref[jnp.maximum(i - 1, 0)])


def _experts(m_tiles, pair_of_slot, blk_e, blk_valid, n_used, w1_all, b1, w2_all, b2, layer):
    _, ne, d, dff2 = w1_all.shape
    n_tok = m_tiles.shape[0] // TOK_ROWS
    n_blk = blk_e.shape[0]
    dff = dff2 // 2
    bm = MOE_BM
    grid_spec = pltpu.PrefetchScalarGridSpec(
        num_scalar_prefetch=3,
        grid=(n_blk,),
        in_specs=[pl.BlockSpec((bm,), lambda i, be, nv, nu: (i,), memory_space=pltpu.SMEM),
                  pl.BlockSpec(memory_space=pl.ANY),
                  pl.BlockSpec((None, 1, d, dff2), lambda i, be, nv, nu: (layer, be[i], 0, 0)),
                  pl.BlockSpec((1, 1, dff2), lambda i, be, nv, nu: (be[i], 0, 0)),
                  pl.BlockSpec((None, 1, dff, d), lambda i, be, nv, nu: (layer, be[i], 0, 0)),
                  pl.BlockSpec((1, 1, d), lambda i, be, nv, nu: (be[i], 0, 0))],
        out_specs=pl.BlockSpec(memory_space=pl.ANY),
        scratch_shapes=[pltpu.VMEM((bm * TOK_ROWS, LANES), F32), pltpu.VMEM((bm * TOK_ROWS, LANES), F32),
                        pltpu.VMEM((d, dff2), BF16), pltpu.VMEM((dff, d), BF16),
                        pltpu.SemaphoreType.DMA(()), pltpu.SemaphoreType.DMA(())],
    )
    return pl.pallas_call(
        functools.partial(_expert_kernel, n_tok=n_tok),
        grid_spec=grid_spec,
        out_shape=jax.ShapeDtypeStruct((TOP_K * n_tok * TOK_ROWS, LANES), F32),
        compiler_params=pltpu.CompilerParams(dimension_semantics=("arbitrary",), has_side_effects=True,
                                             vmem_limit_bytes=VMEM_LIMIT),
    )(blk_e, blk_valid, n_used, pair_of_slot, m_tiles, w1_all, b1.reshape(ne, 1, dff2), w2_all, b2.reshape(ne, 1, d))


def _combine_kernel(x_ref, gt_ref, g2_ref, *rest):
    y_refs, o_ref = rest[:TOP_K], rest[TOP_K]
    n_tok = x_ref.shape[0]
    gt = gt_ref[...]
    for j in range(TOK_ROWS):
        sl = slice(j * LANES, (j + 1) * LANES)
        acc = y_refs[0][pl.ds(j, n_tok, stride=TOK_ROWS), :] * gt[:, 0:1]
        for kk in range(1, TOP_K):
            acc = acc + y_refs[kk][pl.ds(j, n_tok, stride=TOK_ROWS), :] * gt[:, kk:kk + 1]
        o_ref[:, sl] = x_ref[:, sl] + g2_ref[0, :, sl] * acc


def _combine(x, y_tiles, gates, mod_l, tpb, n_batch):
    n, d = x.shape
    tm = ROW_TILE
    n_lat_tiles = tpb * n_batch
    n_tiles = n // tm

    def mod_row(t):
        return jnp.where(t < n_lat_tiles, t // tpb, n_batch)

    y_specs = [pl.BlockSpec((tm * TOK_ROWS, LANES), functools.partial(lambda t, kk: (kk * n_tiles + t, 0), kk=kk))
               for kk in range(TOP_K)]
    return pl.pallas_call(
        _combine_kernel,
        grid=(n_tiles,),
        in_specs=[pl.BlockSpec((tm, d), lambda t: (t, 0)), pl.BlockSpec((tm, LANES), lambda t: (t, 0)),
                  pl.BlockSpec((1, 1, d), lambda t: (mod_row(t), 0, 5))] + y_specs,
        out_specs=pl.BlockSpec((tm, d), lambda t: (t, 0)),
        out_shape=jax.ShapeDtypeStruct((n, d), F32),
        compiler_params=_cp(("parallel",)),
    )(x, gates, mod_l, *([y_tiles] * TOP_K))


def _moe(m_tiles, top_e, gates, x, mod_l, w1_all, b1, w2_all, b2, layer, tpb, n_batch):
    n, d = x.shape
    nk = n * TOP_K
    bm = MOE_BM
    n_blk = (nk + N_EXPERTS * (bm - 1)) // bm + 1
    cap = n_blk * bm
    flat_e = top_e[:, :TOP_K].reshape(nk)
    order = jnp.argsort(flat_e, stable=True).astype(jnp.int32)
    counts = jnp.sum((flat_e[:, None] == jnp.arange(N_EXPERTS, dtype=jnp.int32)[None, :]).astype(jnp.int32), axis=0)
    start = jnp.cumsum(counts) - counts
    padded = ((counts + bm - 1) // bm) * bm
    pend = jnp.cumsum(padded)
    pstart = pend - padded
    n_used = (pend[-1] // bm).astype(jnp.int32)
    blk_idx = jnp.arange(n_blk, dtype=jnp.int32)
    blk = jnp.minimum(blk_idx, n_used - 1)
    blk_e = jnp.sum((pend[None, :] <= (blk * bm)[:, None]).astype(jnp.int32), axis=1)
    blk_e = jnp.clip(blk_e, 0, N_EXPERTS - 1).astype(jnp.int32)
    blk_valid = jnp.clip(counts[blk_e] - (blk_idx * bm - pstart[blk_e]), 0, bm).astype(jnp.int32)
    slot = jnp.arange(cap, dtype=jnp.int32)
    slot_e = jnp.repeat(blk_e, bm)
    rank = slot - pstart[slot_e]
    pair_of_slot = jnp.where(rank < counts[slot_e], order[jnp.clip(start[slot_e] + rank, 0, nk - 1)], 0)
    y = _experts(m_tiles, pair_of_slot.astype(jnp.int32), blk_e, blk_valid, n_used.reshape(1), w1_all, b1, w2_all, b2, layer)
    return _combine(x, y, gates, mod_l, tpb, n_batch)


def _rope_tables(seq_len, d_rot, reps, n_extra):
    t = jnp.arange(seq_len)
    rows = (t // GRID_W).astype(F32)
    cols = (t % GRID_W).astype(F32)
    n_freq = d_rot // 4
    inv = ROPE_THETA ** (-jnp.arange(n_freq, dtype=F32) / n_freq)
    ang = jnp.concatenate([rows[:, None] * inv, cols[:, None] * inv], axis=-1)
    cos = jnp.repeat(jnp.cos(ang), 2, axis=-1)
    sin = jnp.repeat(jnp.sin(ang), 2, axis=-1) * jnp.tile(jnp.asarray([-1.0, 1.0], F32), d_rot // 2)
    cos = jnp.concatenate([jnp.tile(cos, (1, reps)), jnp.ones((n_extra, d_rot * reps), F32)], axis=0)
    sin = jnp.concatenate([jnp.tile(sin, (1, reps)), jnp.zeros((n_extra, d_rot * reps), F32)], axis=0)
    return cos, sin


def _retention_tables(decay_logit):
    log_g = jax.nn.log_sigmoid(decay_logit.astype(F32))
    c = RET_CHUNK
    pos = jnp.arange(c, dtype=F32)
    diff = pos[:, None] - pos[None, :]
    lf = log_g[0][:, None, None]
    lb = log_g[1][:, None, None]
    dm_f = jnp.where(diff >= 0, jnp.exp(lf * jnp.where(diff >= 0, diff, 0.0)), 0.0)
    dm_b = jnp.where(diff < 0, jnp.exp(lb * jnp.where(diff < 0, -diff, 0.0)), 0.0)
    qd_f = jnp.exp(log_g[0][:, None] * (pos + 1.0))
    qd_b = jnp.exp(log_g[1][:, None] * (c - pos))
    kd_f = jnp.exp(log_g[0][:, None] * (c - 1.0 - pos))
    kd_b = jnp.exp(log_g[1][:, None] * pos)
    bc = lambda a: jnp.broadcast_to(a[..., None], a.shape + (LANES,))
    dmask = jnp.stack([dm_f, dm_b])
    qdec = jnp.stack([bc(qd_f), bc(qd_b)])
    kdec = jnp.stack([bc(kd_f), bc(kd_b)])
    cdec = jnp.exp(log_g * c).reshape(-1)
    return dmask, qdec, kdec, cdec


def _heads_major(t, n_batch, length, n_heads):
    return t.reshape(n_batch, length, n_heads, -1).transpose(0, 2, 1, 3)


def _heads_t(t, n_batch, length, n_heads):
    return t.reshape(n_batch, length, n_heads, -1).transpose(0, 2, 3, 1)


def kernel(x, c, ctx, c_ctx, norm1_g, norm2_g, w_mod, b_mod, w_in_even, w_out_even, a_q_norm, a_k_norm, b_q_norm, b_k_norm, b_rpb, w_in_odd, w_out_odd, ret_decay, ret_gn, mla_cq_norm, mla_ckv_norm, w_uq, w_ukv, mla_q_norm, mla_k_norm, w_router, b_router, w_exp1, b_exp1, w_exp2, b_exp2):
    bsz, s, d = x.shape
    cl = ctx.shape[1]
    depth = w_mod.shape[0]
    tm = ROW_TILE
    assert s % tm == 0 and (bsz * cl) == tm and s % NA_QB == 0 and bsz + 1 <= 8 and d == TOK_ROWS * LANES
    tpb = s // tm
    nl = bsz * s
    n_all = nl + bsz * cl

    c_rows = jnp.zeros((8, d), F32).at[:bsz].set(c).at[bsz].set(c_ctx)
    mod = _mod_vectors(c_rows, w_mod, b_mod)

    cos_a, sin_a = _rope_tables(s, HEAD_DIM, LANES // HEAD_DIM, tm)
    cos_c, sin_c = _rope_tables(s, RET_DK, 1, tm)
    cos_d, sin_d = _rope_tables(s, MLA_ROPE, LANES // MLA_ROPE, tm)
    na_bases, na_var, na_row_sel, na_col_in, na_dc_onehot = _na_tables(s)

    x_all = jnp.concatenate([x.reshape(nl, d), ctx.reshape(bsz * cl, d)], axis=0)
    lat = lambda t: t[:nl]
    cx = lambda t: t[nl:]
    att_scale = HEAD_DIM ** -0.5 * LOG2E
    grp = GQA_HEADS // GQA_KV_HEADS
    bkv = bsz * GQA_KV_HEADS

    for l in range(depth):
        need_ctx = l < depth - 1
        i = l // 2
        mod_l = mod[l].reshape(8, 1, 6 * d)
        if l % 2 == 0:
            wq, wk, wv, wn = GQA_HEADS * HEAD_DIM, GQA_KV_HEADS * HEAD_DIM, GQA_KV_HEADS * HEAD_DIM, NA_HEADS * HEAD_DIM
            starts = np.cumsum([0, wq, wk, wv, wn, wn])
            segs = [(int(starts[0]), wq, "norm_rope", 1.0), (int(starts[1]), wk, "norm_rope", 1.0),
                    (int(starts[2]), wv, "plain", 1.0), (int(starts[3]), wn, "norm", 1.0),
                    (int(starts[4]), wn, "norm", 1.0), (int(starts[5]), wn, "plain", 1.0)]
            gain = jnp.concatenate([jnp.tile(a_q_norm[i], GQA_HEADS) * att_scale, jnp.tile(a_k_norm[i], GQA_KV_HEADS),
                                    jnp.ones((wv,), F32), jnp.tile(b_q_norm[i], NA_HEADS) * att_scale,
                                    jnp.tile(b_k_norm[i], NA_HEADS), jnp.ones((wn,), F32)]).reshape(1, -1)
            qa, ka, va, qb, kb, vb = _proj(x_all, mod_l, norm1_g[l], w_in_even[i].astype(BF16), gain, cos_a, sin_a,
                                           segs, [BF16] * 6, tpb, bsz)
            k_lat = _heads_major(lat(ka), bsz, s, GQA_KV_HEADS)
            k_cx = _heads_major(cx(ka), bsz, cl, GQA_KV_HEADS)
            v_lat = _heads_major(lat(va), bsz, s, GQA_KV_HEADS)
            v_cx = _heads_major(cx(va), bsz, cl, GQA_KV_HEADS)
            k_all = jnp.concatenate([k_lat, k_cx], axis=2).reshape(bkv, s + cl, HEAD_DIM)
            v_all = jnp.concatenate([v_lat, v_cx], axis=2).reshape(bkv, s + cl, HEAD_DIM)
            qt = _heads_t(lat(qa), bsz, s, GQA_HEADS).reshape(bkv, grp, HEAD_DIM, s)
            oa_t = _flash(qt, k_all, v_all, Q_SUB)
            oa = oa_t.reshape(bsz, GQA_HEADS, HEAD_DIM, s).transpose(0, 3, 1, 2).reshape(nl, wq)
            bias = _na_bias(b_rpb[i], na_row_sel, na_col_in, na_dc_onehot)
            bh = bsz * NA_HEADS
            qn = _heads_major(lat(qb), bsz, s, NA_HEADS).reshape(bh, s, HEAD_DIM)
            kn = _heads_major(lat(kb), bsz, s, NA_HEADS).reshape(bh, s, HEAD_DIM)
            vn = _heads_major(lat(vb), bsz, s, NA_HEADS).reshape(bh, s, HEAD_DIM)
            kn_c = _heads_major(cx(kb), bsz, cl, NA_HEADS).reshape(bh, cl, HEAD_DIM)
            vn_c = _heads_major(cx(vb), bsz, cl, NA_HEADS).reshape(bh, cl, HEAD_DIM)
            ob = _na_attention(qn, kn, vn, kn_c, vn_c, bias, jnp.asarray(na_bases), jnp.asarray(na_var), NA_HEADS)
            ob = ob.reshape(bsz, NA_HEADS, s, HEAD_DIM).transpose(0, 2, 1, 3).reshape(nl, wn)
            if need_ctx:
                qt_c = _heads_t(cx(qa), bsz, cl, GQA_HEADS).reshape(bkv, grp, HEAD_DIM, cl)
                oa_c = _flash(qt_c, k_cx.reshape(bkv, cl, HEAD_DIM), v_cx.reshape(bkv, cl, HEAD_DIM), Q_SUB)
                oa_c = oa_c.reshape(bsz, GQA_HEADS, HEAD_DIM, cl).transpose(0, 3, 1, 2).reshape(bsz * cl, wq)
                qnt_c = _heads_t(cx(qb), bsz, cl, NA_HEADS).reshape(bh, 1, HEAD_DIM, cl)
                ob_c = _flash(qnt_c, kn_c, vn_c, Q_SUB)
                ob_c = ob_c.reshape(bsz, NA_HEADS, HEAD_DIM, cl).transpose(0, 3, 1, 2).reshape(bsz * cl, wn)
                a1 = jnp.concatenate([oa, oa_c], axis=0)
                a2 = jnp.concatenate([ob, ob_c], axis=0)
            else:
                a1, a2 = oa, ob
            w_out = w_out_even[i].astype(BF16)
        else:
            rw = RET_HEADS * RET_DK
            kr_cols = w_in_odd[i][:, 4 * rw + MLA_Q_LORA + MLA_KV_LORA:]
            w_ext = jnp.concatenate([w_in_odd[i]] + [kr_cols] * (LANES * 2 // MLA_ROPE - 1), axis=1).astype(BF16)
            mla_w = MLA_Q_LORA + MLA_KV_LORA + 2 * LANES
            segs = [(0, rw, "rope", RET_DK ** -0.5), (rw, rw, "rope", 1.0), (2 * rw, rw, "plain", 1.0),
                    (3 * rw, rw, "plain", 1.0), (4 * rw, mla_w, "plain", 1.0)]
            gain = jnp.ones((1, w_ext.shape[1]), F32)
            rq, rk, rv, rg, mla_in = _proj(x_all, mod_l, norm1_g[l], w_ext, gain, cos_c, sin_c, segs,
                                           [BF16, BF16, BF16, F32, F32], tpb, bsz)
            dmask, qdec, kdec, cdec = _retention_tables(ret_decay[i])
            yf, yb = _retention(rq, rk, rv, dmask, qdec, kdec, cdec, bsz, s, cl)
            a1 = _ret_finish(yf, yb, rg, ret_gn[i].reshape(1, rw))
            dqk = MLA_NOPE + MLA_ROPE
            perm_q = np.concatenate([np.arange(h * dqk, h * dqk + MLA_NOPE) for h in range(MLA_HEADS)]
                                    + [np.arange(h * dqk + MLA_NOPE, (h + 1) * dqk) for h in range(MLA_HEADS)])
            dkv = MLA_NOPE + MLA_V
            perm_kv = np.concatenate([np.arange(h * dkv, h * dkv + MLA_NOPE) for h in range(MLA_HEADS)]
                                     + [np.arange(h * dkv + MLA_NOPE, (h + 1) * dkv) for h in range(MLA_HEADS)])
            mla_scale = dqk ** -0.5 * LOG2E
            gq = jnp.concatenate([jnp.tile(mla_q_norm[i][:MLA_NOPE], MLA_HEADS),
                                  jnp.tile(mla_q_norm[i][MLA_NOPE:], MLA_HEADS)]).reshape(1, -1) * mla_scale
            gkn = mla_k_norm[i][:MLA_NOPE].reshape(1, -1)
            gkr = jnp.tile(mla_k_norm[i][MLA_NOPE:], LANES // MLA_ROPE).reshape(1, -1)
            q_m, k_m, v_m = _mla_proj(mla_in, w_uq[i][:, perm_q].astype(BF16), w_ukv[i][:, perm_kv].astype(BF16),
                                      mla_cq_norm[i].reshape(1, -1), mla_ckv_norm[i].reshape(1, -1), gq, gkn, gkr,
                                      cos_d, sin_d, tpb, bsz)
            nw = MLA_HEADS * MLA_NOPE

            def qk_heads(t, length, transposed):
                nope = t[:, :nw].reshape(bsz, length, MLA_HEADS, MLA_NOPE)
                rope = t[:, nw:].reshape(bsz, length, MLA_HEADS, MLA_ROPE)
                full = jnp.concatenate([nope, rope], axis=-1)
                return full.transpose(0, 2, 3, 1) if transposed else full.transpose(0, 2, 1, 3)

            bhm = bsz * MLA_HEADS
            k_lat = qk_heads(lat(k_m), s, False)
            k_cx = qk_heads(cx(k_m), cl, False)
            v_lat = _heads_major(lat(v_m), bsz, s, MLA_HEADS)
            v_cx = _heads_major(cx(v_m), bsz, cl, MLA_HEADS)
            k_all = jnp.concatenate([k_lat, k_cx], axis=2).reshape(bhm, s + cl, dqk)
            v_all = jnp.concatenate([v_lat, v_cx], axis=2).reshape(bhm, s + cl, MLA_V)
            qt = qk_heads(lat(q_m), s, True).reshape(bhm, 1, dqk, s)
            om = _flash(qt, k_all, v_all, 4 * Q_SUB)
            om = om.reshape(bsz, MLA_HEADS, MLA_V, s).transpose(0, 3, 1, 2).reshape(nl, MLA_HEADS * MLA_V)
            if need_ctx:
                qt_c = qk_heads(cx(q_m), cl, True).reshape(bhm, 1, dqk, cl)
                om_c = _flash(qt_c, k_cx.reshape(bhm, cl, dqk), v_cx.reshape(bhm, cl, MLA_V), Q_SUB)
                om_c = om_c.reshape(bsz, MLA_HEADS, MLA_V, cl).transpose(0, 3, 1, 2).reshape(bsz * cl, MLA_HEADS * MLA_V)
                a2 = jnp.concatenate([om, om_c], axis=0)
            else:
                a1 = a1[:nl]
                a2 = om
            w_out = w_out_odd[i].astype(BF16)

        n_rows = n_all if need_ctx else nl
        w_r_hi = w_router[l].astype(BF16)
        w_r_lo = (w_router[l] - w_r_hi.astype(F32)).astype(BF16)
        w_r = (jnp.zeros((d, 2 * LANES), BF16).at[:, :N_EXPERTS].set(w_r_hi)
               .at[:, LANES:LANES + N_EXPERTS].set(w_r_lo))
        b_r = jnp.full((1, LANES), NEG, F32).at[0, :N_EXPERTS].set(b_router[l])
        x_new, m, top_e, gates = _out_proj(a1, a2, x_all, w_out, mod_l, norm2_g[l], w_r, b_r, n_rows, tpb, bsz)
        x_all = _moe(m, top_e, gates, x_new, mod_l, w_exp1, b_exp1[l], w_exp2, b_exp2[l], l, tpb, bsz)
    return x_all[:nl].reshape(bsz, s, d)
```

```python
import functools
import math

import numpy as np
import jax
import jax.numpy as jnp
from jax import lax
from jax.experimental import pallas as pl
from jax.experimental.pallas import tpu as pltpu

F32 = jnp.float32
BF16 = jnp.bfloat16

GRID_W = 64
HEAD_DIM = 64
GQA_HEADS = 8
GQA_KV_HEADS = 2
NA_HEADS = 8
NA_ROWS = 8
NA_COLS = 16
RET_HEADS = 4
RET_DK = 128
MLA_HEADS = 4
MLA_Q_LORA = 256
MLA_KV_LORA = 128
MLA_NOPE = 128
MLA_ROPE = 64
MLA_V = 128
N_EXPERTS = 32
TOP_K = 4
SWIGLU_LIMIT = 7.0
SWIGLU_ALPHA = 1.702
ROPE_THETA = 10000.0
EPS = 1e-6
GN_EPS = 1e-5
LOG2E = math.log2(math.e)
NEG = -1e30

LANES = 128
ROW_TILE = 512
Q_SUB = 256
ONES_ROWS = 16
NA_QB = 256
NA_WIN_ROWS = 12
RET_CHUNK = 256
MOE_BM = 512
TOK_ROWS = 8
VMEM_LIMIT = 56 * 1024 * 1024


def _cp(sem):
    return pltpu.CompilerParams(dimension_semantics=sem, vmem_limit_bytes=VMEM_LIMIT)


def _mod_kernel(c_ref, w_ref, b_ref, o_ref):
    c = c_ref[...]
    s = c * jax.nn.sigmoid(c)
    o_ref[0] = jnp.dot(s, w_ref[0], precision=lax.Precision.HIGHEST, preferred_element_type=F32) + b_ref[0]


def _mod_vectors(c_rows, w_mod, b_mod):
    depth, d, d6 = w_mod.shape
    tn = 1536
    return pl.pallas_call(
        _mod_kernel,
        grid=(depth, d6 // tn),
        in_specs=[pl.BlockSpec((8, d), lambda l, j: (0, 0)),
                  pl.BlockSpec((1, d, tn), lambda l, j: (l, 0, j)),
                  pl.BlockSpec((1, 1, tn), lambda l, j: (l, 0, j))],
        out_specs=pl.BlockSpec((1, 8, tn), lambda l, j: (l, 0, j)),
        out_shape=jax.ShapeDtypeStruct((depth, 8, d6), F32),
        compiler_params=_cp(("parallel", "parallel")),
    )(c_rows, w_mod, b_mod.reshape(depth, 1, d6))


def _modulated_norm(x, g, sc, sh):
    ms = jnp.mean(x * x, axis=-1, keepdims=True)
    return x * lax.rsqrt(ms + EPS) * g * (1.0 + sc) + sh


def _pair_rope(y, cos, sin_signed):
    lane = lax.broadcasted_iota(jnp.int32, y.shape, 1)
    partner = jnp.where((lane & 1) == 0, pltpu.roll(y, LANES - 1, 1), pltpu.roll(y, 1, 1))
    return y * cos + partner * sin_signed


def _store_token_tiles(ref, val):
    n = val.shape[0]
    for j in range(TOK_ROWS):
        ref[pl.ds(j, n, stride=TOK_ROWS), :] = val[:, j * LANES:(j + 1) * LANES]


def _load_token_tiles(ref, n):
    return jnp.concatenate([ref[pl.ds(j, n, stride=TOK_ROWS), :] for j in range(TOK_ROWS)], axis=-1)


def _split_dot(a_f32, w_bf16):
    hi = a_f32.astype(BF16)
    lo = (a_f32 - hi.astype(F32)).astype(BF16)
    return (jnp.dot(hi, w_bf16, preferred_element_type=F32) + jnp.dot(lo, w_bf16, preferred_element_type=F32))


def _proj_kernel(x_ref, sh_ref, sc_ref, g_ref, w_ref, gain_ref, bd_ref, cos_ref, sin_ref, *out_refs, segs):
    a = _modulated_norm(x_ref[...], g_ref[...], sc_ref[0], sh_ref[0]).astype(BF16)
    cos = cos_ref[...]
    sin = sin_ref[...]
    for (start, width, mode, scale), o_ref in zip(segs, out_refs):
        y_seg = jnp.dot(a, w_ref[:, start:start + width], preferred_element_type=F32)
        if mode == "plain":
            o_ref[...] = y_seg.astype(o_ref.dtype)
            continue
        for j in range(width // LANES):
            y = y_seg[:, j * LANES:(j + 1) * LANES]
            if "norm" in mode:
                ms = _split_dot(y * y, bd_ref[...])
                y = y * lax.rsqrt(ms + EPS) * gain_ref[:, start + j * LANES:start + (j + 1) * LANES]
            if scale != 1.0:
                y = y * scale
            if "rope" in mode:
                y = _pair_rope(y, cos, sin)
            o_ref[:, j * LANES:(j + 1) * LANES] = y.astype(o_ref.dtype)


def _proj(x_all, mod_l, norm_g, w, gain, cos_t, sin_t, segs, out_dtypes, n_lat_tiles_per_batch, n_batch):
    n, d = x_all.shape
    tm = ROW_TILE
    n_tiles = n // tm
    wtot = w.shape[1]
    tpb = n_lat_tiles_per_batch
    n_lat_tiles = tpb * n_batch

    def mod_row(t):
        return jnp.where(t < n_lat_tiles, t // tpb, n_batch)

    def rope_row(t):
        return jnp.where(t < n_lat_tiles, t % tpb, tpb)

    bd = np.kron(np.eye(2, dtype=np.float32), np.full((HEAD_DIM, HEAD_DIM), 1.0 / HEAD_DIM, np.float32))
    in_specs = [
        pl.BlockSpec((tm, d), lambda t: (t, 0)),
        pl.BlockSpec((1, 1, d), lambda t: (mod_row(t), 0, 0)),
        pl.BlockSpec((1, 1, d), lambda t: (mod_row(t), 0, 1)),
        pl.BlockSpec((1, d), lambda t: (0, 0)),
        pl.BlockSpec((d, wtot), lambda t: (0, 0)),
        pl.BlockSpec((1, wtot), lambda t: (0, 0)),
        pl.BlockSpec((LANES, LANES), lambda t: (0, 0)),
        pl.BlockSpec((tm, LANES), lambda t: (rope_row(t), 0)),
        pl.BlockSpec((tm, LANES), lambda t: (rope_row(t), 0)),
    ]
    out_specs = [pl.BlockSpec((tm, s[1]), lambda t: (t, 0)) for s in segs]
    out_shape = [jax.ShapeDtypeStruct((n, s[1]), dt) for s, dt in zip(segs, out_dtypes)]
    return pl.pallas_call(
        functools.partial(_proj_kernel, segs=tuple(segs)),
        grid=(n_tiles,),
        in_specs=in_specs,
        out_specs=out_specs,
        out_shape=out_shape,
        compiler_params=_cp(("parallel",)),
    )(x_all, mod_l, mod_l, norm_g.reshape(1, d), w, gain, jnp.asarray(bd, BF16), cos_t, sin_t)


def _mla_proj_kernel(x_ref, wuq_ref, wukv_ref, gcq_ref, gckv_ref, gq_ref, gkn_ref, gkr_ref, cos_ref, sin_ref,
                     q_ref, k_ref, v_ref):
    x = x_ref[...]
    cq = x[:, :MLA_Q_LORA]
    ckv = x[:, MLA_Q_LORA:MLA_Q_LORA + MLA_KV_LORA]
    kr = x[:, MLA_Q_LORA + MLA_KV_LORA:MLA_Q_LORA + MLA_KV_LORA + LANES]
    cos = cos_ref[...]
    sin = sin_ref[...]
    cqn = cq * lax.rsqrt(jnp.mean(cq * cq, axis=-1, keepdims=True) + EPS) * gcq_ref[...]
    ckvn = ckv * lax.rsqrt(jnp.mean(ckv * ckv, axis=-1, keepdims=True) + EPS) * gckv_ref[...]
    q = jnp.dot(cqn.astype(BF16), wuq_ref[...], preferred_element_type=F32)
    kv = jnp.dot(ckvn.astype(BF16), wukv_ref[...], preferred_element_type=F32)
    nh = MLA_HEADS
    d_qk = float(MLA_NOPE + MLA_ROPE)
    lane = lax.broadcasted_iota(jnp.int32, (1, LANES), 1)
    low = lane < MLA_ROPE

    def half_sums(slab):
        sq = slab * slab
        a = jnp.sum(jnp.where(low, sq, 0.0), axis=-1, keepdims=True)
        return a, jnp.sum(sq, axis=-1, keepdims=True) - a

    rope_w = nh * MLA_NOPE
    q_rope_ss = []
    for r in range(nh // 2):
        q_rope_ss.extend(half_sums(q[:, rope_w + r * LANES:rope_w + (r + 1) * LANES]))
    kr_ss, _ = half_sums(kr)
    rs_q, rs_k = [], []
    for h in range(nh):
        qn = q[:, h * MLA_NOPE:(h + 1) * MLA_NOPE]
        kn = kv[:, h * MLA_NOPE:(h + 1) * MLA_NOPE]
        rs_q.append(lax.rsqrt((jnp.sum(qn * qn, axis=-1, keepdims=True) + q_rope_ss[h]) / d_qk + EPS))
        rs_k.append(lax.rsqrt((jnp.sum(kn * kn, axis=-1, keepdims=True) + kr_ss) / d_qk + EPS))
        q_ref[:, h * MLA_NOPE:(h + 1) * MLA_NOPE] = (qn * rs_q[h] * gq_ref[:, h * MLA_NOPE:(h + 1) * MLA_NOPE]).astype(q_ref.dtype)
        k_ref[:, h * MLA_NOPE:(h + 1) * MLA_NOPE] = (kn * rs_k[h] * gkn_ref[...]).astype(k_ref.dtype)
    kr_rot = _pair_rope(kr * gkr_ref[...], cos, sin)
    for r in range(nh // 2):
        sl = slice(rope_w + r * LANES, rope_w + (r + 1) * LANES)
        yq = q[:, sl] * jnp.where(low, rs_q[2 * r], rs_q[2 * r + 1]) * gq_ref[:, sl]
        q_ref[:, sl] = _pair_rope(yq, cos, sin).astype(q_ref.dtype)
        k_ref[:, sl] = (kr_rot * jnp.where(low, rs_k[2 * r], rs_k[2 * r + 1])).astype(k_ref.dtype)
    v_ref[...] = kv[:, nh * MLA_NOPE:].astype(v_ref.dtype)


def _mla_proj(mla_in, wuq, wukv, gcq, gckv, gq, gkn, gkr, cos_t, sin_t, tpb, n_batch):
    n, win = mla_in.shape
    tm = ROW_TILE
    n_lat_tiles = tpb * n_batch

    def rope_row(t):
        return jnp.where(t < n_lat_tiles, t % tpb, tpb)

    qk_w = MLA_HEADS * (MLA_NOPE + MLA_ROPE)
    v_w = MLA_HEADS * MLA_V
    full = lambda a: pl.BlockSpec(a.shape, lambda t: (0,) * a.ndim)
    return pl.pallas_call(
        _mla_proj_kernel,
        grid=(n // tm,),
        in_specs=[pl.BlockSpec((tm, win), lambda t: (t, 0)), full(wuq), full(wukv), full(gcq), full(gckv),
                  full(gq), full(gkn), full(gkr),
                  pl.BlockSpec((tm, LANES), lambda t: (rope_row(t), 0)),
                  pl.BlockSpec((tm, LANES), lambda t: (rope_row(t), 0))],
        out_specs=[pl.BlockSpec((tm, qk_w), lambda t: (t, 0)), pl.BlockSpec((tm, qk_w), lambda t: (t, 0)),
                   pl.BlockSpec((tm, v_w), lambda t: (t, 0))],
        out_shape=[jax.ShapeDtypeStruct((n, qk_w), BF16), jax.ShapeDtypeStruct((n, qk_w), BF16),
                   jax.ShapeDtypeStruct((n, v_w), BF16)],
        compiler_params=_cp(("parallel",)),
    )(mla_in, wuq, wukv, gcq, gckv, gq, gkn, gkr, cos_t, sin_t)


def _flash_kernel(qt_ref, k_ref, vt_ref, ot_ref, q_sc, s_0, s_1, s_2, mx_0, mx_1, mx_2, m_sc, acc_sc, *,
                  g, n_chunks, dv):
    bufs = ((s_0, mx_0), (s_1, mx_1), (s_2, mx_2))
    tq = qt_ref.shape[3]
    for gi in range(g):
        q_sc[:, gi * tq:(gi + 1) * tq] = qt_ref[0, gi]
    m_sc[...] = jnp.full(m_sc.shape, NEG, F32)
    acc_sc[...] = jnp.zeros(acc_sc.shape, F32)

    def scores(ci, s_ref, mx_ref):
        s = jnp.dot(k_ref[0, ci], q_sc[...], preferred_element_type=F32)
        s_ref[...] = s
        mx_ref[...] = jnp.max(s, axis=0, keepdims=True)

    def accumulate(ci, s_ref, mx_ref):
        m_old = m_sc[...]
        m_new = jnp.maximum(m_old, mx_ref[...])
        alpha = jnp.exp2(m_old - m_new)
        p = jnp.exp2(s_ref[...] - m_new).astype(BF16)
        acc_sc[...] = alpha * acc_sc[...] + jnp.dot(vt_ref[0, ci], p, preferred_element_type=F32)
        m_sc[...] = m_new

    def fused(ci_n, s_n, mx_n, ci_c, s_c, mx_c):
        m_old = m_sc[...]
        m_new = jnp.maximum(m_old, mx_c[...])
        alpha = jnp.exp2(m_old - m_new)
        pv = None
        mx = None
        for k0 in range(0, s_c.shape[0], 256):
            s = jnp.dot(k_ref[0, ci_n, k0:k0 + 256, :], q_sc[...], preferred_element_type=F32)
            s_n[k0:k0 + 256, :] = s
            mxj = jnp.max(s, axis=0, keepdims=True)
            mx = mxj if mx is None else jnp.maximum(mx, mxj)
            p = jnp.exp2(s_c[k0:k0 + 256, :] - m_new).astype(BF16)
            part = jnp.dot(vt_ref[0, ci_c, :, k0:k0 + 256], p, preferred_element_type=F32)
            pv = part if pv is None else pv + part
        mx_n[...] = mx
        acc_sc[...] = alpha * acc_sc[...] + pv
        m_sc[...] = m_new

    scores(0, *bufs[0])
    if n_chunks == 1:
        accumulate(0, *bufs[0])
    else:
        scores(1, *bufs[1])
        n_fused = n_chunks - 2

        def triple(j, carry):
            c = 3 * j
            for r in range(3):
                fused(c + r + 2, *bufs[(r + 2) % 3], c + r, *bufs[r])
            return carry

        lax.fori_loop(0, n_fused // 3, triple, 0)
        for c in range(3 * (n_fused // 3), n_fused):
            fused(c + 2, *bufs[(c + 2) % 3], c, *bufs[c % 3])
        accumulate(n_chunks - 2, *bufs[(n_chunks - 2) % 3])
        accumulate(n_chunks - 1, *bufs[(n_chunks - 1) % 3])
    acc = acc_sc[...]
    o = acc[:dv] / acc[dv:dv + 1]
    for gi in range(g):
        ot_ref[0, gi] = o[:, gi * tq:(gi + 1) * tq].astype(ot_ref.dtype)


def _key_chunk(lk):
    for tk in (1280, 1024, 768, 512, 256):
        if lk % tk == 0:
            return tk
    raise ValueError(f"key length {lk} must be a multiple of 256")


def _flash(qt, k, v, tq_blk):
    bk, g, dq, lq = qt.shape
    lk, dv = v.shape[1], v.shape[2]
    tk = _key_chunk(lk)
    nch = lk // tk
    kc = k.reshape(bk, nch, tk, dq)
    dve = dv + ONES_ROWS
    vt = jnp.concatenate([v, jnp.ones((bk, lk, ONES_ROWS), v.dtype)], axis=-1)
    vt = vt.reshape(bk, nch, tk, dve).transpose(0, 1, 3, 2)
    tq_blk = min(tq_blk, lq)
    assert lq % tq_blk == 0 and tq_blk % LANES == 0
    w = g * tq_blk
    return pl.pallas_call(
        functools.partial(_flash_kernel, g=g, n_chunks=nch, dv=dv),
        grid=(bk, lq // tq_blk),
        in_specs=[pl.BlockSpec((1, g, dq, tq_blk), lambda b, i: (b, 0, 0, i)),
                  pl.BlockSpec((1, nch, tk, dq), lambda b, i: (b, 0, 0, 0)),
                  pl.BlockSpec((1, nch, dve, tk), lambda b, i: (b, 0, 0, 0))],
        out_specs=pl.BlockSpec((1, g, dv, tq_blk), lambda b, i: (b, 0, 0, i)),
        out_shape=jax.ShapeDtypeStruct((bk, g, dv, lq), BF16),
        scratch_shapes=[pltpu.VMEM((dq, w), BF16)] + [pltpu.VMEM((tk, w), F32)] * 3 + [pltpu.VMEM((1, w), F32)] * 4
                       + [pltpu.VMEM((dve, w), F32)],
        compiler_params=_cp(("parallel", "parallel")),
    )(qt, kc, vt)


def _na_tables(seq_len):
    rows_n = seq_len // GRID_W
    assert rows_n >= NA_WIN_ROWS and NA_ROWS <= rows_n
    nb = seq_len // NA_QB
    rpq = NA_QB // GRID_W
    band = NA_WIN_ROWS * GRID_W
    variants, var_id, bases = {}, [], []
    for j in range(nb):
        base = int(np.clip(rpq * j - NA_ROWS // 2, 0, rows_n - NA_WIN_ROWS))
        bases.append(base)
        t = np.arange(NA_QB) + j * NA_QB
        r, col = t // GRID_W, t % GRID_W
        r0 = np.clip(r - NA_ROWS // 2, 0, rows_n - NA_ROWS)
        c0 = np.clip(col - NA_COLS // 2, 0, GRID_W - NA_COLS)
        kk = np.arange(band)
        kr = base + kk // GRID_W
        kc = kk % GRID_W
        inside = ((kr[None] >= r0[:, None]) & (kr[None] < r0[:, None] + NA_ROWS)
                  & (kc[None] >= c0[:, None]) & (kc[None] < c0[:, None] + NA_COLS))
        rel = (kr[None] - r[:, None] + NA_ROWS - 1) * (2 * NA_COLS - 1) + (kc[None] - col[:, None] + NA_COLS - 1)
        tab = np.where(inside, rel, -1).astype(np.int32)
        assert (inside.sum(axis=1) == NA_ROWS * NA_COLS).all()
        key = tab.tobytes()
        if key not in variants:
            variants[key] = (len(variants), tab)
        var_id.append(variants[key][0])
    tabs = np.stack([v[1] for v in sorted(variants.values(), key=lambda kv: kv[0])])
    n_dr = 2 * NA_ROWS - 1
    n_dc = 2 * NA_COLS - 1
    col = np.arange(GRID_W)
    c0 = np.clip(col - NA_COLS // 2, 0, GRID_W - NA_COLS)
    col_in = (col[None] >= c0[:, None]) & (col[None] < c0[:, None] + NA_COLS)
    dc = col[None] - col[:, None] + NA_COLS - 1
    t5 = tabs.reshape(len(tabs), rpq, GRID_W, NA_WIN_ROWS, GRID_W)
    row_sel = np.full((len(tabs), rpq, NA_WIN_ROWS), n_dr, np.int32)
    for v in range(len(tabs)):
        for a in range(rpq):
            for i in range(NA_WIN_ROWS):
                blk = t5[v, a, :, i, :]
                if (blk >= 0).any():
                    dr = int(blk[blk >= 0][0]) // n_dc
                    assert (np.where(col_in, dr * n_dc + dc, -1) == blk).all()
                    row_sel[v, a, i] = dr
                else:
                    assert (blk < 0).all()
    dc_onehot = (dc[None] == np.arange(n_dc)[:, None, None]).astype(np.float32)
    return np.asarray(bases, np.int32), np.asarray(var_id, np.int32), row_sel, col_in, dc_onehot


def _na_bias(rpb, row_sel, col_in, dc_onehot):
    h = rpb.shape[0]
    n_dr, n_dc = 2 * NA_ROWS - 1, 2 * NA_COLS - 1
    t = jnp.einsum("hdj,jck->hdck", rpb.astype(F32).reshape(h, n_dr, n_dc) * LOG2E, jnp.asarray(dc_onehot),
                   precision=lax.Precision.HIGHEST)
    t = jnp.where(jnp.asarray(col_in)[None, None], t, NEG)
    t = jnp.concatenate([t, jnp.full((h, 1, GRID_W, GRID_W), NEG, F32)], axis=1)
    nv, rpq, nw = row_sel.shape
    b = t[:, jnp.asarray(row_sel)]
    return b.transpose(1, 0, 2, 4, 3, 5).reshape(nv, h, rpq * GRID_W, nw * GRID_W)


def _na_kernel(base_ref, var_ref, q_ref, k_ref, v_ref, kc_ref, vc_ref, bias_ref, o_ref):
    j = pl.program_id(2)
    band = NA_WIN_ROWS * GRID_W
    start = pl.multiple_of(base_ref[j] * GRID_W, GRID_W)
    q = q_ref[...]
    kw = k_ref[pl.ds(start, band), :]
    vw = v_ref[pl.ds(start, band), :]
    kc = kc_ref[...]
    vc = vc_ref[...]
    nt = (((1,), (1,)), ((), ()))
    lane = lax.broadcasted_iota(jnp.int32, q.shape, 1)
    outs = []
    for hh in range(LANES // HEAD_DIM):
        mine = (lane >= hh * HEAD_DIM) & (lane < (hh + 1) * HEAD_DIM)
        qh = jnp.where(mine, q, jnp.zeros_like(q))
        s_win = lax.dot_general(qh, kw, nt, preferred_element_type=F32) + bias_ref[0, hh]
        s_ctx = lax.dot_general(qh, kc, nt, preferred_element_type=F32)
        m = jnp.maximum(jnp.max(s_win, axis=-1, keepdims=True), jnp.max(s_ctx, axis=-1, keepdims=True))
        p_win = jnp.exp2(s_win - m)
        p_ctx = jnp.exp2(s_ctx - m)
        l = jnp.sum(p_win, axis=-1, keepdims=True) + jnp.sum(p_ctx, axis=-1, keepdims=True)
        o = (jnp.dot(p_win.astype(BF16), vw, preferred_element_type=F32)
             + jnp.dot(p_ctx.astype(BF16), vc, preferred_element_type=F32))
        outs.append(o / l)
    o_ref[...] = jnp.where(lane < HEAD_DIM, outs[0], outs[1]).astype(o_ref.dtype)


def _na_attention(q, k, v, bias, bases, var_id, n_batch, seq_len, ctx_len):
    w = q.shape[1]
    assert LANES // HEAD_DIM == 2 and ctx_len % 8 == 0 and (n_batch * seq_len) % ctx_len == 0
    band = NA_WIN_ROWS * GRID_W
    nb = seq_len // NA_QB
    ctx0 = (n_batch * seq_len) // ctx_len
    lat_spec = pl.BlockSpec((seq_len, LANES), lambda b, p, j, bs, vr: (b, p))
    ctx_spec = pl.BlockSpec((ctx_len, LANES), lambda b, p, j, bs, vr: (ctx0 + b, p))
    grid_spec = pltpu.PrefetchScalarGridSpec(
        num_scalar_prefetch=2,
        grid=(n_batch, w // LANES, nb),
        in_specs=[pl.BlockSpec((NA_QB, LANES), lambda b, p, j, bs, vr: (b * nb + j, p)),
                  lat_spec, lat_spec, ctx_spec, ctx_spec,
                  pl.BlockSpec((1, LANES // HEAD_DIM, NA_QB, band), lambda b, p, j, bs, vr: (vr[j], p, 0, 0))],
        out_specs=pl.BlockSpec((NA_QB, LANES), lambda b, p, j, bs, vr: (b * nb + j, p)),
    )
    return pl.pallas_call(
        _na_kernel,
        grid_spec=grid_spec,
        out_shape=jax.ShapeDtypeStruct((n_batch * seq_len, w), BF16),
        compiler_params=_cp(("parallel", "parallel", "arbitrary")),
    )(bases, var_id, q, k, v, k, v, bias)


def _ret_kernel(cdec_ref, qf_ref, kf_ref, vf_ref, qb_ref, kb_ref, vb_ref, dmask_ref, qdec_ref, kdec_ref,
                yf_ref, yb_ref, state_sc):
    @pl.when(pl.program_id(1) == 0)
    def _():
        state_sc[...] = jnp.zeros(state_sc.shape, F32)

    nt = (((1,), (1,)), ((), ()))
    tn = (((0,), (0,)), ((), ()))
    dk = RET_DK
    for d, (q_ref, k_ref, v_ref, y_ref) in enumerate(((qf_ref, kf_ref, vf_ref, yf_ref),
                                                      (qb_ref, kb_ref, vb_ref, yb_ref))):
        for h in range(RET_HEADS):
            sl = slice(h * dk, (h + 1) * dk)
            q = q_ref[:, sl]
            k = k_ref[:, sl]
            v = v_ref[:, sl]
            st = state_sc[d, h]
            a = lax.dot_general(q, k, nt, preferred_element_type=F32) * dmask_ref[d, h]
            inner = jnp.dot(a.astype(BF16), v, preferred_element_type=F32)
            cross = jnp.dot(q, st.astype(BF16), preferred_element_type=F32) * qdec_ref[d, h]
            y_ref[:, sl] = inner + cross
            vs = (v.astype(F32) * kdec_ref[d, h]).astype(BF16)
            state_sc[d, h] = st * cdec_ref[d * RET_HEADS + h] + lax.dot_general(k, vs, tn, preferred_element_type=F32)


def _retention(rq, rk, rv, dmask, qdec, kdec, cdec, n_batch, seq_len, ctx_len):
    n, w = rq.shape
    c = RET_CHUNK
    assert ctx_len == c and seq_len % c == 0
    ncl = seq_len // c
    ctx_blk0 = (n_batch * seq_len) // c

    def fwd(b, s, cd):
        return (jnp.where(s == 0, ctx_blk0 + b, b * ncl + s - 1), 0)

    def bwd(b, s, cd):
        return (jnp.where(s == 0, ctx_blk0 + b, b * ncl + ncl - s), 0)

    full = lambda a: pl.BlockSpec(a.shape, lambda b, s, cd: (0,) * a.ndim)
    grid_spec = pltpu.PrefetchScalarGridSpec(
        num_scalar_prefetch=1,
        grid=(n_batch, ncl + 1),
        in_specs=[pl.BlockSpec((c, w), fwd)] * 3 + [pl.BlockSpec((c, w), bwd)] * 3 + [full(dmask), full(qdec), full(kdec)],
        out_specs=[pl.BlockSpec((c, w), fwd), pl.BlockSpec((c, w), bwd)],
        scratch_shapes=[pltpu.VMEM((2, RET_HEADS, RET_DK, RET_DK), F32)],
    )
    return pl.pallas_call(
        _ret_kernel,
        grid_spec=grid_spec,
        out_shape=[jax.ShapeDtypeStruct((n, w), F32)] * 2,
        compiler_params=_cp(("parallel", "arbitrary")),
    )(cdec, rq, rk, rv, rq, rk, rv, dmask, qdec, kdec)


def _ret_finish_kernel(yf_ref, yb_ref, rg_ref, gn_ref, o_ref):
    y = yf_ref[...] + yb_ref[...]
    gate = rg_ref[...]
    gate = gate * jax.nn.sigmoid(gate)
    for h in range(RET_HEADS):
        sl = slice(h * RET_DK, (h + 1) * RET_DK)
        yh = y[:, sl]
        mu = jnp.mean(yh, axis=-1, keepdims=True)
        var = jnp.mean(jnp.square(yh - mu), axis=-1, keepdims=True)
        o_ref[:, sl] = ((yh - mu) * lax.rsqrt(var + GN_EPS) * gn_ref[:, sl] * gate[:, sl]).astype(o_ref.dtype)


def _ret_finish(yf, yb, rg, gn):
    n, w = yf.shape
    tm = ROW_TILE
    spec = pl.BlockSpec((tm, w), lambda t: (t, 0))
    return pl.pallas_call(
        _ret_finish_kernel,
        grid=(n // tm,),
        in_specs=[spec, spec, spec, pl.BlockSpec((1, w), lambda t: (0, 0))],
        out_specs=spec,
        out_shape=jax.ShapeDtypeStruct((n, w), BF16),
        compiler_params=_cp(("parallel",)),
    )(yf, yb, rg, gn)


def _out_kernel(a1_ref, a2_ref, x_ref, w_ref, g1_ref, ng_ref, sh_ref, sc_ref, wr_ref, br_ref,
                xo_ref, m_ref, e_ref, gt_ref):
    half = a1_ref.shape[1]
    o = (jnp.dot(a1_ref[...], w_ref[:half, :], preferred_element_type=F32)
         + jnp.dot(a2_ref[...], w_ref[half:, :], preferred_element_type=F32))
    x = x_ref[...] + g1_ref[0] * o
    xo_ref[...] = x
    m = _modulated_norm(x, ng_ref[...], sc_ref[0], sh_ref[0])
    _store_token_tiles(m_ref, m)
    m_hi = m.astype(BF16)
    m_lo = (m - m_hi.astype(F32)).astype(BF16)
    hi_prod = jnp.dot(m_hi, wr_ref[...], preferred_element_type=F32)
    logits = (hi_prod[:, :LANES] + hi_prod[:, LANES:]
              + jnp.dot(m_lo, wr_ref[:, :LANES], preferred_element_type=F32) + br_ref[...])
    lane = lax.broadcasted_iota(jnp.int32, logits.shape, 1).astype(F32)
    e_out = jnp.zeros(logits.shape, F32)
    g_out = jnp.zeros(logits.shape, F32)
    top0 = None
    denom = None
    for kk in range(TOP_K):
        mx = jnp.max(logits, axis=-1, keepdims=True)
        idx = jnp.min(jnp.where(logits == mx, lane, float(LANES)), axis=-1, keepdims=True)
        if kk == 0:
            top0 = mx
            ex = jnp.ones_like(mx)
            denom = ex
        else:
            ex = jnp.exp(mx - top0)
            denom = denom + ex
        e_out = jnp.where(lane == kk, idx, e_out)
        g_out = jnp.where(lane == kk, ex, g_out)
        logits = jnp.where(lane == idx, NEG * 2.0, logits)
    e_ref[...] = e_out.astype(jnp.int32)
    gt_ref[...] = g_out / denom


def _out_proj(a1, a2, x_all, w_out, mod_l, norm2_g, w_r, b_r, n_rows, tpb, n_batch):
    d = x_all.shape[1]
    half = a1.shape[1]
    tm = ROW_TILE
    n_lat_tiles = tpb * n_batch

    def mod_row(t):
        return jnp.where(t < n_lat_tiles, t // tpb, n_batch)

    row = lambda wd: pl.BlockSpec((tm, wd), lambda t: (t, 0))
    modspec = lambda col: pl.BlockSpec((1, 1, d), lambda t: (mod_row(t), 0, col))
    return pl.pallas_call(
        _out_kernel,
        grid=(n_rows // tm,),
        in_specs=[row(half), row(half), row(d), pl.BlockSpec((2 * half, d), lambda t: (0, 0)),
                  modspec(2), pl.BlockSpec((1, d), lambda t: (0, 0)), modspec(3), modspec(4),
                  pl.BlockSpec((d, 2 * LANES), lambda t: (0, 0)), pl.BlockSpec((1, LANES), lambda t: (0, 0))],
        out_specs=[row(d), pl.BlockSpec((tm * TOK_ROWS, LANES), lambda t: (t, 0)), row(LANES), row(LANES)],
        out_shape=[jax.ShapeDtypeStruct((n_rows, d), F32), jax.ShapeDtypeStruct((n_rows * TOK_ROWS, LANES), F32),
                   jax.ShapeDtypeStruct((n_rows, LANES), jnp.int32), jax.ShapeDtypeStruct((n_rows, LANES), F32)],
        compiler_params=_cp(("parallel",)),
    )(a1, a2, x_all, w_out, mod_l, norm2_g.reshape(1, d), mod_l, mod_l, w_r, b_r)


def _token_rows(ref, idx):
    return ref.at[pl.ds(pl.multiple_of(idx * TOK_ROWS, TOK_ROWS), TOK_ROWS)]


def _dispatch_kernel(pad_start_ref, pad_cnt_ref, pos_ref, m_ref, hs_ref, zero_sc, sem, pad_sem):
    n_tok = m_ref.shape[0] // TOK_ROWS

    @pl.when(pl.program_id(0) == 0)
    def _():
        zero_sc[...] = jnp.zeros(zero_sc.shape, zero_sc.dtype)

        def per_expert(e, carry):
            def fill(r, c):
                pltpu.make_async_copy(zero_sc, _token_rows(hs_ref, pad_start_ref[e] + r), pad_sem).start()
                return c

            lax.fori_loop(0, pad_cnt_ref[e], fill, 0)

            def fill_done(r, c):
                pltpu.make_async_copy(zero_sc, _token_rows(hs_ref, 0), pad_sem).wait()
                return c

            lax.fori_loop(0, pad_cnt_ref[e], fill_done, 0)
            return carry

        lax.fori_loop(0, N_EXPERTS, per_expert, 0)

    def issue(r, carry):
        for kk in range(TOP_K):
            pltpu.make_async_copy(_token_rows(m_ref, r), _token_rows(hs_ref, pos_ref[r * TOP_K + kk]), sem).start()
        return carry

    lax.fori_loop(0, n_tok, issue, 0)

    def drain(r, carry):
        for kk in range(TOP_K):
            pltpu.make_async_copy(_token_rows(m_ref, 0), _token_rows(hs_ref, 0), sem).wait()
        return carry

    lax.fori_loop(0, n_tok, drain, 0)


def _dispatch(m_tiles, pos, pad_start, pad_cnt, cap):
    n_tok = m_tiles.shape[0] // TOK_ROWS
    tm = ROW_TILE
    grid_spec = pltpu.PrefetchScalarGridSpec(
        num_scalar_prefetch=2,
        grid=(n_tok // tm,),
        in_specs=[pl.BlockSpec((tm * TOP_K,), lambda t, ps, pc: (t,), memory_space=pltpu.SMEM),
                  pl.BlockSpec((tm * TOK_ROWS, LANES), lambda t, ps, pc: (t, 0))],
        out_specs=pl.BlockSpec(memory_space=pl.ANY),
        scratch_shapes=[pltpu.VMEM((TOK_ROWS, LANES), F32), pltpu.SemaphoreType.DMA(()), pltpu.SemaphoreType.DMA(())],
    )
    return pl.pallas_call(
        _dispatch_kernel,
        grid_spec=grid_spec,
        out_shape=jax.ShapeDtypeStruct((cap * TOK_ROWS, LANES), F32),
        compiler_params=pltpu.CompilerParams(dimension_semantics=("arbitrary",), has_side_effects=True,
                                             vmem_limit_bytes=VMEM_LIMIT),
    )(pad_start, pad_cnt, pos, m_tiles)


def _expert_kernel(be_ref, nu_ref, x_ref, w1_ref, b1_ref, w2_ref, b2_ref, o_ref, w1_sc, w2_sc):
    i = pl.program_id(0)

    @pl.when((i == 0) | (be_ref[i] != be_ref[jnp.maximum(i - 1, 0)]))
    def _():
        w1_sc[...] = w1_ref[0].astype(BF16)
        w2_sc[...] = w2_ref[0].astype(BF16)

    @pl.when(i < nu_ref[0])
    def _():
        dff = w2_ref.shape[1]
        x = _load_token_tiles(x_ref, MOE_BM).astype(BF16)
        u = jnp.dot(x, w1_sc[...], preferred_element_type=F32) + b1_ref[0]
        gl = jnp.minimum(u[:, :dff], SWIGLU_LIMIT)
        up = jnp.clip(u[:, dff:], -SWIGLU_LIMIT, SWIGLU_LIMIT)
        act = gl * jax.nn.sigmoid(SWIGLU_ALPHA * gl) * (up + 1.0)
        _store_token_tiles(o_ref, jnp.dot(act.astype(BF16), w2_sc[...], preferred_element_type=F32) + b2_ref[0])

    @pl.when(i >= nu_ref[0])
    def _():
        o_ref[...] = jnp.zeros(o_ref.shape, o_ref.dtype)


def _experts(hs, blk_e, n_used, w1_all, b1, w2_all, b2, layer):
    _, ne, d, dff2 = w1_all.shape
    cap = hs.shape[0] // TOK_ROWS
    dff = dff2 // 2
    bm = MOE_BM
    tile_spec = pl.BlockSpec((bm * TOK_ROWS, LANES), lambda i, be, nu: (i, 0))
    grid_spec = pltpu.PrefetchScalarGridSpec(
        num_scalar_prefetch=2,
        grid=(cap // bm,),
        in_specs=[tile_spec,
                  pl.BlockSpec((None, 1, d, dff2), lambda i, be, nu: (layer, be[i], 0, 0)),
                  pl.BlockSpec((1, 1, dff2), lambda i, be, nu: (be[i], 0, 0)),
                  pl.BlockSpec((None, 1, dff, d), lambda i, be, nu: (layer, be[i], 0, 0)),
                  pl.BlockSpec((1, 1, d), lambda i, be, nu: (be[i], 0, 0))],
        out_specs=tile_spec,
        scratch_shapes=[pltpu.VMEM((d, dff2), BF16), pltpu.VMEM((dff, d), BF16)],
    )
    return pl.pallas_call(
        _expert_kernel,
        grid_spec=grid_spec,
        out_shape=jax.ShapeDtypeStruct(hs.shape, F32),
        compiler_params=_cp(("arbitrary",)),
    )(blk_e, n_used, hs, w1_all, b1.reshape(ne, 1, dff2), w2_all, b2.reshape(ne, 1, d))


def _combine_kernel(pos_ref, pos_next_ref, x_ref, gt_ref, g2_ref, out_ref, o_ref, y_sc, sem):
    t = pl.program_id(0)
    n_tok = x_ref.shape[0]
    slot = t % 2

    def gather(p_ref, sl):
        def issue(r, carry):
            for kk in range(TOP_K):
                pltpu.make_async_copy(_token_rows(out_ref, p_ref[r * TOP_K + kk]),
                                      _token_rows(y_sc.at[sl * TOP_K + kk], r), sem.at[sl]).start()
            return carry

        lax.fori_loop(0, n_tok, issue, 0)

    @pl.when(t == 0)
    def _():
        gather(pos_ref, 0)

    @pl.when(t + 1 < pl.num_programs(0))
    def _():
        gather(pos_next_ref, 1 - slot)

    def drain(r, carry):
        for kk in range(TOP_K):
            pltpu.make_async_copy(_token_rows(out_ref, 0), _token_rows(y_sc.at[slot * TOP_K + kk], 0), sem.at[slot]).wait()
        return carry

    lax.fori_loop(0, n_tok, drain, 0)
    gt = gt_ref[...]
    for j in range(TOK_ROWS):
        sl = slice(j * LANES, (j + 1) * LANES)
        acc = y_sc[slot * TOP_K, pl.ds(j, n_tok, stride=TOK_ROWS), :] * gt[:, 0:1]
        for kk in range(1, TOP_K):
            acc = acc + y_sc[slot * TOP_K + kk, pl.ds(j, n_tok, stride=TOK_ROWS), :] * gt[:, kk:kk + 1]
        o_ref[:, sl] = x_ref[:, sl] + g2_ref[0, :, sl] * acc


def _combine(x, out_tiles, pos, gates, mod_l, tpb, n_batch):
    n, d = x.shape
    tm = ROW_TILE
    n_lat_tiles = tpb * n_batch

    def mod_row(t):
        return jnp.where(t < n_lat_tiles, t // tpb, n_batch)

    n_tiles = n // tm
    return pl.pallas_call(
        _combine_kernel,
        grid=(n_tiles,),
        in_specs=[pl.BlockSpec((tm * TOP_K,), lambda t: (t,), memory_space=pltpu.SMEM),
                  pl.BlockSpec((tm * TOP_K,), lambda t: (jnp.minimum(t + 1, n_tiles - 1),), memory_space=pltpu.SMEM),
                  pl.BlockSpec((tm, d), lambda t: (t, 0)), pl.BlockSpec((tm, LANES), lambda t: (t, 0)),
                  pl.BlockSpec((1, 1, d), lambda t: (mod_row(t), 0, 5)), pl.BlockSpec(memory_space=pl.ANY)],
        out_specs=pl.BlockSpec((tm, d), lambda t: (t, 0)),
        out_shape=jax.ShapeDtypeStruct((n, d), F32),
        scratch_shapes=[pltpu.VMEM((2 * TOP_K, tm * TOK_ROWS, LANES), F32), pltpu.SemaphoreType.DMA((2,))],
        compiler_params=_cp(("arbitrary",)),
    )(pos, pos, x, gates, mod_l, out_tiles)


def _moe(m_tiles, top_e, gates, x, mod_l, w1_all, b1, w2_all, b2, layer, tpb, n_batch):
    n, d = x.shape
    nk = n * TOP_K
    bm = MOE_BM
    n_blk = (nk + N_EXPERTS * (bm - 1)) // bm + 1
    cap = n_blk * bm
    flat_e = top_e[:, :TOP_K].reshape(nk)
    onehot = (flat_e[:, None] == jnp.arange(N_EXPERTS, dtype=jnp.int32)[None, :]).astype(jnp.int32)
    csum = jnp.cumsum(onehot, axis=0)
    rank = jnp.sum(jnp.where(onehot > 0, csum, 0), axis=1) - 1
    counts = csum[-1]
    padded = ((counts + bm - 1) // bm) * bm
    pend = jnp.cumsum(padded)
    pstart = pend - padded
    pos = (pstart[flat_e] + rank).astype(jnp.int32)
    n_used = (pend[-1] // bm).astype(jnp.int32)
    blk = jnp.minimum(jnp.arange(n_blk, dtype=jnp.int32), n_used - 1)
    blk_e = jnp.sum((pend[None, :] <= (blk * bm)[:, None]).astype(jnp.int32), axis=1)
    blk_e = jnp.clip(blk_e, 0, N_EXPERTS - 1).astype(jnp.int32)
    hs = _dispatch(m_tiles, pos, (pstart + counts).astype(jnp.int32), (padded - counts).astype(jnp.int32), cap)
    out = _experts(hs, blk_e, n_used.reshape(1), w1_all, b1, w2_all, b2, layer)
    return _combine(x, out, pos, gates, mod_l, tpb, n_batch)


def _rope_tables(seq_len, d_rot, reps, n_extra):
    t = jnp.arange(seq_len)
    rows = (t // GRID_W).astype(F32)
    cols = (t % GRID_W).astype(F32)
    n_freq = d_rot // 4
    inv = ROPE_THETA ** (-jnp.arange(n_freq, dtype=F32) / n_freq)
    ang = jnp.concatenate([rows[:, None] * inv, cols[:, None] * inv], axis=-1)
    cos = jnp.repeat(jnp.cos(ang), 2, axis=-1)
    sin = jnp.repeat(jnp.sin(ang), 2, axis=-1) * jnp.tile(jnp.asarray([-1.0, 1.0], F32), d_rot // 2)
    cos = jnp.concatenate([jnp.tile(cos, (1, reps)), jnp.ones((n_extra, d_rot * reps), F32)], axis=0)
    sin = jnp.concatenate([jnp.tile(sin, (1, reps)), jnp.zeros((n_extra, d_rot * reps), F32)], axis=0)
    return cos, sin


def _retention_tables(decay_logit):
    log_g = jax.nn.log_sigmoid(decay_logit.astype(F32))
    c = RET_CHUNK
    pos = jnp.arange(c, dtype=F32)
    diff = pos[:, None] - pos[None, :]
    lf = log_g[0][:, None, None]
    lb = log_g[1][:, None, None]
    dm_f = jnp.where(diff >= 0, jnp.exp(lf * jnp.where(diff >= 0, diff, 0.0)), 0.0)
    dm_b = jnp.where(diff < 0, jnp.exp(lb * jnp.where(diff < 0, -diff, 0.0)), 0.0)
    qd_f = jnp.exp(log_g[0][:, None] * (pos + 1.0))
    qd_b = jnp.exp(log_g[1][:, None] * (c - pos))
    kd_f = jnp.exp(log_g[0][:, None] * (c - 1.0 - pos))
    kd_b = jnp.exp(log_g[1][:, None] * pos)
    bc = lambda a: jnp.broadcast_to(a[..., None], a.shape + (LANES,))
    dmask = jnp.stack([dm_f, dm_b])
    qdec = jnp.stack([bc(qd_f), bc(qd_b)])
    kdec = jnp.stack([bc(kd_f), bc(kd_b)])
    cdec = jnp.exp(log_g * c).reshape(-1)
    return dmask, qdec, kdec, cdec


def _heads_major(t, n_batch, length, n_heads):
    return t.reshape(n_batch, length, n_heads, -1).transpose(0, 2, 1, 3)


def _heads_t(t, n_batch, length, n_heads):
    return t.reshape(n_batch, length, n_heads, -1).transpose(0, 2, 3, 1)


def kernel(x, c, ctx, c_ctx, norm1_g, norm2_g, w_mod, b_mod, w_in_even, w_out_even, a_q_norm, a_k_norm, b_q_norm, b_k_norm, b_rpb, w_in_odd, w_out_odd, ret_decay, ret_gn, mla_cq_norm, mla_ckv_norm, w_uq, w_ukv, mla_q_norm, mla_k_norm, w_router, b_router, w_exp1, b_exp1, w_exp2, b_exp2):
    bsz, s, d = x.shape
    cl = ctx.shape[1]
    depth = w_mod.shape[0]
    tm = ROW_TILE
    assert s % tm == 0 and (bsz * cl) == tm and s % NA_QB == 0 and bsz + 1 <= 8 and d == TOK_ROWS * LANES
    tpb = s // tm
    nl = bsz * s
    n_all = nl + bsz * cl

    c_rows = jnp.zeros((8, d), F32).at[:bsz].set(c).at[bsz].set(c_ctx)
    mod = _mod_vectors(c_rows, w_mod, b_mod)

    cos_a, sin_a = _rope_tables(s, HEAD_DIM, LANES // HEAD_DIM, tm)
    cos_c, sin_c = _rope_tables(s, RET_DK, 1, tm)
    cos_d, sin_d = _rope_tables(s, MLA_ROPE, LANES // MLA_ROPE, tm)
    na_bases, na_var, na_row_sel, na_col_in, na_dc_onehot = _na_tables(s)

    x_all = jnp.concatenate([x.reshape(nl, d), ctx.reshape(bsz * cl, d)], axis=0)
    lat = lambda t: t[:nl]
    cx = lambda t: t[nl:]
    att_scale = HEAD_DIM ** -0.5 * LOG2E
    grp = GQA_HEADS // GQA_KV_HEADS
    bkv = bsz * GQA_KV_HEADS

    for l in range(depth):
        need_ctx = l < depth - 1
        i = l // 2
        mod_l = mod[l].reshape(8, 1, 6 * d)
        if l % 2 == 0:
            wq, wk, wv, wn = GQA_HEADS * HEAD_DIM, GQA_KV_HEADS * HEAD_DIM, GQA_KV_HEADS * HEAD_DIM, NA_HEADS * HEAD_DIM
            starts = np.cumsum([0, wq, wk, wv, wn, wn])
            segs = [(int(starts[0]), wq, "norm_rope", 1.0), (int(starts[1]), wk, "norm_rope", 1.0),
                    (int(starts[2]), wv, "plain", 1.0), (int(starts[3]), wn, "norm", 1.0),
                    (int(starts[4]), wn, "norm", 1.0), (int(starts[5]), wn, "plain", 1.0)]
            gain = jnp.concatenate([jnp.tile(a_q_norm[i], GQA_HEADS) * att_scale, jnp.tile(a_k_norm[i], GQA_KV_HEADS),
                                    jnp.ones((wv,), F32), jnp.tile(b_q_norm[i], NA_HEADS) * att_scale,
                                    jnp.tile(b_k_norm[i], NA_HEADS), jnp.ones((wn,), F32)]).reshape(1, -1)
            qa, ka, va, qb, kb, vb = _proj(x_all, mod_l, norm1_g[l], w_in_even[i].astype(BF16), gain, cos_a, sin_a,
                                           segs, [BF16] * 6, tpb, bsz)
            k_lat = _heads_major(lat(ka), bsz, s, GQA_KV_HEADS)
            k_cx = _heads_major(cx(ka), bsz, cl, GQA_KV_HEADS)
            v_lat = _heads_major(lat(va), bsz, s, GQA_KV_HEADS)
            v_cx = _heads_major(cx(va), bsz, cl, GQA_KV_HEADS)
            k_all = jnp.concatenate([k_lat, k_cx], axis=2).reshape(bkv, s + cl, HEAD_DIM)
            v_all = jnp.concatenate([v_lat, v_cx], axis=2).reshape(bkv, s + cl, HEAD_DIM)
            qt = _heads_t(lat(qa), bsz, s, GQA_HEADS).reshape(bkv, grp, HEAD_DIM, s)
            oa_t = _flash(qt, k_all, v_all, Q_SUB)
            oa = oa_t.reshape(bsz, GQA_HEADS, HEAD_DIM, s).transpose(0, 3, 1, 2).reshape(nl, wq)
            bias = _na_bias(b_rpb[i], na_row_sel, na_col_in, na_dc_onehot)
            bh = bsz * NA_HEADS
            ob = _na_attention(qb, kb, vb, bias, jnp.asarray(na_bases), jnp.asarray(na_var), bsz, s, cl)
            kn_c = _heads_major(cx(kb), bsz, cl, NA_HEADS).reshape(bh, cl, HEAD_DIM)
            vn_c = _heads_major(cx(vb), bsz, cl, NA_HEADS).reshape(bh, cl, HEAD_DIM)
            if need_ctx:
                qt_c = _heads_t(cx(qa), bsz, cl, GQA_HEADS).reshape(bkv, grp, HEAD_DIM, cl)
                oa_c = _flash(qt_c, k_cx.reshape(bkv, cl, HEAD_DIM), v_cx.reshape(bkv, cl, HEAD_DIM), Q_SUB)
                oa_c = oa_c.reshape(bsz, GQA_HEADS, HEAD_DIM, cl).transpose(0, 3, 1, 2).reshape(bsz * cl, wq)
                qnt_c = _heads_t(cx(qb), bsz, cl, NA_HEADS).reshape(bh, 1, HEAD_DIM, cl)
                ob_c = _flash(qnt_c, kn_c, vn_c, Q_SUB)
                ob_c = ob_c.reshape(bsz, NA_HEADS, HEAD_DIM, cl).transpose(0, 3, 1, 2).reshape(bsz * cl, wn)
                a1 = jnp.concatenate([oa, oa_c], axis=0)
                a2 = jnp.concatenate([ob, ob_c], axis=0)
            else:
                a1, a2 = oa, ob
            w_out = w_out_even[i].astype(BF16)
        else:
            rw = RET_HEADS * RET_DK
            kr_cols = w_in_odd[i][:, 4 * rw + MLA_Q_LORA + MLA_KV_LORA:]
            w_ext = jnp.concatenate([w_in_odd[i]] + [kr_cols] * (LANES * 2 // MLA_ROPE - 1), axis=1).astype(BF16)
            mla_w = MLA_Q_LORA + MLA_KV_LORA + 2 * LANES
            segs = [(0, rw, "rope", RET_DK ** -0.5), (rw, rw, "rope", 1.0), (2 * rw, rw, "plain", 1.0),
                    (3 * rw, rw, "plain", 1.0), (4 * rw, mla_w, "plain", 1.0)]
            gain = jnp.ones((1, w_ext.shape[1]), F32)
            rq, rk, rv, rg, mla_in = _proj(x_all, mod_l, norm1_g[l], w_ext, gain, cos_c, sin_c, segs,
                                           [BF16, BF16, BF16, F32, F32], tpb, bsz)
            dmask, qdec, kdec, cdec = _retention_tables(ret_decay[i])
            yf, yb = _retention(rq, rk, rv, dmask, qdec, kdec, cdec, bsz, s, cl)
            a1 = _ret_finish(yf, yb, rg, ret_gn[i].reshape(1, rw))
            dqk = MLA_NOPE + MLA_ROPE
            perm_q = np.concatenate([np.arange(h * dqk, h * dqk + MLA_NOPE) for h in range(MLA_HEADS)]
                                    + [np.arange(h * dqk + MLA_NOPE, (h + 1) * dqk) for h in range(MLA_HEADS)])
            dkv = MLA_NOPE + MLA_V
            perm_kv = np.concatenate([np.arange(h * dkv, h * dkv + MLA_NOPE) for h in range(MLA_HEADS)]
                                     + [np.arange(h * dkv + MLA_NOPE, (h + 1) * dkv) for h in range(MLA_HEADS)])
            mla_scale = dqk ** -0.5 * LOG2E
            gq = jnp.concatenate([jnp.tile(mla_q_norm[i][:MLA_NOPE], MLA_HEADS),
                                  jnp.tile(mla_q_norm[i][MLA_NOPE:], MLA_HEADS)]).reshape(1, -1) * mla_scale
            gkn = mla_k_norm[i][:MLA_NOPE].reshape(1, -1)
            gkr = jnp.tile(mla_k_norm[i][MLA_NOPE:], LANES // MLA_ROPE).reshape(1, -1)
            q_m, k_m, v_m = _mla_proj(mla_in, w_uq[i][:, perm_q].astype(BF16), w_ukv[i][:, perm_kv].astype(BF16),
                                      mla_cq_norm[i].reshape(1, -1), mla_ckv_norm[i].reshape(1, -1), gq, gkn, gkr,
                                      cos_d, sin_d, tpb, bsz)
            nw = MLA_HEADS * MLA_NOPE

            def qk_heads(t, length, transposed):
                nope = t[:, :nw].reshape(bsz, length, MLA_HEADS, MLA_NOPE)
                rope = t[:, nw:].reshape(bsz, length, MLA_HEADS, MLA_ROPE)
                full = jnp.concatenate([nope, rope], axis=-1)
                return full.transpose(0, 2, 3, 1) if transposed else full.transpose(0, 2, 1, 3)

            bhm = bsz * MLA_HEADS
            k_lat = qk_heads(lat(k_m), s, False)
            k_cx = qk_heads(cx(k_m), cl, False)
            v_lat = _heads_major(lat(v_m), bsz, s, MLA_HEADS)
            v_cx = _heads_major(cx(v_m), bsz, cl, MLA_HEADS)
            k_all = jnp.concatenate([k_lat, k_cx], axis=2).reshape(bhm, s + cl, dqk)
            v_all = jnp.concatenate([v_lat, v_cx], axis=2).reshape(bhm, s + cl, MLA_V)
            qt = qk_heads(lat(q_m), s, True).reshape(bhm, 1, dqk, s)
            om = _flash(qt, k_all, v_all, 4 * Q_SUB)
            om = om.reshape(bsz, MLA_HEADS, MLA_V, s).transpose(0, 3, 1, 2).reshape(nl, MLA_HEADS * MLA_V)
            if need_ctx:
                qt_c = qk_heads(cx(q_m), cl, True).reshape(bhm, 1, dqk, cl)
                om_c = _flash(qt_c, k_cx.reshape(bhm, cl, dqk), v_cx.reshape(bhm, cl, MLA_V), Q_SUB)
                om_c = om_c.reshape(bsz, MLA_HEADS, MLA_V, cl).transpose(0, 3, 1, 2).reshape(bsz * cl, MLA_HEADS * MLA_V)
                a2 = jnp.concatenate([om, om_c], axis=0)
            else:
                a1 = a1[:nl]
                a2 = om
            w_out = w_out_odd[i].astype(BF16)

        n_rows = n_all if need_ctx else nl
        w_r_hi32 = lax.reduce_precision(w_router[l].astype(F32), exponent_bits=8, mantissa_bits=7)
        w_r_hi = w_r_hi32.astype(BF16)
        w_r_lo = (w_router[l] - w_r_hi32).astype(BF16)
        w_r = (jnp.zeros((d, 2 * LANES), BF16).at[:, :N_EXPERTS].set(w_r_hi)
               .at[:, LANES:LANES + N_EXPERTS].set(w_r_lo))
        b_r = jnp.full((1, LANES), NEG, F32).at[0, :N_EXPERTS].set(b_router[l])
        x_new, m, top_e, gates = _out_proj(a1, a2, x_all, w_out, mod_l, norm2_g[l], w_r, b_r, n_rows, tpb, bsz)
        x_all = _moe(m, top_e, gates, x_new, mod_l, w_exp1, b_exp1[l], w_exp2, b_exp2[l], l, tpb, bsz)
    return x_all[:nl].reshape(bsz, s, d)
```

```python
import functools
import math

import numpy as np
import jax
import jax.numpy as jnp
from jax import lax
from jax.experimental import pallas as pl
from jax.experimental.pallas import tpu as pltpu

F32 = jnp.float32
BF16 = jnp.bfloat16

GRID_W = 64
HEAD_DIM = 64
GQA_HEADS = 8
GQA_KV_HEADS = 2
NA_HEADS = 8
NA_ROWS = 8
NA_COLS = 16
RET_HEADS = 4
RET_DK = 128
MLA_HEADS = 4
MLA_Q_LORA = 256
MLA_KV_LORA = 128
MLA_NOPE = 128
MLA_ROPE = 64
MLA_V = 128
N_EXPERTS = 32
TOP_K = 4
SWIGLU_LIMIT = 7.0
SWIGLU_ALPHA = 1.702
ROPE_THETA = 10000.0
EPS = 1e-6
GN_EPS = 1e-5
LOG2E = math.log2(math.e)
NEG = -1e30

LANES = 128
ROW_TILE = 512
Q_SUB = 256
FLASH_TILES = 2
ONES_ROWS = 16
NA_QB = 256
NA_WIN_ROWS = 12
RET_CHUNK = 256
MOE_BM = 512
TOK_ROWS = 8
VMEM_LIMIT = 56 * 1024 * 1024


def _cp(sem):
    return pltpu.CompilerParams(dimension_semantics=sem, vmem_limit_bytes=VMEM_LIMIT)


def _mod_kernel(c_ref, w_ref, b_ref, o_ref):
    c = c_ref[...]
    s = c * jax.nn.sigmoid(c)
    o_ref[0] = jnp.dot(s, w_ref[0], precision=lax.Precision.HIGHEST, preferred_element_type=F32) + b_ref[0]


def _mod_vectors(c_rows, w_mod, b_mod):
    depth, d, d6 = w_mod.shape
    tn = 1536
    return pl.pallas_call(
        _mod_kernel,
        grid=(depth, d6 // tn),
        in_specs=[pl.BlockSpec((8, d), lambda l, j: (0, 0)),
                  pl.BlockSpec((1, d, tn), lambda l, j: (l, 0, j)),
                  pl.BlockSpec((1, 1, tn), lambda l, j: (l, 0, j))],
        out_specs=pl.BlockSpec((1, 8, tn), lambda l, j: (l, 0, j)),
        out_shape=jax.ShapeDtypeStruct((depth, 8, d6), F32),
        compiler_params=_cp(("parallel", "parallel")),
    )(c_rows, w_mod, b_mod.reshape(depth, 1, d6))


def _modulated_norm(x, g, sc, sh):
    ms = jnp.mean(x * x, axis=-1, keepdims=True)
    return x * lax.rsqrt(ms + EPS) * g * (1.0 + sc) + sh


def _pair_rope(y, cos, sin_signed):
    lane = lax.broadcasted_iota(jnp.int32, y.shape, 1)
    partner = jnp.where((lane & 1) == 0, pltpu.roll(y, LANES - 1, 1), pltpu.roll(y, 1, 1))
    return y * cos + partner * sin_signed


def _store_token_tiles(ref, val):
    n = val.shape[0]
    for j in range(TOK_ROWS):
        ref[pl.ds(j, n, stride=TOK_ROWS), :] = val[:, j * LANES:(j + 1) * LANES]


def _load_token_tiles(ref, n):
    return jnp.concatenate([ref[pl.ds(j, n, stride=TOK_ROWS), :] for j in range(TOK_ROWS)], axis=-1)


def _split_dot(a_f32, w_bf16):
    hi = a_f32.astype(BF16)
    lo = (a_f32 - hi.astype(F32)).astype(BF16)
    return (jnp.dot(hi, w_bf16, preferred_element_type=F32) + jnp.dot(lo, w_bf16, preferred_element_type=F32))


def _proj_kernel(x_ref, sh_ref, sc_ref, g_ref, w_ref, gain_ref, bd_ref, cos_ref, sin_ref, *out_refs, segs):
    a = _modulated_norm(x_ref[...], g_ref[...], sc_ref[0], sh_ref[0]).astype(BF16)
    cos = cos_ref[...]
    sin = sin_ref[...]
    for (start, width, mode, scale), o_ref in zip(segs, out_refs):
        y_seg = jnp.dot(a, w_ref[:, start:start + width], preferred_element_type=F32)
        if mode == "plain":
            o_ref[...] = y_seg.astype(o_ref.dtype)
            continue
        for j in range(width // LANES):
            y = y_seg[:, j * LANES:(j + 1) * LANES]
            if "norm" in mode:
                ms = _split_dot(y * y, bd_ref[...])
                y = y * lax.rsqrt(ms + EPS) * gain_ref[:, start + j * LANES:start + (j + 1) * LANES]
            if scale != 1.0:
                y = y * scale
            if "rope" in mode:
                y = _pair_rope(y, cos, sin)
            o_ref[:, j * LANES:(j + 1) * LANES] = y.astype(o_ref.dtype)


def _proj(x_all, mod_l, norm_g, w, gain, cos_t, sin_t, segs, out_dtypes, n_lat_tiles_per_batch, n_batch):
    n, d = x_all.shape
    tm = ROW_TILE
    n_tiles = n // tm
    wtot = w.shape[1]
    tpb = n_lat_tiles_per_batch
    n_lat_tiles = tpb * n_batch

    def mod_row(t):
        return jnp.where(t < n_lat_tiles, t // tpb, n_batch)

    def rope_row(t):
        return jnp.where(t < n_lat_tiles, t % tpb, tpb)

    bd = np.kron(np.eye(2, dtype=np.float32), np.full((HEAD_DIM, HEAD_DIM), 1.0 / HEAD_DIM, np.float32))
    in_specs = [
        pl.BlockSpec((tm, d), lambda t: (t, 0)),
        pl.BlockSpec((1, 1, d), lambda t: (mod_row(t), 0, 0)),
        pl.BlockSpec((1, 1, d), lambda t: (mod_row(t), 0, 1)),
        pl.BlockSpec((1, d), lambda t: (0, 0)),
        pl.BlockSpec((d, wtot), lambda t: (0, 0)),
        pl.BlockSpec((1, wtot), lambda t: (0, 0)),
        pl.BlockSpec((LANES, LANES), lambda t: (0, 0)),
        pl.BlockSpec((tm, LANES), lambda t: (rope_row(t), 0)),
        pl.BlockSpec((tm, LANES), lambda t: (rope_row(t), 0)),
    ]
    out_specs = [pl.BlockSpec((tm, s[1]), lambda t: (t, 0)) for s in segs]
    out_shape = [jax.ShapeDtypeStruct((n, s[1]), dt) for s, dt in zip(segs, out_dtypes)]
    return pl.pallas_call(
        functools.partial(_proj_kernel, segs=tuple(segs)),
        grid=(n_tiles,),
        in_specs=in_specs,
        out_specs=out_specs,
        out_shape=out_shape,
        compiler_params=_cp(("parallel",)),
    )(x_all, mod_l, mod_l, norm_g.reshape(1, d), w, gain, jnp.asarray(bd, BF16), cos_t, sin_t)


def _mla_proj_kernel(x_ref, wuq_ref, wukv_ref, gcq_ref, gckv_ref, gq_ref, gkn_ref, gkr_ref, cos_ref, sin_ref,
                     q_ref, k_ref, v_ref):
    x = x_ref[...]
    cq = x[:, :MLA_Q_LORA]
    ckv = x[:, MLA_Q_LORA:MLA_Q_LORA + MLA_KV_LORA]
    kr = x[:, MLA_Q_LORA + MLA_KV_LORA:MLA_Q_LORA + MLA_KV_LORA + LANES]
    cos = cos_ref[...]
    sin = sin_ref[...]
    cqn = cq * lax.rsqrt(jnp.mean(cq * cq, axis=-1, keepdims=True) + EPS) * gcq_ref[...]
    ckvn = ckv * lax.rsqrt(jnp.mean(ckv * ckv, axis=-1, keepdims=True) + EPS) * gckv_ref[...]
    q = jnp.dot(cqn.astype(BF16), wuq_ref[...], preferred_element_type=F32)
    kv = jnp.dot(ckvn.astype(BF16), wukv_ref[...], preferred_element_type=F32)
    nh = MLA_HEADS
    d_qk = float(MLA_NOPE + MLA_ROPE)
    lane = lax.broadcasted_iota(jnp.int32, (1, LANES), 1)
    low = lane < MLA_ROPE

    def half_sums(slab):
        sq = slab * slab
        a = jnp.sum(jnp.where(low, sq, 0.0), axis=-1, keepdims=True)
        return a, jnp.sum(sq, axis=-1, keepdims=True) - a

    rope_w = nh * MLA_NOPE
    q_rope_ss = []
    for r in range(nh // 2):
        q_rope_ss.extend(half_sums(q[:, rope_w + r * LANES:rope_w + (r + 1) * LANES]))
    kr_ss, _ = half_sums(kr)
    rs_q, rs_k = [], []
    for h in range(nh):
        qn = q[:, h * MLA_NOPE:(h + 1) * MLA_NOPE]
        kn = kv[:, h * MLA_NOPE:(h + 1) * MLA_NOPE]
        rs_q.append(lax.rsqrt((jnp.sum(qn * qn, axis=-1, keepdims=True) + q_rope_ss[h]) / d_qk + EPS))
        rs_k.append(lax.rsqrt((jnp.sum(kn * kn, axis=-1, keepdims=True) + kr_ss) / d_qk + EPS))
        q_ref[:, h * MLA_NOPE:(h + 1) * MLA_NOPE] = (qn * rs_q[h] * gq_ref[:, h * MLA_NOPE:(h + 1) * MLA_NOPE]).astype(q_ref.dtype)
        k_ref[:, h * MLA_NOPE:(h + 1) * MLA_NOPE] = (kn * rs_k[h] * gkn_ref[...]).astype(k_ref.dtype)
    kr_rot = _pair_rope(kr * gkr_ref[...], cos, sin)
    for r in range(nh // 2):
        sl = slice(rope_w + r * LANES, rope_w + (r + 1) * LANES)
        yq = q[:, sl] * jnp.where(low, rs_q[2 * r], rs_q[2 * r + 1]) * gq_ref[:, sl]
        q_ref[:, sl] = _pair_rope(yq, cos, sin).astype(q_ref.dtype)
        k_ref[:, sl] = (kr_rot * jnp.where(low, rs_k[2 * r], rs_k[2 * r + 1])).astype(k_ref.dtype)
    v_ref[...] = kv[:, nh * MLA_NOPE:].astype(v_ref.dtype)


def _mla_proj(mla_in, wuq, wukv, gcq, gckv, gq, gkn, gkr, cos_t, sin_t, tpb, n_batch):
    n, win = mla_in.shape
    tm = ROW_TILE
    n_lat_tiles = tpb * n_batch

    def rope_row(t):
        return jnp.where(t < n_lat_tiles, t % tpb, tpb)

    qk_w = MLA_HEADS * (MLA_NOPE + MLA_ROPE)
    v_w = MLA_HEADS * MLA_V
    full = lambda a: pl.BlockSpec(a.shape, lambda t: (0,) * a.ndim)
    return pl.pallas_call(
        _mla_proj_kernel,
        grid=(n // tm,),
        in_specs=[pl.BlockSpec((tm, win), lambda t: (t, 0)), full(wuq), full(wukv), full(gcq), full(gckv),
                  full(gq), full(gkn), full(gkr),
                  pl.BlockSpec((tm, LANES), lambda t: (rope_row(t), 0)),
                  pl.BlockSpec((tm, LANES), lambda t: (rope_row(t), 0))],
        out_specs=[pl.BlockSpec((tm, qk_w), lambda t: (t, 0)), pl.BlockSpec((tm, qk_w), lambda t: (t, 0)),
                   pl.BlockSpec((tm, v_w), lambda t: (t, 0))],
        out_shape=[jax.ShapeDtypeStruct((n, qk_w), BF16), jax.ShapeDtypeStruct((n, qk_w), BF16),
                   jax.ShapeDtypeStruct((n, v_w), BF16)],
        compiler_params=_cp(("parallel",)),
    )(mla_in, wuq, wukv, gcq, gckv, gq, gkn, gkr, cos_t, sin_t)


def _flash_kernel(qt_ref, k_ref, vt_ref, ot_ref, q_sc, s_0, s_1, s_2, mx_0, mx_1, mx_2, m_sc, acc_sc, *,
                  g, n_chunks, dv, n_tiles):
    bufs = ((s_0, mx_0), (s_1, mx_1), (s_2, mx_2))
    tq = qt_ref.shape[3] // n_tiles
    for t in range(n_tiles):
        for gi in range(g):
            q_sc[t, :, gi * tq:(gi + 1) * tq] = qt_ref[0, gi, :, t * tq:(t + 1) * tq]

    def reset():
        m_sc[...] = jnp.full(m_sc.shape, NEG, F32)
        acc_sc[...] = jnp.zeros(acc_sc.shape, F32)

    def finalize(t):
        acc = acc_sc[...]
        o = acc[:dv] / acc[dv:dv + 1]
        for gi in range(g):
            ot_ref[0, gi, :, t * tq:(t + 1) * tq] = o[:, gi * tq:(gi + 1) * tq].astype(ot_ref.dtype)

    def scores(t, ci, s_ref, mx_ref):
        s = jnp.dot(k_ref[0, ci], q_sc[t], preferred_element_type=F32)
        s_ref[...] = s
        mx_ref[...] = jnp.max(s, axis=0, keepdims=True)

    def accumulate(ci, s_ref, mx_ref):
        m_old = m_sc[...]
        m_new = jnp.maximum(m_old, mx_ref[...])
        alpha = jnp.exp2(m_old - m_new)
        p = jnp.exp2(s_ref[...] - m_new).astype(BF16)
        acc_sc[...] = alpha * acc_sc[...] + jnp.dot(vt_ref[0, ci], p, preferred_element_type=F32)
        m_sc[...] = m_new

    def fused(t_n, ci_n, s_n, mx_n, ci_c, s_c, mx_c):
        m_old = m_sc[...]
        m_new = jnp.maximum(m_old, mx_c[...])
        alpha = jnp.exp2(m_old - m_new)
        pv = None
        mx = None
        for k0 in range(0, s_c.shape[0], 256):
            s = jnp.dot(k_ref[0, ci_n, k0:k0 + 256, :], q_sc[t_n], preferred_element_type=F32)
            s_n[k0:k0 + 256, :] = s
            mxj = jnp.max(s, axis=0, keepdims=True)
            mx = mxj if mx is None else jnp.maximum(mx, mxj)
            p = jnp.exp2(s_c[k0:k0 + 256, :] - m_new).astype(BF16)
            part = jnp.dot(vt_ref[0, ci_c, :, k0:k0 + 256], p, preferred_element_type=F32)
            pv = part if pv is None else pv + part
        mx_n[...] = mx
        acc_sc[...] = alpha * acc_sc[...] + pv
        m_sc[...] = m_new

    reset()
    if n_chunks == 1:
        for t in range(n_tiles):
            scores(t, 0, *bufs[0])
            accumulate(0, *bufs[0])
            finalize(t)
            reset()
    else:
        scores(0, 0, *bufs[0])
        scores(0, 1, *bufs[1])
        n_fused = n_chunks - 2
        for t in range(n_tiles):
            off = t * n_chunks

            def triple(j, carry, t=t, off=off):
                c = 3 * j
                for r in range(3):
                    fused(t, c + r + 2, *bufs[(r + 2 + off) % 3], c + r, *bufs[(r + off) % 3])
                return carry

            lax.fori_loop(0, n_fused // 3, triple, 0)
            for c in range(3 * (n_fused // 3), n_fused):
                fused(t, c + 2, *bufs[(c + 2 + off) % 3], c, *bufs[(c + off) % 3])
            for c in (n_chunks - 2, n_chunks - 1):
                if t + 1 < n_tiles:
                    c_n = c - (n_chunks - 2)
                    fused(t + 1, c_n, *bufs[(c_n + off + n_chunks) % 3], c, *bufs[(c + off) % 3])
                else:
                    accumulate(c, *bufs[(c + off) % 3])
            finalize(t)
            reset()


def _key_chunk(lk):
    for tk in (1280, 1024, 768, 512, 256):
        if lk % tk == 0:
            return tk
    raise ValueError(f"key length {lk} must be a multiple of 256")


def _flash(qt, k, v, tq_blk):
    bk, g, dq, lq = qt.shape
    lk, dv = v.shape[1], v.shape[2]
    tk = _key_chunk(lk)
    nch = lk // tk
    kc = k.reshape(bk, nch, tk, dq)
    dve = dv + ONES_ROWS
    vt = jnp.concatenate([v, jnp.ones((bk, lk, ONES_ROWS), v.dtype)], axis=-1)
    vt = vt.reshape(bk, nch, tk, dve).transpose(0, 1, 3, 2)
    tq_blk = min(tq_blk, lq)
    n_tiles = FLASH_TILES if (nch > 1 and lq % (FLASH_TILES * tq_blk) == 0) else 1
    assert lq % tq_blk == 0 and tq_blk % LANES == 0
    w = g * tq_blk
    blk = n_tiles * tq_blk
    return pl.pallas_call(
        functools.partial(_flash_kernel, g=g, n_chunks=nch, dv=dv, n_tiles=n_tiles),
        grid=(bk, lq // blk),
        in_specs=[pl.BlockSpec((1, g, dq, blk), lambda b, i: (b, 0, 0, i)),
                  pl.BlockSpec((1, nch, tk, dq), lambda b, i: (b, 0, 0, 0)),
                  pl.BlockSpec((1, nch, dve, tk), lambda b, i: (b, 0, 0, 0))],
        out_specs=pl.BlockSpec((1, g, dv, blk), lambda b, i: (b, 0, 0, i)),
        out_shape=jax.ShapeDtypeStruct((bk, g, dv, lq), BF16),
        scratch_shapes=[pltpu.VMEM((n_tiles, dq, w), BF16)] + [pltpu.VMEM((tk, w), F32)] * 3
                       + [pltpu.VMEM((1, w), F32)] * 4 + [pltpu.VMEM((dve, w), F32)],
        compiler_params=_cp(("parallel", "parallel")),
    )(qt, kc, vt)


def _na_tables(seq_len):
    rows_n = seq_len // GRID_W
    assert rows_n >= NA_WIN_ROWS and NA_ROWS <= rows_n
    nb = seq_len // NA_QB
    rpq = NA_QB // GRID_W
    band = NA_WIN_ROWS * GRID_W
    variants, var_id, bases = {}, [], []
    for j in range(nb):
        base = int(np.clip(rpq * j - NA_ROWS // 2, 0, rows_n - NA_WIN_ROWS))
        bases.append(base)
        t = np.arange(NA_QB) + j * NA_QB
        r, col = t // GRID_W, t % GRID_W
        r0 = np.clip(r - NA_ROWS // 2, 0, rows_n - NA_ROWS)
        c0 = np.clip(col - NA_COLS // 2, 0, GRID_W - NA_COLS)
        kk = np.arange(band)
        kr = base + kk // GRID_W
        kc = kk % GRID_W
        inside = ((kr[None] >= r0[:, None]) & (kr[None] < r0[:, None] + NA_ROWS)
                  & (kc[None] >= c0[:, None]) & (kc[None] < c0[:, None] + NA_COLS))
        rel = (kr[None] - r[:, None] + NA_ROWS - 1) * (2 * NA_COLS - 1) + (kc[None] - col[:, None] + NA_COLS - 1)
        tab = np.where(inside, rel, -1).astype(np.int32)
        assert (inside.sum(axis=1) == NA_ROWS * NA_COLS).all()
        key = tab.tobytes()
        if key not in variants:
            variants[key] = (len(variants), tab)
        var_id.append(variants[key][0])
    tabs = np.stack([v[1] for v in sorted(variants.values(), key=lambda kv: kv[0])])
    n_dr = 2 * NA_ROWS - 1
    n_dc = 2 * NA_COLS - 1
    col = np.arange(GRID_W)
    c0 = np.clip(col - NA_COLS // 2, 0, GRID_W - NA_COLS)
    col_in = (col[None] >= c0[:, None]) & (col[None] < c0[:, None] + NA_COLS)
    dc = col[None] - col[:, None] + NA_COLS - 1
    t5 = tabs.reshape(len(tabs), rpq, GRID_W, NA_WIN_ROWS, GRID_W)
    row_sel = np.full((len(tabs), rpq, NA_WIN_ROWS), n_dr, np.int32)
    for v in range(len(tabs)):
        for a in range(rpq):
            for i in range(NA_WIN_ROWS):
                blk = t5[v, a, :, i, :]
                if (blk >= 0).any():
                    dr = int(blk[blk >= 0][0]) // n_dc
                    assert (np.where(col_in, dr * n_dc + dc, -1) == blk).all()
                    row_sel[v, a, i] = dr
                else:
                    assert (blk < 0).all()
    dc_onehot = (dc[None] == np.arange(n_dc)[:, None, None]).astype(np.float32)
    return np.asarray(bases, np.int32), np.asarray(var_id, np.int32), row_sel, col_in, dc_onehot


def _na_bias(rpb, row_sel, col_in, dc_onehot):
    h = rpb.shape[0]
    n_dr, n_dc = 2 * NA_ROWS - 1, 2 * NA_COLS - 1
    t = jnp.einsum("hdj,jck->hdck", rpb.astype(F32).reshape(h, n_dr, n_dc) * LOG2E, jnp.asarray(dc_onehot),
                   precision=lax.Precision.HIGHEST)
    t = jnp.where(jnp.asarray(col_in)[None, None], t, NEG)
    t = jnp.concatenate([t, jnp.full((h, 1, GRID_W, GRID_W), NEG, F32)], axis=1)
    nv, rpq, nw = row_sel.shape
    b = t[:, jnp.asarray(row_sel)]
    return b.transpose(1, 0, 2, 4, 3, 5).reshape(nv, h, rpq * GRID_W, nw * GRID_W)


def _na_kernel(base_ref, var_ref, q_ref, k_ref, v_ref, kc_ref, vc_ref, bias_ref, o_ref):
    j = pl.program_id(2)
    band = NA_WIN_ROWS * GRID_W
    start = pl.multiple_of(base_ref[j] * GRID_W, GRID_W)
    q = q_ref[...]
    kw = k_ref[pl.ds(start, band), :]
    vw = v_ref[pl.ds(start, band), :]
    kc = kc_ref[...]
    vc = vc_ref[...]
    nt = (((1,), (1,)), ((), ()))
    lane = lax.broadcasted_iota(jnp.int32, q.shape, 1)
    outs = []
    for hh in range(LANES // HEAD_DIM):
        mine = (lane >= hh * HEAD_DIM) & (lane < (hh + 1) * HEAD_DIM)
        qh = jnp.where(mine, q, jnp.zeros_like(q))
        s_win = lax.dot_general(qh, kw, nt, preferred_element_type=F32) + bias_ref[0, hh]
        s_ctx = lax.dot_general(qh, kc, nt, preferred_element_type=F32)
        m = jnp.maximum(jnp.max(s_win, axis=-1, keepdims=True), jnp.max(s_ctx, axis=-1, keepdims=True))
        p_win = jnp.exp2(s_win - m)
        p_ctx = jnp.exp2(s_ctx - m)
        l = jnp.sum(p_win, axis=-1, keepdims=True) + jnp.sum(p_ctx, axis=-1, keepdims=True)
        o = (jnp.dot(p_win.astype(BF16), vw, preferred_element_type=F32)
             + jnp.dot(p_ctx.astype(BF16), vc, preferred_element_type=F32))
        outs.append(o / l)
    o_ref[...] = jnp.where(lane < HEAD_DIM, outs[0], outs[1]).astype(o_ref.dtype)


def _na_attention(q, k, v, bias, bases, var_id, n_batch, seq_len, ctx_len):
    w = q.shape[1]
    assert LANES // HEAD_DIM == 2 and ctx_len % 8 == 0 and (n_batch * seq_len) % ctx_len == 0
    band = NA_WIN_ROWS * GRID_W
    nb = seq_len // NA_QB
    ctx0 = (n_batch * seq_len) // ctx_len
    lat_spec = pl.BlockSpec((seq_len, LANES), lambda b, p, j, bs, vr: (b, p))
    ctx_spec = pl.BlockSpec((ctx_len, LANES), lambda b, p, j, bs, vr: (ctx0 + b, p))
    grid_spec = pltpu.PrefetchScalarGridSpec(
        num_scalar_prefetch=2,
        grid=(n_batch, w // LANES, nb),
        in_specs=[pl.BlockSpec((NA_QB, LANES), lambda b, p, j, bs, vr: (b * nb + j, p)),
                  lat_spec, lat_spec, ctx_spec, ctx_spec,
                  pl.BlockSpec((1, LANES // HEAD_DIM, NA_QB, band), lambda b, p, j, bs, vr: (vr[j], p, 0, 0))],
        out_specs=pl.BlockSpec((NA_QB, LANES), lambda b, p, j, bs, vr: (b * nb + j, p)),
    )
    return pl.pallas_call(
        _na_kernel,
        grid_spec=grid_spec,
        out_shape=jax.ShapeDtypeStruct((n_batch * seq_len, w), BF16),
        compiler_params=_cp(("parallel", "parallel", "arbitrary")),
    )(bases, var_id, q, k, v, k, v, bias)


def _ret_kernel(cdec_ref, qf_ref, kf_ref, vf_ref, qb_ref, kb_ref, vb_ref, dmask_ref, qdec_ref, kdec_ref,
                yf_ref, yb_ref, state_sc):
    @pl.when(pl.program_id(1) == 0)
    def _():
        state_sc[...] = jnp.zeros(state_sc.shape, F32)

    nt = (((1,), (1,)), ((), ()))
    tn = (((0,), (0,)), ((), ()))
    dk = RET_DK
    for d, (q_ref, k_ref, v_ref, y_ref) in enumerate(((qf_ref, kf_ref, vf_ref, yf_ref),
                                                      (qb_ref, kb_ref, vb_ref, yb_ref))):
        for h in range(RET_HEADS):
            sl = slice(h * dk, (h + 1) * dk)
            q = q_ref[:, sl]
            k = k_ref[:, sl]
            v = v_ref[:, sl]
            st = state_sc[d, h]
            a = lax.dot_general(q, k, nt, preferred_element_type=F32) * dmask_ref[d, h]
            inner = jnp.dot(a.astype(BF16), v, preferred_element_type=F32)
            cross = jnp.dot(q, st.astype(BF16), preferred_element_type=F32) * qdec_ref[d, h]
            y_ref[:, sl] = inner + cross
            vs = (v.astype(F32) * kdec_ref[d, h]).astype(BF16)
            state_sc[d, h] = st * cdec_ref[d * RET_HEADS + h] + lax.dot_general(k, vs, tn, preferred_element_type=F32)


def _retention(rq, rk, rv, dmask, qdec, kdec, cdec, n_batch, seq_len, ctx_len):
    n, w = rq.shape
    c = RET_CHUNK
    assert ctx_len == c and seq_len % c == 0
    ncl = seq_len // c
    ctx_blk0 = (n_batch * seq_len) // c

    def fwd(b, s, cd):
        return (jnp.where(s == 0, ctx_blk0 + b, b * ncl + s - 1), 0)

    def bwd(b, s, cd):
        return (jnp.where(s == 0, ctx_blk0 + b, b * ncl + ncl - s), 0)

    full = lambda a: pl.BlockSpec(a.shape, lambda b, s, cd: (0,) * a.ndim)
    grid_spec = pltpu.PrefetchScalarGridSpec(
        num_scalar_prefetch=1,
        grid=(n_batch, ncl + 1),
        in_specs=[pl.BlockSpec((c, w), fwd)] * 3 + [pl.BlockSpec((c, w), bwd)] * 3 + [full(dmask), full(qdec), full(kdec)],
        out_specs=[pl.BlockSpec((c, w), fwd), pl.BlockSpec((c, w), bwd)],
        scratch_shapes=[pltpu.VMEM((2, RET_HEADS, RET_DK, RET_DK), F32)],
    )
    return pl.pallas_call(
        _ret_kernel,
        grid_spec=grid_spec,
        out_shape=[jax.ShapeDtypeStruct((n, w), F32)] * 2,
        compiler_params=_cp(("parallel", "arbitrary")),
    )(cdec, rq, rk, rv, rq, rk, rv, dmask, qdec, kdec)


def _ret_finish_kernel(yf_ref, yb_ref, rg_ref, gn_ref, o_ref):
    y = yf_ref[...] + yb_ref[...]
    gate = rg_ref[...]
    gate = gate * jax.nn.sigmoid(gate)
    for h in range(RET_HEADS):
        sl = slice(h * RET_DK, (h + 1) * RET_DK)
        yh = y[:, sl]
        mu = jnp.mean(yh, axis=-1, keepdims=True)
        var = jnp.mean(jnp.square(yh - mu), axis=-1, keepdims=True)
        o_ref[:, sl] = ((yh - mu) * lax.rsqrt(var + GN_EPS) * gn_ref[:, sl] * gate[:, sl]).astype(o_ref.dtype)


def _ret_finish(yf, yb, rg, gn):
    n, w = yf.shape
    tm = ROW_TILE
    spec = pl.BlockSpec((tm, w), lambda t: (t, 0))
    return pl.pallas_call(
        _ret_finish_kernel,
        grid=(n // tm,),
        in_specs=[spec, spec, spec, pl.BlockSpec((1, w), lambda t: (0, 0))],
        out_specs=spec,
        out_shape=jax.ShapeDtypeStruct((n, w), BF16),
        compiler_params=_cp(("parallel",)),
    )(yf, yb, rg, gn)


def _out_kernel(a1_ref, a2_ref, x_ref, w_ref, g1_ref, ng_ref, sh_ref, sc_ref, wr_ref, br_ref,
                xo_ref, m_ref, e_ref, gt_ref):
    half = a1_ref.shape[1]
    o = (jnp.dot(a1_ref[...], w_ref[:half, :], preferred_element_type=F32)
         + jnp.dot(a2_ref[...], w_ref[half:, :], preferred_element_type=F32))
    x = x_ref[...] + g1_ref[0] * o
    xo_ref[...] = x
    m = _modulated_norm(x, ng_ref[...], sc_ref[0], sh_ref[0])
    _store_token_tiles(m_ref, m)
    m_hi = m.astype(BF16)
    m_lo = (m - m_hi.astype(F32)).astype(BF16)
    hi_prod = jnp.dot(m_hi, wr_ref[...], preferred_element_type=F32)
    logits = (hi_prod[:, :LANES] + hi_prod[:, LANES:]
              + jnp.dot(m_lo, wr_ref[:, :LANES], preferred_element_type=F32) + br_ref[...])
    lane = lax.broadcasted_iota(jnp.int32, logits.shape, 1).astype(F32)
    e_out = jnp.zeros(logits.shape, F32)
    g_out = jnp.zeros(logits.shape, F32)
    top0 = None
    denom = None
    for kk in range(TOP_K):
        mx = jnp.max(logits, axis=-1, keepdims=True)
        idx = jnp.min(jnp.where(logits == mx, lane, float(LANES)), axis=-1, keepdims=True)
        if kk == 0:
            top0 = mx
            ex = jnp.ones_like(mx)
            denom = ex
        else:
            ex = jnp.exp(mx - top0)
            denom = denom + ex
        e_out = jnp.where(lane == kk, idx, e_out)
        g_out = jnp.where(lane == kk, ex, g_out)
        logits = jnp.where(lane == idx, NEG * 2.0, logits)
    e_ref[...] = e_out.astype(jnp.int32)
    gt_ref[...] = g_out / denom


def _out_proj(a1, a2, x_all, w_out, mod_l, norm2_g, w_r, b_r, n_rows, tpb, n_batch):
    d = x_all.shape[1]
    half = a1.shape[1]
    tm = ROW_TILE
    n_lat_tiles = tpb * n_batch

    def mod_row(t):
        return jnp.where(t < n_lat_tiles, t // tpb, n_batch)

    row = lambda wd: pl.BlockSpec((tm, wd), lambda t: (t, 0))
    modspec = lambda col: pl.BlockSpec((1, 1, d), lambda t: (mod_row(t), 0, col))
    return pl.pallas_call(
        _out_kernel,
        grid=(n_rows // tm,),
        in_specs=[row(half), row(half), row(d), pl.BlockSpec((2 * half, d), lambda t: (0, 0)),
                  modspec(2), pl.BlockSpec((1, d), lambda t: (0, 0)), modspec(3), modspec(4),
                  pl.BlockSpec((d, 2 * LANES), lambda t: (0, 0)), pl.BlockSpec((1, LANES), lambda t: (0, 0))],
        out_specs=[row(d), pl.BlockSpec((tm * TOK_ROWS, LANES), lambda t: (t, 0)), row(LANES), row(LANES)],
        out_shape=[jax.ShapeDtypeStruct((n_rows, d), F32), jax.ShapeDtypeStruct((n_rows * TOK_ROWS, LANES), F32),
                   jax.ShapeDtypeStruct((n_rows, LANES), jnp.int32), jax.ShapeDtypeStruct((n_rows, LANES), F32)],
        compiler_params=_cp(("parallel",)),
    )(a1, a2, x_all, w_out, mod_l, norm2_g.reshape(1, d), mod_l, mod_l, w_r, b_r)


def _token_rows(ref, idx):
    return ref.at[pl.ds(pl.multiple_of(idx * TOK_ROWS, TOK_ROWS), TOK_ROWS)]


def _dispatch_kernel(pad_start_ref, pad_cnt_ref, pos_ref, m_ref, hs_ref, zero_sc, sem, pad_sem):
    n_tok = m_ref.shape[0] // TOK_ROWS

    @pl.when(pl.program_id(0) == 0)
    def _():
        zero_sc[...] = jnp.zeros(zero_sc.shape, zero_sc.dtype)

        def per_expert(e, carry):
            def fill(r, c):
                pltpu.make_async_copy(zero_sc, _token_rows(hs_ref, pad_start_ref[e] + r), pad_sem).start()
                return c

            lax.fori_loop(0, pad_cnt_ref[e], fill, 0)

            def fill_done(r, c):
                pltpu.make_async_copy(zero_sc, _token_rows(hs_ref, 0), pad_sem).wait()
                return c

            lax.fori_loop(0, pad_cnt_ref[e], fill_done, 0)
            return carry

        lax.fori_loop(0, N_EXPERTS, per_expert, 0)

    def issue(r, carry):
        for kk in range(TOP_K):
            pltpu.make_async_copy(_token_rows(m_ref, r), _token_rows(hs_ref, pos_ref[r * TOP_K + kk]),
                                  sem).start(priority=kk % 2)
        return carry

    lax.fori_loop(0, n_tok, issue, 0)

    def drain(r, carry):
        for kk in range(TOP_K):
            pltpu.make_async_copy(_token_rows(m_ref, 0), _token_rows(hs_ref, 0), sem).wait()
        return carry

    lax.fori_loop(0, n_tok, drain, 0)


def _dispatch(m_tiles, pos, pad_start, pad_cnt, cap):
    n_tok = m_tiles.shape[0] // TOK_ROWS
    tm = ROW_TILE
    grid_spec = pltpu.PrefetchScalarGridSpec(
        num_scalar_prefetch=2,
        grid=(n_tok // tm,),
        in_specs=[pl.BlockSpec((tm * TOP_K,), lambda t, ps, pc: (t,), memory_space=pltpu.SMEM),
                  pl.BlockSpec((tm * TOK_ROWS, LANES), lambda t, ps, pc: (t, 0))],
        out_specs=pl.BlockSpec(memory_space=pl.ANY),
        scratch_shapes=[pltpu.VMEM((TOK_ROWS, LANES), F32), pltpu.SemaphoreType.DMA(()), pltpu.SemaphoreType.DMA(())],
    )
    return pl.pallas_call(
        _dispatch_kernel,
        grid_spec=grid_spec,
        out_shape=jax.ShapeDtypeStruct((cap * TOK_ROWS, LANES), F32),
        compiler_params=pltpu.CompilerParams(dimension_semantics=("arbitrary",), has_side_effects=True,
                                             vmem_limit_bytes=VMEM_LIMIT),
    )(pad_start, pad_cnt, pos, m_tiles)


def _expert_kernel(be_ref, nu_ref, x_ref, w1_ref, b1_ref, w2_ref, b2_ref, o_ref, w1_sc, w2_sc):
    i = pl.program_id(0)

    @pl.when((i == 0) | (be_ref[i] != be_ref[jnp.maximum(i - 1, 0)]))
    def _():
        w1_sc[...] = w1_ref[0].astype(BF16)
        w2_sc[...] = w2_ref[0].astype(BF16)

    @pl.when(i < nu_ref[0])
    def _():
        dff = w2_ref.shape[1]
        x = _load_token_tiles(x_ref, MOE_BM).astype(BF16)
        u = jnp.dot(x, w1_sc[...], preferred_element_type=F32) + b1_ref[0]
        gl = jnp.minimum(u[:, :dff], SWIGLU_LIMIT)
        up = jnp.clip(u[:, dff:], -SWIGLU_LIMIT, SWIGLU_LIMIT)
        act = gl * jax.nn.sigmoid(SWIGLU_ALPHA * gl) * (up + 1.0)
        _store_token_tiles(o_ref, jnp.dot(act.astype(BF16), w2_sc[...], preferred_element_type=F32) + b2_ref[0])

    @pl.when(i >= nu_ref[0])
    def _():
        o_ref[...] = jnp.zeros(o_ref.shape, o_ref.dtype)


def _experts(hs, blk_e, n_used, w1_all, b1, w2_all, b2, layer):
    _, ne, d, dff2 = w1_all.shape
    cap = hs.shape[0] // TOK_ROWS
    dff = dff2 // 2
    bm = MOE_BM
    tile_spec = pl.BlockSpec((bm * TOK_ROWS, LANES), lambda i, be, nu: (i, 0))
    grid_spec = pltpu.PrefetchScalarGridSpec(
        num_scalar_prefetch=2,
        grid=(cap // bm,),
        in_specs=[tile_spec,
                  pl.BlockSpec((None, 1, d, dff2), lambda i, be, nu: (layer, be[i], 0, 0)),
                  pl.BlockSpec((1, 1, dff2), lambda i, be, nu: (be[i], 0, 0)),
                  pl.BlockSpec((None, 1, dff, d), lambda i, be, nu: (layer, be[i], 0, 0)),
                  pl.BlockSpec((1, 1, d), lambda i, be, nu: (be[i], 0, 0))],
        out_specs=tile_spec,
        scratch_shapes=[pltpu.VMEM((d, dff2), BF16), pltpu.VMEM((dff, d), BF16)],
    )
    return pl.pallas_call(
        _expert_kernel,
        grid_spec=grid_spec,
        out_shape=jax.ShapeDtypeStruct(hs.shape, F32),
        compiler_params=_cp(("arbitrary",)),
    )(blk_e, n_used, hs, w1_all, b1.reshape(ne, 1, dff2), w2_all, b2.reshape(ne, 1, d))


def _combine_kernel(pos_ref, pos_next_ref, x_ref, gt_ref, g2_ref, out_ref, o_ref, y_sc, sem):
    t = pl.program_id(0)
    n_tok = x_ref.shape[0]
    slot = t % 2

    def gather(p_ref, sl):
        def issue(r, carry):
            for kk in range(TOP_K):
                pltpu.make_async_copy(_token_rows(out_ref, p_ref[r * TOP_K + kk]),
                                      _token_rows(y_sc.at[sl * TOP_K + kk], r), sem.at[sl]).start()
            return carry

        lax.fori_loop(0, n_tok, issue, 0)

    @pl.when(t == 0)
    def _():
        gather(pos_ref, 0)

    @pl.when(t + 1 < pl.num_programs(0))
    def _():
        gather(pos_next_ref, 1 - slot)

    def drain(r, carry):
        for kk in range(TOP_K):
            pltpu.make_async_copy(_token_rows(out_ref, 0), _token_rows(y_sc.at[slot * TOP_K + kk], 0), sem.at[slot]).wait()
        return carry

    lax.fori_loop(0, n_tok, drain, 0)
    gt = gt_ref[...]
    for j in range(TOK_ROWS):
        sl = slice(j * LANES, (j + 1) * LANES)
        acc = y_sc[slot * TOP_K, pl.ds(j, n_tok, stride=TOK_ROWS), :] * gt[:, 0:1]
        for kk in range(1, TOP_K):
            acc = acc + y_sc[slot * TOP_K + kk, pl.ds(j, n_tok, stride=TOK_ROWS), :] * gt[:, kk:kk + 1]
        o_ref[:, sl] = x_ref[:, sl] + g2_ref[0, :, sl] * acc


def _combine(x, out_tiles, pos, gates, mod_l, tpb, n_batch):
    n, d = x.shape
    tm = ROW_TILE
    n_lat_tiles = tpb * n_batch

    def mod_row(t):
        return jnp.where(t < n_lat_tiles, t // tpb, n_batch)

    n_tiles = n // tm
    return pl.pallas_call(
        _combine_kernel,
        grid=(n_tiles,),
        in_specs=[pl.BlockSpec((tm * TOP_K,), lambda t: (t,), memory_space=pltpu.SMEM),
                  pl.BlockSpec((tm * TOP_K,), lambda t: (jnp.minimum(t + 1, n_tiles - 1),), memory_space=pltpu.SMEM),
                  pl.BlockSpec((tm, d), lambda t: (t, 0)), pl.BlockSpec((tm, LANES), lambda t: (t, 0)),
                  pl.BlockSpec((1, 1, d), lambda t: (mod_row(t), 0, 5)), pl.BlockSpec(memory_space=pl.ANY)],
        out_specs=pl.BlockSpec((tm, d), lambda t: (t, 0)),
        out_shape=jax.ShapeDtypeStruct((n, d), F32),
        scratch_shapes=[pltpu.VMEM((2 * TOP_K, tm * TOK_ROWS, LANES), F32), pltpu.SemaphoreType.DMA((2,))],
        compiler_params=_cp(("arbitrary",)),
    )(pos, pos, x, gates, mod_l, out_tiles)


def _moe(m_tiles, top_e, gates, x, mod_l, w1_all, b1, w2_all, b2, layer, tpb, n_batch):
    n, d = x.shape
    nk = n * TOP_K
    bm = MOE_BM
    n_blk = (nk + N_EXPERTS * (bm - 1)) // bm + 1
    cap = n_blk * bm
    flat_e = top_e[:, :TOP_K].reshape(nk)
    onehot = (flat_e[:, None] == jnp.arange(N_EXPERTS, dtype=jnp.int32)[None, :]).astype(jnp.int32)
    csum = jnp.cumsum(onehot, axis=0)
    rank = jnp.sum(jnp.where(onehot > 0, csum, 0), axis=1) - 1
    counts = csum[-1]
    padded = ((counts + bm - 1) // bm) * bm
    pend = jnp.cumsum(padded)
    pstart = pend - padded
    pos = (pstart[flat_e] + rank).astype(jnp.int32)
    n_used = (pend[-1] // bm).astype(jnp.int32)
    blk = jnp.minimum(jnp.arange(n_blk, dtype=jnp.int32), n_used - 1)
    blk_e = jnp.sum((pend[None, :] <= (blk * bm)[:, None]).astype(jnp.int32), axis=1)
    blk_e = jnp.clip(blk_e, 0, N_EXPERTS - 1).astype(jnp.int32)
    hs = _dispatch(m_tiles, pos, (pstart + counts).astype(jnp.int32), (padded - counts).astype(jnp.int32), cap)
    out = _experts(hs, blk_e, n_used.reshape(1), w1_all, b1, w2_all, b2, layer)
    return _combine(x, out, pos, gates, mod_l, tpb, n_batch)


def _rope_tables(seq_len, d_rot, reps, n_extra):
    t = jnp.arange(seq_len)
    rows = (t // GRID_W).astype(F32)
    cols = (t % GRID_W).astype(F32)
    n_freq = d_rot // 4
    inv = ROPE_THETA ** (-jnp.arange(n_freq, dtype=F32) / n_freq)
    ang = jnp.concatenate([rows[:, None] * inv, cols[:, None] * inv], axis=-1)
    cos = jnp.repeat(jnp.cos(ang), 2, axis=-1)
    sin = jnp.repeat(jnp.sin(ang), 2, axis=-1) * jnp.tile(jnp.asarray([-1.0, 1.0], F32), d_rot // 2)
    cos = jnp.concatenate([jnp.tile(cos, (1, reps)), jnp.ones((n_extra, d_rot * reps), F32)], axis=0)
    sin = jnp.concatenate([jnp.tile(sin, (1, reps)), jnp.zeros((n_extra, d_rot * reps), F32)], axis=0)
    return cos, sin


def _retention_tables(decay_logit):
    log_g = jax.nn.log_sigmoid(decay_logit.astype(F32))
    c = RET_CHUNK
    pos = jnp.arange(c, dtype=F32)
    diff = pos[:, None] - pos[None, :]
    lf = log_g[0][:, None, None]
    lb = log_g[1][:, None, None]
    dm_f = jnp.where(diff >= 0, jnp.exp(lf * jnp.where(diff >= 0, diff, 0.0)), 0.0)
    dm_b = jnp.where(diff < 0, jnp.exp(lb * jnp.where(diff < 0, -diff, 0.0)), 0.0)
    qd_f = jnp.exp(log_g[0][:, None] * (pos + 1.0))
    qd_b = jnp.exp(log_g[1][:, None] * (c - pos))
    kd_f = jnp.exp(log_g[0][:, None] * (c - 1.0 - pos))
    kd_b = jnp.exp(log_g[1][:, None] * pos)
    bc = lambda a: jnp.broadcast_to(a[..., None], a.shape + (LANES,))
    dmask = jnp.stack([dm_f, dm_b])
    qdec = jnp.stack([bc(qd_f), bc(qd_b)])
    kdec = jnp.stack([bc(kd_f), bc(kd_b)])
    cdec = jnp.exp(log_g * c).reshape(-1)
    return dmask, qdec, kdec, cdec


def _heads_major(t, n_batch, length, n_heads):
    return t.reshape(n_batch, length, n_heads, -1).transpose(0, 2, 1, 3)


def _heads_t(t, n_batch, length, n_heads):
    return t.reshape(n_batch, length, n_heads, -1).transpose(0, 2, 3, 1)


def kernel(x, c, ctx, c_ctx, norm1_g, norm2_g, w_mod, b_mod, w_in_even, w_out_even, a_q_norm, a_k_norm, b_q_norm, b_k_norm, b_rpb, w_in_odd, w_out_odd, ret_decay, ret_gn, mla_cq_norm, mla_ckv_norm, w_uq, w_ukv, mla_q_norm, mla_k_norm, w_router, b_router, w_exp1, b_exp1, w_exp2, b_exp2):
    bsz, s, d = x.shape
    cl = ctx.shape[1]
    depth = w_mod.shape[0]
    tm = ROW_TILE
    assert s % tm == 0 and (bsz * cl) == tm and s % NA_QB == 0 and bsz + 1 <= 8 and d == TOK_ROWS * LANES
    tpb = s // tm
    nl = bsz * s
    n_all = nl + bsz * cl

    c_rows = jnp.zeros((8, d), F32).at[:bsz].set(c).at[bsz].set(c_ctx)
    mod = _mod_vectors(c_rows, w_mod, b_mod)

    cos_a, sin_a = _rope_tables(s, HEAD_DIM, LANES // HEAD_DIM, tm)
    cos_c, sin_c = _rope_tables(s, RET_DK, 1, tm)
    cos_d, sin_d = _rope_tables(s, MLA_ROPE, LANES // MLA_ROPE, tm)
    na_bases, na_var, na_row_sel, na_col_in, na_dc_onehot = _na_tables(s)

    x_all = jnp.concatenate([x.reshape(nl, d), ctx.reshape(bsz * cl, d)], axis=0)
    lat = lambda t: t[:nl]
    cx = lambda t: t[nl:]
    att_scale = HEAD_DIM ** -0.5 * LOG2E
    grp = GQA_HEADS // GQA_KV_HEADS
    bkv = bsz * GQA_KV_HEADS

    for l in range(depth):
        need_ctx = l < depth - 1
        i = l // 2
        mod_l = mod[l].reshape(8, 1, 6 * d)
        if l % 2 == 0:
            wq, wk, wv, wn = GQA_HEADS * HEAD_DIM, GQA_KV_HEADS * HEAD_DIM, GQA_KV_HEADS * HEAD_DIM, NA_HEADS * HEAD_DIM
            starts = np.cumsum([0, wq, wk, wv, wn, wn])
            segs = [(int(starts[0]), wq, "norm_rope", 1.0), (int(starts[1]), wk, "norm_rope", 1.0),
                    (int(starts[2]), wv, "plain", 1.0), (int(starts[3]), wn, "norm", 1.0),
                    (int(starts[4]), wn, "norm", 1.0), (int(starts[5]), wn, "plain", 1.0)]
            gain = jnp.concatenate([jnp.tile(a_q_norm[i], GQA_HEADS) * att_scale, jnp.tile(a_k_norm[i], GQA_KV_HEADS),
                                    jnp.ones((wv,), F32), jnp.tile(b_q_norm[i], NA_HEADS) * att_scale,
                                    jnp.tile(b_k_norm[i], NA_HEADS), jnp.ones((wn,), F32)]).reshape(1, -1)
            qa, ka, va, qb, kb, vb = _proj(x_all, mod_l, norm1_g[l], w_in_even[i].astype(BF16), gain, cos_a, sin_a,
                                           segs, [BF16] * 6, tpb, bsz)
            k_lat = _heads_major(lat(ka), bsz, s, GQA_KV_HEADS)
            k_cx = _heads_major(cx(ka), bsz, cl, GQA_KV_HEADS)
            v_lat = _heads_major(lat(va), bsz, s, GQA_KV_HEADS)
            v_cx = _heads_major(cx(va), bsz, cl, GQA_KV_HEADS)
            k_all = jnp.concatenate([k_lat, k_cx], axis=2).reshape(bkv, s + cl, HEAD_DIM)
            v_all = jnp.concatenate([v_lat, v_cx], axis=2).reshape(bkv, s + cl, HEAD_DIM)
            qt = _heads_t(lat(qa), bsz, s, GQA_HEADS).reshape(bkv, grp, HEAD_DIM, s)
            oa_t = _flash(qt, k_all, v_all, Q_SUB)
            oa = oa_t.reshape(bsz, GQA_HEADS, HEAD_DIM, s).transpose(0, 3, 1, 2).reshape(nl, wq)
            bias = _na_bias(b_rpb[i], na_row_sel, na_col_in, na_dc_onehot)
            bh = bsz * NA_HEADS
            ob = _na_attention(qb, kb, vb, bias, jnp.asarray(na_bases), jnp.asarray(na_var), bsz, s, cl)
            kn_c = _heads_major(cx(kb), bsz, cl, NA_HEADS).reshape(bh, cl, HEAD_DIM)
            vn_c = _heads_major(cx(vb), bsz, cl, NA_HEADS).reshape(bh, cl, HEAD_DIM)
            if need_ctx:
                qt_c = _heads_t(cx(qa), bsz, cl, GQA_HEADS).reshape(bkv, grp, HEAD_DIM, cl)
                oa_c = _flash(qt_c, k_cx.reshape(bkv, cl, HEAD_DIM), v_cx.reshape(bkv, cl, HEAD_DIM), Q_SUB)
                oa_c = oa_c.reshape(bsz, GQA_HEADS, HEAD_DIM, cl).transpose(0, 3, 1, 2).reshape(bsz * cl, wq)
                qnt_c = _heads_t(cx(qb), bsz, cl, NA_HEADS).reshape(bh, 1, HEAD_DIM, cl)
                ob_c = _flash(qnt_c, kn_c, vn_c, Q_SUB)
                ob_c = ob_c.reshape(bsz, NA_HEADS, HEAD_DIM, cl).transpose(0, 3, 1, 2).reshape(bsz * cl, wn)
                a1 = jnp.concatenate([oa, oa_c], axis=0)
                a2 = jnp.concatenate([ob, ob_c], axis=0)
            else:
                a1, a2 = oa, ob
            w_out = w_out_even[i].astype(BF16)
        else:
            rw = RET_HEADS * RET_DK
            kr_cols = w_in_odd[i][:, 4 * rw + MLA_Q_LORA + MLA_KV_LORA:]
            w_ext = jnp.concatenate([w_in_odd[i]] + [kr_cols] * (LANES * 2 // MLA_ROPE - 1), axis=1).astype(BF16)
            mla_w = MLA_Q_LORA + MLA_KV_LORA + 2 * LANES
            segs = [(0, rw, "rope", RET_DK ** -0.5), (rw, rw, "rope", 1.0), (2 * rw, rw, "plain", 1.0),
                    (3 * rw, rw, "plain", 1.0), (4 * rw, mla_w, "plain", 1.0)]
            gain = jnp.ones((1, w_ext.shape[1]), F32)
            rq, rk, rv, rg, mla_in = _proj(x_all, mod_l, norm1_g[l], w_ext, gain, cos_c, sin_c, segs,
                                           [BF16, BF16, BF16, F32, F32], tpb, bsz)
            dmask, qdec, kdec, cdec = _retention_tables(ret_decay[i])
            yf, yb = _retention(rq, rk, rv, dmask, qdec, kdec, cdec, bsz, s, cl)
            a1 = _ret_finish(yf, yb, rg, ret_gn[i].reshape(1, rw))
            dqk = MLA_NOPE + MLA_ROPE
            perm_q = np.concatenate([np.arange(h * dqk, h * dqk + MLA_NOPE) for h in range(MLA_HEADS)]
                                    + [np.arange(h * dqk + MLA_NOPE, (h + 1) * dqk) for h in range(MLA_HEADS)])
            dkv = MLA_NOPE + MLA_V
            perm_kv = np.concatenate([np.arange(h * dkv, h * dkv + MLA_NOPE) for h in range(MLA_HEADS)]
                                     + [np.arange(h * dkv + MLA_NOPE, (h + 1) * dkv) for h in range(MLA_HEADS)])
            mla_scale = dqk ** -0.5 * LOG2E
            gq = jnp.concatenate([jnp.tile(mla_q_norm[i][:MLA_NOPE], MLA_HEADS),
                                  jnp.tile(mla_q_norm[i][MLA_NOPE:], MLA_HEADS)]).reshape(1, -1) * mla_scale
            gkn = mla_k_norm[i][:MLA_NOPE].reshape(1, -1)
            gkr = jnp.tile(mla_k_norm[i][MLA_NOPE:], LANES // MLA_ROPE).reshape(1, -1)
            q_m, k_m, v_m = _mla_proj(mla_in, w_uq[i][:, perm_q].astype(BF16), w_ukv[i][:, perm_kv].astype(BF16),
                                      mla_cq_norm[i].reshape(1, -1), mla_ckv_norm[i].reshape(1, -1), gq, gkn, gkr,
                                      cos_d, sin_d, tpb, bsz)
            nw = MLA_HEADS * MLA_NOPE

            def qk_heads(t, length, transposed):
                nope = t[:, :nw].reshape(bsz, length, MLA_HEADS, MLA_NOPE)
                rope = t[:, nw:].reshape(bsz, length, MLA_HEADS, MLA_ROPE)
                full = jnp.concatenate([nope, rope], axis=-1)
                return full.transpose(0, 2, 3, 1) if transposed else full.transpose(0, 2, 1, 3)

            bhm = bsz * MLA_HEADS
            k_lat = qk_heads(lat(k_m), s, False)
            k_cx = qk_heads(cx(k_m), cl, False)
            v_lat = _heads_major(lat(v_m), bsz, s, MLA_HEADS)
            v_cx = _heads_major(cx(v_m), bsz, cl, MLA_HEADS)
            k_all = jnp.concatenate([k_lat, k_cx], axis=2).reshape(bhm, s + cl, dqk)
            v_all = jnp.concatenate([v_lat, v_cx], axis=2).reshape(bhm, s + cl, MLA_V)
            qt = qk_heads(lat(q_m), s, True).reshape(bhm, 1, dqk, s)
            om = _flash(qt, k_all, v_all, 4 * Q_SUB)
            om = om.reshape(bsz, MLA_HEADS, MLA_V, s).transpose(0, 3, 1, 2).reshape(nl, MLA_HEADS * MLA_V)
            if need_ctx:
                qt_c = qk_heads(cx(q_m), cl, True).reshape(bhm, 1, dqk, cl)
                om_c = _flash(qt_c, k_cx.reshape(bhm, cl, dqk), v_cx.reshape(bhm, cl, MLA_V), Q_SUB)
                om_c = om_c.reshape(bsz, MLA_HEADS, MLA_V, cl).transpose(0, 3, 1, 2).reshape(bsz * cl, MLA_HEADS * MLA_V)
                a2 = jnp.concatenate([om, om_c], axis=0)
            else:
                a1 = a1[:nl]
                a2 = om
            w_out = w_out_odd[i].astype(BF16)

        n_rows = n_all if need_ctx else nl
        w_r_hi32 = lax.reduce_precision(w_router[l].astype(F32), exponent_bits=8, mantissa_bits=7)
        w_r_hi = w_r_hi32.astype(BF16)
        w_r_lo = (w_router[l] - w_r_hi32).astype(BF16)
        w_r = (jnp.zeros((d, 2 * LANES), BF16).at[:, :N_EXPERTS].set(w_r_hi)
               .at[:, LANES:LANES + N_EXPERTS].set(w_r_lo))
        b_r = jnp.full((1, LANES), NEG, F32).at[0, :N_EXPERTS].set(b_router[l])
        x_new, m, top_e, gates = _out_proj(a1, a2, x_all, w_out, mod_l, norm2_g[l], w_r, b_r, n_rows, tpb, bsz)
        x_all = _moe(m, top_e, gates, x_new, mod_l, w_exp1, b_exp1[l], w_exp2, b_exp2[l], l, tpb, bsz)
    return x_all[:nl].reshape(bsz, s, d)
```

```python
import functools
import math

import numpy as np
import jax
import jax.numpy as jnp
from jax import lax
from jax.experimental import pallas as pl
from jax.experimental.pallas import tpu as pltpu

F32 = jnp.float32
BF16 = jnp.bfloat16

GRID_W = 64
HEAD_DIM = 64
GQA_HEADS = 8
GQA_KV_HEADS = 2
NA_HEADS = 8
NA_ROWS = 8
NA_COLS = 16
RET_HEADS = 4
RET_DK = 128
MLA_HEADS = 4
MLA_Q_LORA = 256
MLA_KV_LORA = 128
MLA_NOPE = 128
MLA_ROPE = 64
MLA_V = 128
N_EXPERTS = 32
TOP_K = 4
SWIGLU_LIMIT = 7.0
SWIGLU_ALPHA = 1.702
ROPE_THETA = 10000.0
EPS = 1e-6
GN_EPS = 1e-5
LOG2E = math.log2(math.e)
NEG = -1e30

LANES = 128
ROW_TILE = 512
Q_SUB = 256
FLASH_TILES = 2
ONES_ROWS = 16
NA_QB = 256
NA_WIN_ROWS = 12
RET_CHUNK = 256
MOE_BM = 512
TOK_ROWS = 8
VMEM_LIMIT = 56 * 1024 * 1024


def _cp(sem):
    return pltpu.CompilerParams(dimension_semantics=sem, vmem_limit_bytes=VMEM_LIMIT)


def _mod_kernel(c_ref, w_ref, b_ref, o_ref):
    c = c_ref[...]
    s = c * jax.nn.sigmoid(c)
    o_ref[0] = jnp.dot(s, w_ref[0], precision=lax.Precision.HIGHEST, preferred_element_type=F32) + b_ref[0]


def _mod_vectors(c_rows, w_mod, b_mod):
    depth, d, d6 = w_mod.shape
    tn = 1536
    return pl.pallas_call(
        _mod_kernel,
        grid=(depth, d6 // tn),
        in_specs=[pl.BlockSpec((8, d), lambda l, j: (0, 0)),
                  pl.BlockSpec((1, d, tn), lambda l, j: (l, 0, j)),
                  pl.BlockSpec((1, 1, tn), lambda l, j: (l, 0, j))],
        out_specs=pl.BlockSpec((1, 8, tn), lambda l, j: (l, 0, j)),
        out_shape=jax.ShapeDtypeStruct((depth, 8, d6), F32),
        compiler_params=_cp(("parallel", "parallel")),
    )(c_rows, w_mod, b_mod.reshape(depth, 1, d6))


def _modulated_norm(x, g, sc, sh):
    ms = jnp.mean(x * x, axis=-1, keepdims=True)
    return x * lax.rsqrt(ms + EPS) * g * (1.0 + sc) + sh


def _pair_rope(y, cos, sin_signed):
    lane = lax.broadcasted_iota(jnp.int32, y.shape, 1)
    partner = jnp.where((lane & 1) == 0, pltpu.roll(y, LANES - 1, 1), pltpu.roll(y, 1, 1))
    return y * cos + partner * sin_signed


def _store_token_tiles(ref, val):
    n = val.shape[0]
    for j in range(TOK_ROWS):
        ref[pl.ds(j, n, stride=TOK_ROWS), :] = val[:, j * LANES:(j + 1) * LANES]


def _load_token_tiles(ref, n):
    return jnp.concatenate([ref[pl.ds(j, n, stride=TOK_ROWS), :] for j in range(TOK_ROWS)], axis=-1)


def _split_dot(a_f32, w_bf16):
    hi = a_f32.astype(BF16)
    lo = (a_f32 - hi.astype(F32)).astype(BF16)
    return (jnp.dot(hi, w_bf16, preferred_element_type=F32) + jnp.dot(lo, w_bf16, preferred_element_type=F32))


def _proj_kernel(x_ref, sh_ref, sc_ref, g_ref, w_ref, gain_ref, bd_ref, cos_ref, sin_ref, *out_refs, segs):
    a = _modulated_norm(x_ref[...], g_ref[...], sc_ref[0], sh_ref[0]).astype(BF16)
    cos = cos_ref[...]
    sin = sin_ref[...]
    for (start, width, mode, scale), o_ref in zip(segs, out_refs):
        y_seg = jnp.dot(a, w_ref[:, start:start + width], preferred_element_type=F32)
        if mode == "plain":
            o_ref[...] = y_seg.astype(o_ref.dtype)
            continue
        for j in range(width // LANES):
            y = y_seg[:, j * LANES:(j + 1) * LANES]
            if "norm" in mode:
                ms = _split_dot(y * y, bd_ref[...])
                y = y * lax.rsqrt(ms + EPS) * gain_ref[:, start + j * LANES:start + (j + 1) * LANES]
            if scale != 1.0:
                y = y * scale
            if "rope" in mode:
                y = _pair_rope(y, cos, sin)
            o_ref[:, j * LANES:(j + 1) * LANES] = y.astype(o_ref.dtype)


def _proj(x_all, mod_l, norm_g, w, gain, cos_t, sin_t, segs, out_dtypes, n_lat_tiles_per_batch, n_batch):
    n, d = x_all.shape
    tm = ROW_TILE
    n_tiles = n // tm
    wtot = w.shape[1]
    tpb = n_lat_tiles_per_batch
    n_lat_tiles = tpb * n_batch

    def mod_row(t):
        return jnp.where(t < n_lat_tiles, t // tpb, n_batch)

    def rope_row(t):
        return jnp.where(t < n_lat_tiles, t % tpb, tpb)

    bd = np.kron(np.eye(2, dtype=np.float32), np.full((HEAD_DIM, HEAD_DIM), 1.0 / HEAD_DIM, np.float32))
    in_specs = [
        pl.BlockSpec((tm, d), lambda t: (t, 0)),
        pl.BlockSpec((1, 1, d), lambda t: (mod_row(t), 0, 0)),
        pl.BlockSpec((1, 1, d), lambda t: (mod_row(t), 0, 1)),
        pl.BlockSpec((1, d), lambda t: (0, 0)),
        pl.BlockSpec((d, wtot), lambda t: (0, 0)),
        pl.BlockSpec((1, wtot), lambda t: (0, 0)),
        pl.BlockSpec((LANES, LANES), lambda t: (0, 0)),
        pl.BlockSpec((tm, LANES), lambda t: (rope_row(t), 0)),
        pl.BlockSpec((tm, LANES), lambda t: (rope_row(t), 0)),
    ]
    out_specs = [pl.BlockSpec((tm, s[1]), lambda t: (t, 0)) for s in segs]
    out_shape = [jax.ShapeDtypeStruct((n, s[1]), dt) for s, dt in zip(segs, out_dtypes)]
    return pl.pallas_call(
        functools.partial(_proj_kernel, segs=tuple(segs)),
        grid=(n_tiles,),
        in_specs=in_specs,
        out_specs=out_specs,
        out_shape=out_shape,
        compiler_params=_cp(("parallel",)),
    )(x_all, mod_l, mod_l, norm_g.reshape(1, d), w, gain, jnp.asarray(bd, BF16), cos_t, sin_t)


def _mla_proj_kernel(x_ref, wuq_ref, wukv_ref, gcq_ref, gckv_ref, gq_ref, gkn_ref, gkr_ref, cos_ref, sin_ref,
                     q_ref, k_ref, v_ref):
    x = x_ref[...]
    cq = x[:, :MLA_Q_LORA]
    ckv = x[:, MLA_Q_LORA:MLA_Q_LORA + MLA_KV_LORA]
    kr = x[:, MLA_Q_LORA + MLA_KV_LORA:MLA_Q_LORA + MLA_KV_LORA + LANES]
    cos = cos_ref[...]
    sin = sin_ref[...]
    cqn = cq * lax.rsqrt(jnp.mean(cq * cq, axis=-1, keepdims=True) + EPS) * gcq_ref[...]
    ckvn = ckv * lax.rsqrt(jnp.mean(ckv * ckv, axis=-1, keepdims=True) + EPS) * gckv_ref[...]
    q = jnp.dot(cqn.astype(BF16), wuq_ref[...], preferred_element_type=F32)
    kv = jnp.dot(ckvn.astype(BF16), wukv_ref[...], preferred_element_type=F32)
    nh = MLA_HEADS
    d_qk = float(MLA_NOPE + MLA_ROPE)
    lane = lax.broadcasted_iota(jnp.int32, (1, LANES), 1)
    low = lane < MLA_ROPE

    def half_sums(slab):
        sq = slab * slab
        a = jnp.sum(jnp.where(low, sq, 0.0), axis=-1, keepdims=True)
        return a, jnp.sum(sq, axis=-1, keepdims=True) - a

    rope_w = nh * MLA_NOPE
    q_rope_ss = []
    for r in range(nh // 2):
        q_rope_ss.extend(half_sums(q[:, rope_w + r * LANES:rope_w + (r + 1) * LANES]))
    kr_ss, _ = half_sums(kr)
    rs_q, rs_k = [], []
    for h in range(nh):
        qn = q[:, h * MLA_NOPE:(h + 1) * MLA_NOPE]
        kn = kv[:, h * MLA_NOPE:(h + 1) * MLA_NOPE]
        rs_q.append(lax.rsqrt((jnp.sum(qn * qn, axis=-1, keepdims=True) + q_rope_ss[h]) / d_qk + EPS))
        rs_k.append(lax.rsqrt((jnp.sum(kn * kn, axis=-1, keepdims=True) + kr_ss) / d_qk + EPS))
        q_ref[:, h * MLA_NOPE:(h + 1) * MLA_NOPE] = (qn * rs_q[h] * gq_ref[:, h * MLA_NOPE:(h + 1) * MLA_NOPE]).astype(q_ref.dtype)
        k_ref[:, h * MLA_NOPE:(h + 1) * MLA_NOPE] = (kn * rs_k[h] * gkn_ref[...]).astype(k_ref.dtype)
    kr_rot = _pair_rope(kr * gkr_ref[...], cos, sin)
    for r in range(nh // 2):
        sl = slice(rope_w + r * LANES, rope_w + (r + 1) * LANES)
        yq = q[:, sl] * jnp.where(low, rs_q[2 * r], rs_q[2 * r + 1]) * gq_ref[:, sl]
        q_ref[:, sl] = _pair_rope(yq, cos, sin).astype(q_ref.dtype)
        k_ref[:, sl] = (kr_rot * jnp.where(low, rs_k[2 * r], rs_k[2 * r + 1])).astype(k_ref.dtype)
    v_ref[...] = kv[:, nh * MLA_NOPE:].astype(v_ref.dtype)


def _mla_proj(mla_in, wuq, wukv, gcq, gckv, gq, gkn, gkr, cos_t, sin_t, tpb, n_batch):
    n, win = mla_in.shape
    tm = ROW_TILE
    n_lat_tiles = tpb * n_batch

    def rope_row(t):
        return jnp.where(t < n_lat_tiles, t % tpb, tpb)

    qk_w = MLA_HEADS * (MLA_NOPE + MLA_ROPE)
    v_w = MLA_HEADS * MLA_V
    full = lambda a: pl.BlockSpec(a.shape, lambda t: (0,) * a.ndim)
    return pl.pallas_call(
        _mla_proj_kernel,
        grid=(n // tm,),
        in_specs=[pl.BlockSpec((tm, win), lambda t: (t, 0)), full(wuq), full(wukv), full(gcq), full(gckv),
                  full(gq), full(gkn), full(gkr),
                  pl.BlockSpec((tm, LANES), lambda t: (rope_row(t), 0)),
                  pl.BlockSpec((tm, LANES), lambda t: (rope_row(t), 0))],
        out_specs=[pl.BlockSpec((tm, qk_w), lambda t: (t, 0)), pl.BlockSpec((tm, qk_w), lambda t: (t, 0)),
                   pl.BlockSpec((tm, v_w), lambda t: (t, 0))],
        out_shape=[jax.ShapeDtypeStruct((n, qk_w), BF16), jax.ShapeDtypeStruct((n, qk_w), BF16),
                   jax.ShapeDtypeStruct((n, v_w), BF16)],
        compiler_params=_cp(("parallel",)),
    )(mla_in, wuq, wukv, gcq, gckv, gq, gkn, gkr, cos_t, sin_t)


def _flash_kernel(qt_ref, k_ref, vt_ref, ot_ref, q_sc, s_0, s_1, s_2, mx_0, mx_1, mx_2, m_sc, acc_sc, *,
                  g, n_chunks, dv, n_tiles):
    bufs = ((s_0, mx_0), (s_1, mx_1), (s_2, mx_2))
    tq = qt_ref.shape[3] // n_tiles
    for t in range(n_tiles):
        for gi in range(g):
            q_sc[t, :, gi * tq:(gi + 1) * tq] = qt_ref[0, gi, :, t * tq:(t + 1) * tq]

    def reset():
        m_sc[...] = jnp.full(m_sc.shape, NEG, F32)
        acc_sc[...] = jnp.zeros(acc_sc.shape, F32)

    def finalize(t):
        acc = acc_sc[...]
        o = acc[:dv] / acc[dv:dv + 1]
        for gi in range(g):
            ot_ref[0, gi, :, t * tq:(t + 1) * tq] = o[:, gi * tq:(gi + 1) * tq].astype(ot_ref.dtype)

    def scores(t, ci, s_ref, mx_ref):
        s = jnp.dot(k_ref[0, ci], q_sc[t], preferred_element_type=F32)
        s_ref[...] = s
        mx_ref[...] = jnp.max(s, axis=0, keepdims=True)

    def accumulate(ci, s_ref, mx_ref):
        m_old = m_sc[...]
        m_new = jnp.maximum(m_old, mx_ref[...])
        alpha = jnp.exp2(m_old - m_new)
        p = jnp.exp2(s_ref[...] - m_new).astype(BF16)
        acc_sc[...] = alpha * acc_sc[...] + jnp.dot(vt_ref[0, ci], p, preferred_element_type=F32)
        m_sc[...] = m_new

    def fused(t_n, ci_n, s_n, mx_n, ci_c, s_c, mx_c):
        m_old = m_sc[...]
        m_new = jnp.maximum(m_old, mx_c[...])
        alpha = jnp.exp2(m_old - m_new)
        pv = None
        mx = None
        for k0 in range(0, s_c.shape[0], 256):
            s = jnp.dot(k_ref[0, ci_n, k0:k0 + 256, :], q_sc[t_n], preferred_element_type=F32)
            s_n[k0:k0 + 256, :] = s
            mxj = jnp.max(s, axis=0, keepdims=True)
            mx = mxj if mx is None else jnp.maximum(mx, mxj)
            p = jnp.exp2(s_c[k0:k0 + 256, :] - m_new).astype(BF16)
            part = jnp.dot(vt_ref[0, ci_c, :, k0:k0 + 256], p, preferred_element_type=F32)
            pv = part if pv is None else pv + part
        mx_n[...] = mx
        acc_sc[...] = alpha * acc_sc[...] + pv
        m_sc[...] = m_new

    reset()
    if n_chunks == 1:
        for t in range(n_tiles):
            scores(t, 0, *bufs[0])
            accumulate(0, *bufs[0])
            finalize(t)
            reset()
    else:
        scores(0, 0, *bufs[0])
        scores(0, 1, *bufs[1])
        n_fused = n_chunks - 2
        for t in range(n_tiles):
            off = t * n_chunks

            def triple(j, carry, t=t, off=off):
                c = 3 * j
                for r in range(3):
                    fused(t, c + r + 2, *bufs[(r + 2 + off) % 3], c + r, *bufs[(r + off) % 3])
                return carry

            lax.fori_loop(0, n_fused // 3, triple, 0)
            for c in range(3 * (n_fused // 3), n_fused):
                fused(t, c + 2, *bufs[(c + 2 + off) % 3], c, *bufs[(c + off) % 3])
            for c in (n_chunks - 2, n_chunks - 1):
                if t + 1 < n_tiles:
                    c_n = c - (n_chunks - 2)
                    fused(t + 1, c_n, *bufs[(c_n + off + n_chunks) % 3], c, *bufs[(c + off) % 3])
                else:
                    accumulate(c, *bufs[(c + off) % 3])
            finalize(t)
            reset()


def _key_chunk(lk):
    for tk in (1280, 1024, 768, 512, 256):
        if lk % tk == 0:
            return tk
    raise ValueError(f"key length {lk} must be a multiple of 256")


def _flash(qt, k, v, tq_blk):
    bk, g, dq, lq = qt.shape
    lk, dv = v.shape[1], v.shape[2]
    tk = _key_chunk(lk)
    nch = lk // tk
    kc = k.reshape(bk, nch, tk, dq)
    dve = dv + ONES_ROWS
    vt = jnp.concatenate([v, jnp.ones((bk, lk, ONES_ROWS), v.dtype)], axis=-1)
    vt = vt.reshape(bk, nch, tk, dve).transpose(0, 1, 3, 2)
    tq_blk = min(tq_blk, lq)
    n_tiles = FLASH_TILES if (nch > 1 and lq % (FLASH_TILES * tq_blk) == 0) else 1
    assert lq % tq_blk == 0 and tq_blk % LANES == 0
    w = g * tq_blk
    blk = n_tiles * tq_blk
    return pl.pallas_call(
        functools.partial(_flash_kernel, g=g, n_chunks=nch, dv=dv, n_tiles=n_tiles),
        grid=(bk, lq // blk),
        in_specs=[pl.BlockSpec((1, g, dq, blk), lambda b, i: (b, 0, 0, i)),
                  pl.BlockSpec((1, nch, tk, dq), lambda b, i: (b, 0, 0, 0)),
                  pl.BlockSpec((1, nch, dve, tk), lambda b, i: (b, 0, 0, 0))],
        out_specs=pl.BlockSpec((1, g, dv, blk), lambda b, i: (b, 0, 0, i)),
        out_shape=jax.ShapeDtypeStruct((bk, g, dv, lq), BF16),
        scratch_shapes=[pltpu.VMEM((n_tiles, dq, w), BF16)] + [pltpu.VMEM((tk, w), F32)] * 3
                       + [pltpu.VMEM((1, w), F32)] * 4 + [pltpu.VMEM((dve, w), F32)],
        compiler_params=_cp(("parallel", "parallel")),
    )(qt, kc, vt)


def _na_tables(seq_len):
    rows_n = seq_len // GRID_W
    assert rows_n >= NA_WIN_ROWS and NA_ROWS <= rows_n
    nb = seq_len // NA_QB
    rpq = NA_QB // GRID_W
    band = NA_WIN_ROWS * GRID_W
    variants, var_id, bases = {}, [], []
    for j in range(nb):
        base = int(np.clip(rpq * j - NA_ROWS // 2, 0, rows_n - NA_WIN_ROWS))
        bases.append(base)
        t = np.arange(NA_QB) + j * NA_QB
        r, col = t // GRID_W, t % GRID_W
        r0 = np.clip(r - NA_ROWS // 2, 0, rows_n - NA_ROWS)
        c0 = np.clip(col - NA_COLS // 2, 0, GRID_W - NA_COLS)
        kk = np.arange(band)
        kr = base + kk // GRID_W
        kc = kk % GRID_W
        inside = ((kr[None] >= r0[:, None]) & (kr[None] < r0[:, None] + NA_ROWS)
                  & (kc[None] >= c0[:, None]) & (kc[None] < c0[:, None] + NA_COLS))
        rel = (kr[None] - r[:, None] + NA_ROWS - 1) * (2 * NA_COLS - 1) + (kc[None] - col[:, None] + NA_COLS - 1)
        tab = np.where(inside, rel, -1).astype(np.int32)
        assert (inside.sum(axis=1) == NA_ROWS * NA_COLS).all()
        key = tab.tobytes()
        if key not in variants:
            variants[key] = (len(variants), tab)
        var_id.append(variants[key][0])
    tabs = np.stack([v[1] for v in sorted(variants.values(), key=lambda kv: kv[0])])
    n_dr = 2 * NA_ROWS - 1
    n_dc = 2 * NA_COLS - 1
    col = np.arange(GRID_W)
    c0 = np.clip(col - NA_COLS // 2, 0, GRID_W - NA_COLS)
    col_in = (col[None] >= c0[:, None]) & (col[None] < c0[:, None] + NA_COLS)
    dc = col[None] - col[:, None] + NA_COLS - 1
    t5 = tabs.reshape(len(tabs), rpq, GRID_W, NA_WIN_ROWS, GRID_W)
    row_sel = np.full((len(tabs), rpq, NA_WIN_ROWS), n_dr, np.int32)
    for v in range(len(tabs)):
        for a in range(rpq):
            for i in range(NA_WIN_ROWS):
                blk = t5[v, a, :, i, :]
                if (blk >= 0).any():
                    dr = int(blk[blk >= 0][0]) // n_dc
                    assert (np.where(col_in, dr * n_dc + dc, -1) == blk).all()
                    row_sel[v, a, i] = dr
                else:
                    assert (blk < 0).all()
    dc_onehot = (dc[None] == np.arange(n_dc)[:, None, None]).astype(np.float32)
    return np.asarray(bases, np.int32), np.asarray(var_id, np.int32), row_sel, col_in, dc_onehot


def _na_bias(rpb, row_sel, col_in, dc_onehot):
    h = rpb.shape[0]
    n_dr, n_dc = 2 * NA_ROWS - 1, 2 * NA_COLS - 1
    t = jnp.einsum("hdj,jck->hdck", rpb.astype(F32).reshape(h, n_dr, n_dc) * LOG2E, jnp.asarray(dc_onehot),
                   precision=lax.Precision.HIGHEST)
    t = jnp.where(jnp.asarray(col_in)[None, None], t, NEG)
    t = jnp.concatenate([t, jnp.full((h, 1, GRID_W, GRID_W), NEG, F32)], axis=1)
    nv, rpq, nw = row_sel.shape
    b = t[:, jnp.asarray(row_sel)]
    return b.transpose(1, 0, 2, 4, 3, 5).reshape(nv, h, rpq * GRID_W, nw * GRID_W)


def _na_kernel(base_ref, var_ref, q_ref, k_ref, v_ref, kc_ref, vc_ref, bias_ref, o_ref):
    j = pl.program_id(2)
    band = NA_WIN_ROWS * GRID_W
    start = pl.multiple_of(base_ref[j] * GRID_W, GRID_W)
    q = q_ref[...]
    kw = k_ref[pl.ds(start, band), :]
    vw = v_ref[pl.ds(start, band), :]
    kc = kc_ref[...]
    vc = vc_ref[...]
    nt = (((1,), (1,)), ((), ()))
    lane = lax.broadcasted_iota(jnp.int32, q.shape, 1)
    outs = []
    for hh in range(LANES // HEAD_DIM):
        mine = (lane >= hh * HEAD_DIM) & (lane < (hh + 1) * HEAD_DIM)
        qh = jnp.where(mine, q, jnp.zeros_like(q))
        s_win = lax.dot_general(qh, kw, nt, preferred_element_type=F32) + bias_ref[0, hh]
        s_ctx = lax.dot_general(qh, kc, nt, preferred_element_type=F32)
        m = jnp.maximum(jnp.max(s_win, axis=-1, keepdims=True), jnp.max(s_ctx, axis=-1, keepdims=True))
        p_win = jnp.exp2(s_win - m)
        p_ctx = jnp.exp2(s_ctx - m)
        l = jnp.sum(p_win, axis=-1, keepdims=True) + jnp.sum(p_ctx, axis=-1, keepdims=True)
        o = (jnp.dot(p_win.astype(BF16), vw, preferred_element_type=F32)
             + jnp.dot(p_ctx.astype(BF16), vc, preferred_element_type=F32))
        outs.append(o / l)
    o_ref[...] = jnp.where(lane < HEAD_DIM, outs[0], outs[1]).astype(o_ref.dtype)


def _na_attention(q, k, v, bias, bases, var_id, n_batch, seq_len, ctx_len):
    w = q.shape[1]
    assert LANES // HEAD_DIM == 2 and ctx_len % 8 == 0 and (n_batch * seq_len) % ctx_len == 0
    band = NA_WIN_ROWS * GRID_W
    nb = seq_len // NA_QB
    ctx0 = (n_batch * seq_len) // ctx_len
    lat_spec = pl.BlockSpec((seq_len, LANES), lambda b, p, j, bs, vr: (b, p))
    ctx_spec = pl.BlockSpec((ctx_len, LANES), lambda b, p, j, bs, vr: (ctx0 + b, p))
    grid_spec = pltpu.PrefetchScalarGridSpec(
        num_scalar_prefetch=2,
        grid=(n_batch, w // LANES, nb),
        in_specs=[pl.BlockSpec((NA_QB, LANES), lambda b, p, j, bs, vr: (b * nb + j, p)),
                  lat_spec, lat_spec, ctx_spec, ctx_spec,
                  pl.BlockSpec((1, LANES // HEAD_DIM, NA_QB, band), lambda b, p, j, bs, vr: (vr[j], p, 0, 0))],
        out_specs=pl.BlockSpec((NA_QB, LANES), lambda b, p, j, bs, vr: (b * nb + j, p)),
    )
    return pl.pallas_call(
        _na_kernel,
        grid_spec=grid_spec,
        out_shape=jax.ShapeDtypeStruct((n_batch * seq_len, w), BF16),
        compiler_params=_cp(("parallel", "parallel", "arbitrary")),
    )(bases, var_id, q, k, v, k, v, bias)


def _ret_kernel(cdec_ref, qf_ref, kf_ref, vf_ref, qb_ref, kb_ref, vb_ref, dmask_ref, qdec_ref, kdec_ref,
                yf_ref, yb_ref, state_sc):
    @pl.when(pl.program_id(1) == 0)
    def _():
        state_sc[...] = jnp.zeros(state_sc.shape, F32)

    nt = (((1,), (1,)), ((), ()))
    tn = (((0,), (0,)), ((), ()))
    dk = RET_DK
    for d, (q_ref, k_ref, v_ref, y_ref) in enumerate(((qf_ref, kf_ref, vf_ref, yf_ref),
                                                      (qb_ref, kb_ref, vb_ref, yb_ref))):
        for h in range(RET_HEADS):
            sl = slice(h * dk, (h + 1) * dk)
            q = q_ref[:, sl]
            k = k_ref[:, sl]
            v = v_ref[:, sl]
            st = state_sc[d, h]
            a = lax.dot_general(q, k, nt, preferred_element_type=F32) * dmask_ref[d, h]
            inner = jnp.dot(a.astype(BF16), v, preferred_element_type=F32)
            cross = jnp.dot(q, st.astype(BF16), preferred_element_type=F32) * qdec_ref[d, h]
            y_ref[:, sl] = inner + cross
            vs = (v.astype(F32) * kdec_ref[d, h]).astype(BF16)
            state_sc[d, h] = st * cdec_ref[d * RET_HEADS + h] + lax.dot_general(k, vs, tn, preferred_element_type=F32)


def _retention(rq, rk, rv, dmask, qdec, kdec, cdec, n_batch, seq_len, ctx_len):
    n, w = rq.shape
    c = RET_CHUNK
    assert ctx_len == c and seq_len % c == 0
    ncl = seq_len // c
    ctx_blk0 = (n_batch * seq_len) // c

    def fwd(b, s, cd):
        return (jnp.where(s == 0, ctx_blk0 + b, b * ncl + s - 1), 0)

    def bwd(b, s, cd):
        return (jnp.where(s == 0, ctx_blk0 + b, b * ncl + ncl - s), 0)

    full = lambda a: pl.BlockSpec(a.shape, lambda b, s, cd: (0,) * a.ndim)
    grid_spec = pltpu.PrefetchScalarGridSpec(
        num_scalar_prefetch=1,
        grid=(n_batch, ncl + 1),
        in_specs=[pl.BlockSpec((c, w), fwd)] * 3 + [pl.BlockSpec((c, w), bwd)] * 3 + [full(dmask), full(qdec), full(kdec)],
        out_specs=[pl.BlockSpec((c, w), fwd), pl.BlockSpec((c, w), bwd)],
        scratch_shapes=[pltpu.VMEM((2, RET_HEADS, RET_DK, RET_DK), F32)],
    )
    return pl.pallas_call(
        _ret_kernel,
        grid_spec=grid_spec,
        out_shape=[jax.ShapeDtypeStruct((n, w), F32)] * 2,
        compiler_params=_cp(("parallel", "arbitrary")),
    )(cdec, rq, rk, rv, rq, rk, rv, dmask, qdec, kdec)


def _ret_finish_kernel(yf_ref, yb_ref, rg_ref, gn_ref, o_ref):
    y = yf_ref[...] + yb_ref[...]
    gate = rg_ref[...]
    gate = gate * jax.nn.sigmoid(gate)
    for h in range(RET_HEADS):
        sl = slice(h * RET_DK, (h + 1) * RET_DK)
        yh = y[:, sl]
        mu = jnp.mean(yh, axis=-1, keepdims=True)
        var = jnp.mean(jnp.square(yh - mu), axis=-1, keepdims=True)
        o_ref[:, sl] = ((yh - mu) * lax.rsqrt(var + GN_EPS) * gn_ref[:, sl] * gate[:, sl]).astype(o_ref.dtype)


def _ret_finish(yf, yb, rg, gn):
    n, w = yf.shape
    tm = ROW_TILE
    spec = pl.BlockSpec((tm, w), lambda t: (t, 0))
    return pl.pallas_call(
        _ret_finish_kernel,
        grid=(n // tm,),
        in_specs=[spec, spec, spec, pl.BlockSpec((1, w), lambda t: (0, 0))],
        out_specs=spec,
        out_shape=jax.ShapeDtypeStruct((n, w), BF16),
        compiler_params=_cp(("parallel",)),
    )(yf, yb, rg, gn)


def _out_kernel(a1_ref, a2_ref, x_ref, w_ref, g1_ref, ng_ref, sh_ref, sc_ref, wr_ref, br_ref,
                xo_ref, m_ref, e_ref, gt_ref):
    half = a1_ref.shape[1]
    o = (jnp.dot(a1_ref[...], w_ref[:half, :], preferred_element_type=F32)
         + jnp.dot(a2_ref[...], w_ref[half:, :], preferred_element_type=F32))
    x = x_ref[...] + g1_ref[0] * o
    xo_ref[...] = x
    m = _modulated_norm(x, ng_ref[...], sc_ref[0], sh_ref[0])
    _store_token_tiles(m_ref, m)
    m_hi = m.astype(BF16)
    m_lo = (m - m_hi.astype(F32)).astype(BF16)
    hi_prod = jnp.dot(m_hi, wr_ref[...], preferred_element_type=F32)
    logits = (hi_prod[:, :LANES] + hi_prod[:, LANES:]
              + jnp.dot(m_lo, wr_ref[:, :LANES], preferred_element_type=F32) + br_ref[...])
    lane = lax.broadcasted_iota(jnp.int32, logits.shape, 1).astype(F32)
    e_out = jnp.zeros(logits.shape, F32)
    g_out = jnp.zeros(logits.shape, F32)
    top0 = None
    denom = None
    for kk in range(TOP_K):
        mx = jnp.max(logits, axis=-1, keepdims=True)
        idx = jnp.min(jnp.where(logits == mx, lane, float(LANES)), axis=-1, keepdims=True)
        if kk == 0:
            top0 = mx
            ex = jnp.ones_like(mx)
            denom = ex
        else:
            ex = jnp.exp(mx - top0)
            denom = denom + ex
        e_out = jnp.where(lane == kk, idx, e_out)
        g_out = jnp.where(lane == kk, ex, g_out)
        logits = jnp.where(lane == idx, NEG * 2.0, logits)
    e_ref[...] = e_out.astype(jnp.int32)
    gt_ref[...] = g_out / denom


def _out_proj(a1, a2, x_all, w_out, mod_l, norm2_g, w_r, b_r, n_rows, tpb, n_batch):
    d = x_all.shape[1]
    half = a1.shape[1]
    tm = ROW_TILE
    n_lat_tiles = tpb * n_batch

    def mod_row(t):
        return jnp.where(t < n_lat_tiles, t // tpb, n_batch)

    row = lambda wd: pl.BlockSpec((tm, wd), lambda t: (t, 0))
    modspec = lambda col: pl.BlockSpec((1, 1, d), lambda t: (mod_row(t), 0, col))
    return pl.pallas_call(
        _out_kernel,
        grid=(n_rows // tm,),
        in_specs=[row(half), row(half), row(d), pl.BlockSpec((2 * half, d), lambda t: (0, 0)),
                  modspec(2), pl.BlockSpec((1, d), lambda t: (0, 0)), modspec(3), modspec(4),
                  pl.BlockSpec((d, 2 * LANES), lambda t: (0, 0)), pl.BlockSpec((1, LANES), lambda t: (0, 0))],
        out_specs=[row(d), pl.BlockSpec((tm * TOK_ROWS, LANES), lambda t: (t, 0)), row(LANES), row(LANES)],
        out_shape=[jax.ShapeDtypeStruct((n_rows, d), F32), jax.ShapeDtypeStruct((n_rows * TOK_ROWS, LANES), F32),
                   jax.ShapeDtypeStruct((n_rows, LANES), jnp.int32), jax.ShapeDtypeStruct((n_rows, LANES), F32)],
        compiler_params=_cp(("parallel",)),
    )(a1, a2, x_all, w_out, mod_l, norm2_g.reshape(1, d), mod_l, mod_l, w_r, b_r)


def _token_rows(ref, idx):
    return ref.at[pl.ds(pl.multiple_of(idx * TOK_ROWS, TOK_ROWS), TOK_ROWS)]


def _dispatch_kernel(pad_start_ref, pad_cnt_ref, pos_ref, m_ref, hs_ref, zero_sc, sem, pad_sem):
    n_tok = m_ref.shape[0] // TOK_ROWS

    @pl.when(pl.program_id(0) == 0)
    def _():
        zero_sc[...] = jnp.zeros(zero_sc.shape, zero_sc.dtype)

        def per_expert(e, carry):
            def fill(r, c):
                pltpu.make_async_copy(zero_sc, _token_rows(hs_ref, pad_start_ref[e] + r), pad_sem).start()
                return c

            lax.fori_loop(0, pad_cnt_ref[e], fill, 0)

            def fill_done(r, c):
                pltpu.make_async_copy(zero_sc, _token_rows(hs_ref, 0), pad_sem).wait()
                return c

            lax.fori_loop(0, pad_cnt_ref[e], fill_done, 0)
            return carry

        lax.fori_loop(0, N_EXPERTS, per_expert, 0)

    def issue(r, carry):
        for kk in range(TOP_K):
            pltpu.make_async_copy(_token_rows(m_ref, r), _token_rows(hs_ref, pos_ref[r * TOP_K + kk]),
                                  sem).start(priority=kk % 2)
        return carry

    lax.fori_loop(0, n_tok, issue, 0)

    for kk in range(TOP_K):
        pltpu.make_async_copy(m_ref, hs_ref.at[pl.ds(0, n_tok * TOK_ROWS)], sem).wait()


def _dispatch(m_tiles, pos, pad_start, pad_cnt, cap):
    n_tok = m_tiles.shape[0] // TOK_ROWS
    tm = ROW_TILE
    grid_spec = pltpu.PrefetchScalarGridSpec(
        num_scalar_prefetch=2,
        grid=(n_tok // tm,),
        in_specs=[pl.BlockSpec((tm * TOP_K,), lambda t, ps, pc: (t,), memory_space=pltpu.SMEM),
                  pl.BlockSpec((tm * TOK_ROWS, LANES), lambda t, ps, pc: (t, 0))],
        out_specs=pl.BlockSpec(memory_space=pl.ANY),
        scratch_shapes=[pltpu.VMEM((TOK_ROWS, LANES), F32), pltpu.SemaphoreType.DMA(()), pltpu.SemaphoreType.DMA(())],
    )
    return pl.pallas_call(
        _dispatch_kernel,
        grid_spec=grid_spec,
        out_shape=jax.ShapeDtypeStruct((cap * TOK_ROWS, LANES), F32),
        compiler_params=pltpu.CompilerParams(dimension_semantics=("arbitrary",), has_side_effects=True,
                                             vmem_limit_bytes=VMEM_LIMIT),
    )(pad_start, pad_cnt, pos, m_tiles)


def _expert_kernel(be_ref, nu_ref, x_ref, w1_ref, b1_ref, w2_ref, b2_ref, o_ref, w1_sc, w2_sc):
    i = pl.program_id(0)

    @pl.when((i == 0) | (be_ref[i] != be_ref[jnp.maximum(i - 1, 0)]))
    def _():
        w1_sc[...] = w1_ref[0].astype(BF16)
        w2_sc[...] = w2_ref[0].astype(BF16)

    @pl.when(i < nu_ref[0])
    def _():
        dff = w2_ref.shape[1]
        x = _load_token_tiles(x_ref, MOE_BM).astype(BF16)
        u = jnp.dot(x, w1_sc[...], preferred_element_type=F32) + b1_ref[0]
        gl = jnp.minimum(u[:, :dff], SWIGLU_LIMIT)
        up = jnp.clip(u[:, dff:], -SWIGLU_LIMIT, SWIGLU_LIMIT)
        act = gl * jax.nn.sigmoid(SWIGLU_ALPHA * gl) * (up + 1.0)
        _store_token_tiles(o_ref, jnp.dot(act.astype(BF16), w2_sc[...], preferred_element_type=F32) + b2_ref[0])

    @pl.when(i >= nu_ref[0])
    def _():
        o_ref[...] = jnp.zeros(o_ref.shape, o_ref.dtype)


def _experts(hs, blk_e, n_used, w1_all, b1, w2_all, b2, layer):
    _, ne, d, dff2 = w1_all.shape
    cap = hs.shape[0] // TOK_ROWS
    dff = dff2 // 2
    bm = MOE_BM
    tile_spec = pl.BlockSpec((bm * TOK_ROWS, LANES), lambda i, be, nu: (i, 0))
    grid_spec = pltpu.PrefetchScalarGridSpec(
        num_scalar_prefetch=2,
        grid=(cap // bm,),
        in_specs=[tile_spec,
                  pl.BlockSpec((None, 1, d, dff2), lambda i, be, nu: (layer, be[i], 0, 0)),
                  pl.BlockSpec((1, 1, dff2), lambda i, be, nu: (be[i], 0, 0)),
                  pl.BlockSpec((None, 1, dff, d), lambda i, be, nu: (layer, be[i], 0, 0)),
                  pl.BlockSpec((1, 1, d), lambda i, be, nu: (be[i], 0, 0))],
        out_specs=tile_spec,
        scratch_shapes=[pltpu.VMEM((d, dff2), BF16), pltpu.VMEM((dff, d), BF16)],
    )
    return pl.pallas_call(
        _expert_kernel,
        grid_spec=grid_spec,
        out_shape=jax.ShapeDtypeStruct(hs.shape, F32),
        compiler_params=_cp(("arbitrary",)),
    )(blk_e, n_used, hs, w1_all, b1.reshape(ne, 1, dff2), w2_all, b2.reshape(ne, 1, d))


def _combine_kernel(pos_ref, pos_next_ref, x_ref, gt_ref, g2_ref, out_ref, o_ref, y_sc, sem):
    t = pl.program_id(0)
    n_tok = x_ref.shape[0]
    slot = t % 2

    def gather(p_ref, sl):
        def issue(r, carry):
            for kk in range(TOP_K):
                pltpu.make_async_copy(_token_rows(out_ref, p_ref[r * TOP_K + kk]),
                                      _token_rows(y_sc.at[sl * TOP_K + kk], r), sem.at[sl]).start()
            return carry

        lax.fori_loop(0, n_tok, issue, 0)

    @pl.when(t == 0)
    def _():
        gather(pos_ref, 0)

    @pl.when(t + 1 < pl.num_programs(0))
    def _():
        gather(pos_next_ref, 1 - slot)

    for kk in range(TOP_K):
        pltpu.make_async_copy(out_ref.at[pl.ds(0, n_tok * TOK_ROWS)], y_sc.at[slot * TOP_K + kk], sem.at[slot]).wait()
    gt = gt_ref[...]
    for j in range(TOK_ROWS):
        sl = slice(j * LANES, (j + 1) * LANES)
        acc = y_sc[slot * TOP_K, pl.ds(j, n_tok, stride=TOK_ROWS), :] * gt[:, 0:1]
        for kk in range(1, TOP_K):
            acc = acc + y_sc[slot * TOP_K + kk, pl.ds(j, n_tok, stride=TOK_ROWS), :] * gt[:, kk:kk + 1]
        o_ref[:, sl] = x_ref[:, sl] + g2_ref[0, :, sl] * acc


def _combine(x, out_tiles, pos, gates, mod_l, tpb, n_batch):
    n, d = x.shape
    tm = ROW_TILE
    n_lat_tiles = tpb * n_batch

    def mod_row(t):
        return jnp.where(t < n_lat_tiles, t // tpb, n_batch)

    n_tiles = n // tm
    return pl.pallas_call(
        _combine_kernel,
        grid=(n_tiles,),
        in_specs=[pl.BlockSpec((tm * TOP_K,), lambda t: (t,), memory_space=pltpu.SMEM),
                  pl.BlockSpec((tm * TOP_K,), lambda t: (jnp.minimum(t + 1, n_tiles - 1),), memory_space=pltpu.SMEM),
                  pl.BlockSpec((tm, d), lambda t: (t, 0)), pl.BlockSpec((tm, LANES), lambda t: (t, 0)),
                  pl.BlockSpec((1, 1, d), lambda t: (mod_row(t), 0, 5)), pl.BlockSpec(memory_space=pl.ANY)],
        out_specs=pl.BlockSpec((tm, d), lambda t: (t, 0)),
        out_shape=jax.ShapeDtypeStruct((n, d), F32),
        scratch_shapes=[pltpu.VMEM((2 * TOP_K, tm * TOK_ROWS, LANES), F32), pltpu.SemaphoreType.DMA((2,))],
        compiler_params=_cp(("arbitrary",)),
    )(pos, pos, x, gates, mod_l, out_tiles)


def _moe(m_tiles, top_e, gates, x, mod_l, w1_all, b1, w2_all, b2, layer, tpb, n_batch):
    n, d = x.shape
    nk = n * TOP_K
    bm = MOE_BM
    n_blk = (nk + N_EXPERTS * (bm - 1)) // bm + 1
    cap = n_blk * bm
    flat_e = top_e[:, :TOP_K].reshape(nk)
    onehot = (flat_e[:, None] == jnp.arange(N_EXPERTS, dtype=jnp.int32)[None, :]).astype(jnp.int32)
    csum = jnp.cumsum(onehot, axis=0)
    rank = jnp.sum(jnp.where(onehot > 0, csum, 0), axis=1) - 1
    counts = csum[-1]
    padded = ((counts + bm - 1) // bm) * bm
    pend = jnp.cumsum(padded)
    pstart = pend - padded
    pos = (pstart[flat_e] + rank).astype(jnp.int32)
    n_used = (pend[-1] // bm).astype(jnp.int32)
    blk = jnp.minimum(jnp.arange(n_blk, dtype=jnp.int32), n_used - 1)
    blk_e = jnp.sum((pend[None, :] <= (blk * bm)[:, None]).astype(jnp.int32), axis=1)
    blk_e = jnp.clip(blk_e, 0, N_EXPERTS - 1).astype(jnp.int32)
    hs = _dispatch(m_tiles, pos, (pstart + counts).astype(jnp.int32), (padded - counts).astype(jnp.int32), cap)
    out = _experts(hs, blk_e, n_used.reshape(1), w1_all, b1, w2_all, b2, layer)
    return _combine(x, out, pos, gates, mod_l, tpb, n_batch)


def _rope_tables(seq_len, d_rot, reps, n_extra):
    t = jnp.arange(seq_len)
    rows = (t // GRID_W).astype(F32)
    cols = (t % GRID_W).astype(F32)
    n_freq = d_rot // 4
    inv = ROPE_THETA ** (-jnp.arange(n_freq, dtype=F32) / n_freq)
    ang = jnp.concatenate([rows[:, None] * inv, cols[:, None] * inv], axis=-1)
    cos = jnp.repeat(jnp.cos(ang), 2, axis=-1)
    sin = jnp.repeat(jnp.sin(ang), 2, axis=-1) * jnp.tile(jnp.asarray([-1.0, 1.0], F32), d_rot // 2)
    cos = jnp.concatenate([jnp.tile(cos, (1, reps)), jnp.ones((n_extra, d_rot * reps), F32)], axis=0)
    sin = jnp.concatenate([jnp.tile(sin, (1, reps)), jnp.zeros((n_extra, d_rot * reps), F32)], axis=0)
    return cos, sin


def _retention_tables(decay_logit):
    log_g = jax.nn.log_sigmoid(decay_logit.astype(F32))
    c = RET_CHUNK
    pos = jnp.arange(c, dtype=F32)
    diff = pos[:, None] - pos[None, :]
    lf = log_g[0][:, None, None]
    lb = log_g[1][:, None, None]
    dm_f = jnp.where(diff >= 0, jnp.exp(lf * jnp.where(diff >= 0, diff, 0.0)), 0.0)
    dm_b = jnp.where(diff < 0, jnp.exp(lb * jnp.where(diff < 0, -diff, 0.0)), 0.0)
    qd_f = jnp.exp(log_g[0][:, None] * (pos + 1.0))
    qd_b = jnp.exp(log_g[1][:, None] * (c - pos))
    kd_f = jnp.exp(log_g[0][:, None] * (c - 1.0 - pos))
    kd_b = jnp.exp(log_g[1][:, None] * pos)
    bc = lambda a: jnp.broadcast_to(a[..., None], a.shape + (LANES,))
    dmask = jnp.stack([dm_f, dm_b])
    qdec = jnp.stack([bc(qd_f), bc(qd_b)])
    kdec = jnp.stack([bc(kd_f), bc(kd_b)])
    cdec = jnp.exp(log_g * c).reshape(-1)
    return dmask, qdec, kdec, cdec


def _heads_major(t, n_batch, length, n_heads):
    return t.reshape(n_batch, length, n_heads, -1).transpose(0, 2, 1, 3)


def _heads_t(t, n_batch, length, n_heads):
    return t.reshape(n_batch, length, n_heads, -1).transpose(0, 2, 3, 1)


def kernel(x, c, ctx, c_ctx, norm1_g, norm2_g, w_mod, b_mod, w_in_even, w_out_even, a_q_norm, a_k_norm, b_q_norm, b_k_norm, b_rpb, w_in_odd, w_out_odd, ret_decay, ret_gn, mla_cq_norm, mla_ckv_norm, w_uq, w_ukv, mla_q_norm, mla_k_norm, w_router, b_router, w_exp1, b_exp1, w_exp2, b_exp2):
    bsz, s, d = x.shape
    cl = ctx.shape[1]
    depth = w_mod.shape[0]
    tm = ROW_TILE
    assert s % tm == 0 and (bsz * cl) == tm and s % NA_QB == 0 and bsz + 1 <= 8 and d == TOK_ROWS * LANES
    tpb = s // tm
    nl = bsz * s
    n_all = nl + bsz * cl

    c_rows = jnp.zeros((8, d), F32).at[:bsz].set(c).at[bsz].set(c_ctx)
    mod = _mod_vectors(c_rows, w_mod, b_mod)

    cos_a, sin_a = _rope_tables(s, HEAD_DIM, LANES // HEAD_DIM, tm)
    cos_c, sin_c = _rope_tables(s, RET_DK, 1, tm)
    cos_d, sin_d = _rope_tables(s, MLA_ROPE, LANES // MLA_ROPE, tm)
    na_bases, na_var, na_row_sel, na_col_in, na_dc_onehot = _na_tables(s)

    x_all = jnp.concatenate([x.reshape(nl, d), ctx.reshape(bsz * cl, d)], axis=0)
    lat = lambda t: t[:nl]
    cx = lambda t: t[nl:]
    att_scale = HEAD_DIM ** -0.5 * LOG2E
    grp = GQA_HEADS // GQA_KV_HEADS
    bkv = bsz * GQA_KV_HEADS

    for l in range(depth):
        need_ctx = l < depth - 1
        i = l // 2
        mod_l = mod[l].reshape(8, 1, 6 * d)
        if l % 2 == 0:
            wq, wk, wv, wn = GQA_HEADS * HEAD_DIM, GQA_KV_HEADS * HEAD_DIM, GQA_KV_HEADS * HEAD_DIM, NA_HEADS * HEAD_DIM
            starts = np.cumsum([0, wq, wk, wv, wn, wn])
            segs = [(int(starts[0]), wq, "norm_rope", 1.0), (int(starts[1]), wk, "norm_rope", 1.0),
                    (int(starts[2]), wv, "plain", 1.0), (int(starts[3]), wn, "norm", 1.0),
                    (int(starts[4]), wn, "norm", 1.0), (int(starts[5]), wn, "plain", 1.0)]
            gain = jnp.concatenate([jnp.tile(a_q_norm[i], GQA_HEADS) * att_scale, jnp.tile(a_k_norm[i], GQA_KV_HEADS),
                                    jnp.ones((wv,), F32), jnp.tile(b_q_norm[i], NA_HEADS) * att_scale,
                                    jnp.tile(b_k_norm[i], NA_HEADS), jnp.ones((wn,), F32)]).reshape(1, -1)
            qa, ka, va, qb, kb, vb = _proj(x_all, mod_l, norm1_g[l], w_in_even[i].astype(BF16), gain, cos_a, sin_a,
                                           segs, [BF16] * 6, tpb, bsz)
            k_lat = _heads_major(lat(ka), bsz, s, GQA_KV_HEADS)
            k_cx = _heads_major(cx(ka), bsz, cl, GQA_KV_HEADS)
            v_lat = _heads_major(lat(va), bsz, s, GQA_KV_HEADS)
            v_cx = _heads_major(cx(va), bsz, cl, GQA_KV_HEADS)
            k_all = jnp.concatenate([k_lat, k_cx], axis=2).reshape(bkv, s + cl, HEAD_DIM)
            v_all = jnp.concatenate([v_lat, v_cx], axis=2).reshape(bkv, s + cl, HEAD_DIM)
            qt = _heads_t(lat(qa), bsz, s, GQA_HEADS).reshape(bkv, grp, HEAD_DIM, s)
            oa_t = _flash(qt, k_all, v_all, Q_SUB)
            oa = oa_t.reshape(bsz, GQA_HEADS, HEAD_DIM, s).transpose(0, 3, 1, 2).reshape(nl, wq)
            bias = _na_bias(b_rpb[i], na_row_sel, na_col_in, na_dc_onehot)
            bh = bsz * NA_HEADS
            ob = _na_attention(qb, kb, vb, bias, jnp.asarray(na_bases), jnp.asarray(na_var), bsz, s, cl)
            kn_c = _heads_major(cx(kb), bsz, cl, NA_HEADS).reshape(bh, cl, HEAD_DIM)
            vn_c = _heads_major(cx(vb), bsz, cl, NA_HEADS).reshape(bh, cl, HEAD_DIM)
            if need_ctx:
                qt_c = _heads_t(cx(qa), bsz, cl, GQA_HEADS).reshape(bkv, grp, HEAD_DIM, cl)
                oa_c = _flash(qt_c, k_cx.reshape(bkv, cl, HEAD_DIM), v_cx.reshape(bkv, cl, HEAD_DIM), Q_SUB)
                oa_c = oa_c.reshape(bsz, GQA_HEADS, HEAD_DIM, cl).transpose(0, 3, 1, 2).reshape(bsz * cl, wq)
                qnt_c = _heads_t(cx(qb), bsz, cl, NA_HEADS).reshape(bh, 1, HEAD_DIM, cl)
                ob_c = _flash(qnt_c, kn_c, vn_c, Q_SUB)
                ob_c = ob_c.reshape(bsz, NA_HEADS, HEAD_DIM, cl).transpose(0, 3, 1, 2).reshape(bsz * cl, wn)
                a1 = jnp.concatenate([oa, oa_c], axis=0)
                a2 = jnp.concatenate([ob, ob_c], axis=0)
            else:
                a1, a2 = oa, ob
            w_out = w_out_even[i].astype(BF16)
        else:
            rw = RET_HEADS * RET_DK
            kr_cols = w_in_odd[i][:, 4 * rw + MLA_Q_LORA + MLA_KV_LORA:]
            w_ext = jnp.concatenate([w_in_odd[i]] + [kr_cols] * (LANES * 2 // MLA_ROPE - 1), axis=1).astype(BF16)
            mla_w = MLA_Q_LORA + MLA_KV_LORA + 2 * LANES
            segs = [(0, rw, "rope", RET_DK ** -0.5), (rw, rw, "rope", 1.0), (2 * rw, rw, "plain", 1.0),
                    (3 * rw, rw, "plain", 1.0), (4 * rw, mla_w, "plain", 1.0)]
            gain = jnp.ones((1, w_ext.shape[1]), F32)
            rq, rk, rv, rg, mla_in = _proj(x_all, mod_l, norm1_g[l], w_ext, gain, cos_c, sin_c, segs,
                                           [BF16, BF16, BF16, F32, F32], tpb, bsz)
            dmask, qdec, kdec, cdec = _retention_tables(ret_decay[i])
            yf, yb = _retention(rq, rk, rv, dmask, qdec, kdec, cdec, bsz, s, cl)
            a1 = _ret_finish(yf, yb, rg, ret_gn[i].reshape(1, rw))
            dqk = MLA_NOPE + MLA_ROPE
            perm_q = np.concatenate([np.arange(h * dqk, h * dqk + MLA_NOPE) for h in range(MLA_HEADS)]
                                    + [np.arange(h * dqk + MLA_NOPE, (h + 1) * dqk) for h in range(MLA_HEADS)])
            dkv = MLA_NOPE + MLA_V
            perm_kv = np.concatenate([np.arange(h * dkv, h * dkv + MLA_NOPE) for h in range(MLA_HEADS)]
                                     + [np.arange(h * dkv + MLA_NOPE, (h + 1) * dkv) for h in range(MLA_HEADS)])
            mla_scale = dqk ** -0.5 * LOG2E
            gq = jnp.concatenate([jnp.tile(mla_q_norm[i][:MLA_NOPE], MLA_HEADS),
                                  jnp.tile(mla_q_norm[i][MLA_NOPE:], MLA_HEADS)]).reshape(1, -1) * mla_scale
            gkn = mla_k_norm[i][:MLA_NOPE].reshape(1, -1)
            gkr = jnp.tile(mla_k_norm[i][MLA_NOPE:], LANES // MLA_ROPE).reshape(1, -1)
            q_m, k_m, v_m = _mla_proj(mla_in, w_uq[i][:, perm_q].astype(BF16), w_ukv[i][:, perm_kv].astype(BF16),
                                      mla_cq_norm[i].reshape(1, -1), mla_ckv_norm[i].reshape(1, -1), gq, gkn, gkr,
                                      cos_d, sin_d, tpb, bsz)
            nw = MLA_HEADS * MLA_NOPE

            def qk_heads(t, length, transposed):
                nope = t[:, :nw].reshape(bsz, length, MLA_HEADS, MLA_NOPE)
                rope = t[:, nw:].reshape(bsz, length, MLA_HEADS, MLA_ROPE)
                full = jnp.concatenate([nope, rope], axis=-1)
                return full.transpose(0, 2, 3, 1) if transposed else full.transpose(0, 2, 1, 3)

            bhm = bsz * MLA_HEADS
            k_lat = qk_heads(lat(k_m), s, False)
            k_cx = qk_heads(cx(k_m), cl, False)
            v_lat = _heads_major(lat(v_m), bsz, s, MLA_HEADS)
            v_cx = _heads_major(cx(v_m), bsz, cl, MLA_HEADS)
            k_all = jnp.concatenate([k_lat, k_cx], axis=2).reshape(bhm, s + cl, dqk)
            v_all = jnp.concatenate([v_lat, v_cx], axis=2).reshape(bhm, s + cl, MLA_V)
            qt = qk_heads(lat(q_m), s, True).reshape(bhm, 1, dqk, s)
            om = _flash(qt, k_all, v_all, 4 * Q_SUB)
            om = om.reshape(bsz, MLA_HEADS, MLA_V, s).transpose(0, 3, 1, 2).reshape(nl, MLA_HEADS * MLA_V)
            if need_ctx:
                qt_c = qk_heads(cx(q_m), cl, True).reshape(bhm, 1, dqk, cl)
                om_c = _flash(qt_c, k_cx.reshape(bhm, cl, dqk), v_cx.reshape(bhm, cl, MLA_V), Q_SUB)
                om_c = om_c.reshape(bsz, MLA_HEADS, MLA_V, cl).transpose(0, 3, 1, 2).reshape(bsz * cl, MLA_HEADS * MLA_V)
                a2 = jnp.concatenate([om, om_c], axis=0)
            else:
                a1 = a1[:nl]
                a2 = om
            w_out = w_out_odd[i].astype(BF16)

        n_rows = n_all if need_ctx else nl
        w_r_hi32 = lax.reduce_precision(w_router[l].astype(F32), exponent_bits=8, mantissa_bits=7)
        w_r_hi = w_r_hi32.astype(BF16)
        w_r_lo = (w_router[l] - w_r_hi32).astype(BF16)
        w_r = (jnp.zeros((d, 2 * LANES), BF16).at[:, :N_EXPERTS].set(w_r_hi)
               .at[:, LANES:LANES + N_EXPERTS].set(w_r_lo))
        b_r = jnp.full((1, LANES), NEG, F32).at[0, :N_EXPERTS].set(b_router[l])
        x_new, m, top_e, gates = _out_proj(a1, a2, x_all, w_out, mod_l, norm2_g[l], w_r, b_r, n_rows, tpb, bsz)
        x_all = _moe(m, top_e, gates, x_new, mod_l, w_exp1, b_exp1[l], w_exp2, b_exp2[l], l, tpb, bsz)
    return x_all[:nl].reshape(bsz, s, d)
```

```python
import functools
import math

import numpy as np
import jax
import jax.numpy as jnp
from jax import lax
from jax.experimental import pallas as pl
from jax.experimental.pallas import tpu as pltpu

F32 = jnp.float32
BF16 = jnp.bfloat16

GRID_W = 64
HEAD_DIM = 64
GQA_HEADS = 8
GQA_KV_HEADS = 2
NA_HEADS = 8
NA_ROWS = 8
NA_COLS = 16
RET_HEADS = 4
RET_DK = 128
MLA_HEADS = 4
MLA_Q_LORA = 256
MLA_KV_LORA = 128
MLA_NOPE = 128
MLA_ROPE = 64
MLA_V = 128
N_EXPERTS = 32
TOP_K = 4
SWIGLU_LIMIT = 7.0
SWIGLU_ALPHA = 1.702
ROPE_THETA = 10000.0
EPS = 1e-6
GN_EPS = 1e-5
LOG2E = math.log2(math.e)
NEG = -1e30

LANES = 128
ROW_TILE = 512
Q_SUB = 256
FLASH_TILES = 2
ONES_ROWS = 16
NA_QB = 256
NA_WIN_ROWS = 12
RET_CHUNK = 256
MOE_BM = 512
TOK_ROWS = 8
VMEM_LIMIT = 56 * 1024 * 1024


def _cp(sem):
    return pltpu.CompilerParams(dimension_semantics=sem, vmem_limit_bytes=VMEM_LIMIT)


def _mod_kernel(c_ref, w_ref, b_ref, o_ref):
    c = c_ref[...]
    s = c * jax.nn.sigmoid(c)
    o_ref[0] = jnp.dot(s, w_ref[0], precision=lax.Precision.HIGHEST, preferred_element_type=F32) + b_ref[0]


def _mod_vectors(c_rows, w_mod, b_mod):
    depth, d, d6 = w_mod.shape
    tn = 1536
    return pl.pallas_call(
        _mod_kernel,
        grid=(depth, d6 // tn),
        in_specs=[pl.BlockSpec((8, d), lambda l, j: (0, 0)),
                  pl.BlockSpec((1, d, tn), lambda l, j: (l, 0, j)),
                  pl.BlockSpec((1, 1, tn), lambda l, j: (l, 0, j))],
        out_specs=pl.BlockSpec((1, 8, tn), lambda l, j: (l, 0, j)),
        out_shape=jax.ShapeDtypeStruct((depth, 8, d6), F32),
        compiler_params=_cp(("parallel", "parallel")),
    )(c_rows, w_mod, b_mod.reshape(depth, 1, d6))


def _modulated_norm(x, g, sc, sh):
    ms = jnp.mean(x * x, axis=-1, keepdims=True)
    return x * lax.rsqrt(ms + EPS) * g * (1.0 + sc) + sh


def _pair_rope(y, cos, sin_signed):
    lane = lax.broadcasted_iota(jnp.int32, y.shape, 1)
    partner = jnp.where((lane & 1) == 0, pltpu.roll(y, LANES - 1, 1), pltpu.roll(y, 1, 1))
    return y * cos + partner * sin_signed


def _store_token_tiles(ref, val):
    n = val.shape[0]
    for j in range(TOK_ROWS):
        ref[pl.ds(j, n, stride=TOK_ROWS), :] = val[:, j * LANES:(j + 1) * LANES]


def _load_token_tiles(ref, n):
    return jnp.concatenate([ref[pl.ds(j, n, stride=TOK_ROWS), :] for j in range(TOK_ROWS)], axis=-1)


def _split_dot(a_f32, w_bf16):
    hi = a_f32.astype(BF16)
    lo = (a_f32 - hi.astype(F32)).astype(BF16)
    return (jnp.dot(hi, w_bf16, preferred_element_type=F32) + jnp.dot(lo, w_bf16, preferred_element_type=F32))


def _proj_kernel(x_ref, sh_ref, sc_ref, g_ref, w_ref, gain_ref, bd_ref, cos_ref, sin_ref, *out_refs, segs):
    a = _modulated_norm(x_ref[...], g_ref[...], sc_ref[0], sh_ref[0]).astype(BF16)
    cos = cos_ref[...]
    sin = sin_ref[...]
    for (start, width, mode, scale), o_ref in zip(segs, out_refs):
        y_seg = jnp.dot(a, w_ref[:, start:start + width], preferred_element_type=F32)
        if mode == "plain":
            o_ref[...] = y_seg.astype(o_ref.dtype)
            continue
        for j in range(width // LANES):
            y = y_seg[:, j * LANES:(j + 1) * LANES]
            if "norm" in mode:
                ms = _split_dot(y * y, bd_ref[...])
                y = y * lax.rsqrt(ms + EPS) * gain_ref[:, start + j * LANES:start + (j + 1) * LANES]
            if scale != 1.0:
                y = y * scale
            if "rope" in mode:
                y = _pair_rope(y, cos, sin)
            o_ref[:, j * LANES:(j + 1) * LANES] = y.astype(o_ref.dtype)


def _proj(x_all, mod_l, norm_g, w, gain, cos_t, sin_t, segs, out_dtypes, n_lat_tiles_per_batch, n_batch):
    n, d = x_all.shape
    tm = ROW_TILE
    n_tiles = n // tm
    wtot = w.shape[1]
    tpb = n_lat_tiles_per_batch
    n_lat_tiles = tpb * n_batch

    def mod_row(t):
        return jnp.where(t < n_lat_tiles, t // tpb, n_batch)

    def rope_row(t):
        return jnp.where(t < n_lat_tiles, t % tpb, tpb)

    bd = np.kron(np.eye(2, dtype=np.float32), np.full((HEAD_DIM, HEAD_DIM), 1.0 / HEAD_DIM, np.float32))
    in_specs = [
        pl.BlockSpec((tm, d), lambda t: (t, 0)),
        pl.BlockSpec((1, 1, d), lambda t: (mod_row(t), 0, 0)),
        pl.BlockSpec((1, 1, d), lambda t: (mod_row(t), 0, 1)),
        pl.BlockSpec((1, d), lambda t: (0, 0)),
        pl.BlockSpec((d, wtot), lambda t: (0, 0)),
        pl.BlockSpec((1, wtot), lambda t: (0, 0)),
        pl.BlockSpec((LANES, LANES), lambda t: (0, 0)),
        pl.BlockSpec((tm, LANES), lambda t: (rope_row(t), 0)),
        pl.BlockSpec((tm, LANES), lambda t: (rope_row(t), 0)),
    ]
    out_specs = [pl.BlockSpec((tm, s[1]), lambda t: (t, 0)) for s in segs]
    out_shape = [jax.ShapeDtypeStruct((n, s[1]), dt) for s, dt in zip(segs, out_dtypes)]
    return pl.pallas_call(
        functools.partial(_proj_kernel, segs=tuple(segs)),
        grid=(n_tiles,),
        in_specs=in_specs,
        out_specs=out_specs,
        out_shape=out_shape,
        compiler_params=_cp(("parallel",)),
    )(x_all, mod_l, mod_l, norm_g.reshape(1, d), w, gain, jnp.asarray(bd, BF16), cos_t, sin_t)


def _mla_proj_kernel(x_ref, wuq_ref, wukv_ref, gcq_ref, gckv_ref, gq_ref, gkn_ref, gkr_ref, cos_ref, sin_ref,
                     q_ref, k_ref, v_ref):
    x = x_ref[...]
    cq = x[:, :MLA_Q_LORA]
    ckv = x[:, MLA_Q_LORA:MLA_Q_LORA + MLA_KV_LORA]
    kr = x[:, MLA_Q_LORA + MLA_KV_LORA:MLA_Q_LORA + MLA_KV_LORA + LANES]
    cos = cos_ref[...]
    sin = sin_ref[...]
    cqn = cq * lax.rsqrt(jnp.mean(cq * cq, axis=-1, keepdims=True) + EPS) * gcq_ref[...]
    ckvn = ckv * lax.rsqrt(jnp.mean(ckv * ckv, axis=-1, keepdims=True) + EPS) * gckv_ref[...]
    q = jnp.dot(cqn.astype(BF16), wuq_ref[...], preferred_element_type=F32)
    kv = jnp.dot(ckvn.astype(BF16), wukv_ref[...], preferred_element_type=F32)
    nh = MLA_HEADS
    d_qk = float(MLA_NOPE + MLA_ROPE)
    lane = lax.broadcasted_iota(jnp.int32, (1, LANES), 1)
    low = lane < MLA_ROPE

    def half_sums(slab):
        sq = slab * slab
        a = jnp.sum(jnp.where(low, sq, 0.0), axis=-1, keepdims=True)
        return a, jnp.sum(sq, axis=-1, keepdims=True) - a

    rope_w = nh * MLA_NOPE
    q_rope_ss = []
    for r in range(nh // 2):
        q_rope_ss.extend(half_sums(q[:, rope_w + r * LANES:rope_w + (r + 1) * LANES]))
    kr_ss, _ = half_sums(kr)
    rs_q, rs_k = [], []
    for h in range(nh):
        qn = q[:, h * MLA_NOPE:(h + 1) * MLA_NOPE]
        kn = kv[:, h * MLA_NOPE:(h + 1) * MLA_NOPE]
        rs_q.append(lax.rsqrt((jnp.sum(qn * qn, axis=-1, keepdims=True) + q_rope_ss[h]) / d_qk + EPS))
        rs_k.append(lax.rsqrt((jnp.sum(kn * kn, axis=-1, keepdims=True) + kr_ss) / d_qk + EPS))
        q_ref[:, h * MLA_NOPE:(h + 1) * MLA_NOPE] = (qn * rs_q[h] * gq_ref[:, h * MLA_NOPE:(h + 1) * MLA_NOPE]).astype(q_ref.dtype)
        k_ref[:, h * MLA_NOPE:(h + 1) * MLA_NOPE] = (kn * rs_k[h] * gkn_ref[...]).astype(k_ref.dtype)
    kr_rot = _pair_rope(kr * gkr_ref[...], cos, sin)
    for r in range(nh // 2):
        sl = slice(rope_w + r * LANES, rope_w + (r + 1) * LANES)
        yq = q[:, sl] * jnp.where(low, rs_q[2 * r], rs_q[2 * r + 1]) * gq_ref[:, sl]
        q_ref[:, sl] = _pair_rope(yq, cos, sin).astype(q_ref.dtype)
        k_ref[:, sl] = (kr_rot * jnp.where(low, rs_k[2 * r], rs_k[2 * r + 1])).astype(k_ref.dtype)
    v_ref[...] = kv[:, nh * MLA_NOPE:].astype(v_ref.dtype)


def _mla_proj(mla_in, wuq, wukv, gcq, gckv, gq, gkn, gkr, cos_t, sin_t, tpb, n_batch):
    n, win = mla_in.shape
    tm = ROW_TILE
    n_lat_tiles = tpb * n_batch

    def rope_row(t):
        return jnp.where(t < n_lat_tiles, t % tpb, tpb)

    qk_w = MLA_HEADS * (MLA_NOPE + MLA_ROPE)
    v_w = MLA_HEADS * MLA_V
    full = lambda a: pl.BlockSpec(a.shape, lambda t: (0,) * a.ndim)
    return pl.pallas_call(
        _mla_proj_kernel,
        grid=(n // tm,),
        in_specs=[pl.BlockSpec((tm, win), lambda t: (t, 0)), full(wuq), full(wukv), full(gcq), full(gckv),
                  full(gq), full(gkn), full(gkr),
                  pl.BlockSpec((tm, LANES), lambda t: (rope_row(t), 0)),
                  pl.BlockSpec((tm, LANES), lambda t: (rope_row(t), 0))],
        out_specs=[pl.BlockSpec((tm, qk_w), lambda t: (t, 0)), pl.BlockSpec((tm, qk_w), lambda t: (t, 0)),
                   pl.BlockSpec((tm, v_w), lambda t: (t, 0))],
        out_shape=[jax.ShapeDtypeStruct((n, qk_w), BF16), jax.ShapeDtypeStruct((n, qk_w), BF16),
                   jax.ShapeDtypeStruct((n, v_w), BF16)],
        compiler_params=_cp(("parallel",)),
    )(mla_in, wuq, wukv, gcq, gckv, gq, gkn, gkr, cos_t, sin_t)


def _flash_kernel(qt_ref, k_ref, vt_ref, ot_ref, q_sc, s_0, s_1, s_2, mx_0, mx_1, mx_2, m_sc, acc_sc, *,
                  g, n_chunks, dv, n_tiles):
    bufs = ((s_0, mx_0), (s_1, mx_1), (s_2, mx_2))
    tq = qt_ref.shape[3] // n_tiles
    for t in range(n_tiles):
        for gi in range(g):
            q_sc[t, :, gi * tq:(gi + 1) * tq] = qt_ref[0, gi, :, t * tq:(t + 1) * tq]

    def reset():
        m_sc[...] = jnp.full(m_sc.shape, NEG, F32)
        acc_sc[...] = jnp.zeros(acc_sc.shape, F32)

    def finalize(t):
        acc = acc_sc[...]
        o = acc[:dv] / acc[dv:dv + 1]
        for gi in range(g):
            ot_ref[0, gi, :, t * tq:(t + 1) * tq] = o[:, gi * tq:(gi + 1) * tq].astype(ot_ref.dtype)

    def scores(t, ci, s_ref, mx_ref):
        s = jnp.dot(k_ref[0, ci], q_sc[t], preferred_element_type=F32)
        s_ref[...] = s
        mx_ref[...] = jnp.max(s, axis=0, keepdims=True)

    def accumulate(ci, s_ref, mx_ref):
        m_old = m_sc[...]
        m_new = jnp.maximum(m_old, mx_ref[...])
        alpha = jnp.exp2(m_old - m_new)
        p = jnp.exp2(s_ref[...] - m_new).astype(BF16)
        acc_sc[...] = alpha * acc_sc[...] + jnp.dot(vt_ref[0, ci], p, preferred_element_type=F32)
        m_sc[...] = m_new

    def fused(t_n, ci_n, s_n, mx_n, ci_c, s_c, mx_c):
        m_old = m_sc[...]
        m_new = jnp.maximum(m_old, mx_c[...])
        alpha = jnp.exp2(m_old - m_new)
        pv = None
        mx = None
        for k0 in range(0, s_c.shape[0], 256):
            s = jnp.dot(k_ref[0, ci_n, k0:k0 + 256, :], q_sc[t_n], preferred_element_type=F32)
            s_n[k0:k0 + 256, :] = s
            mxj = jnp.max(s, axis=0, keepdims=True)
            mx = mxj if mx is None else jnp.maximum(mx, mxj)
            p = jnp.exp2(s_c[k0:k0 + 256, :] - m_new).astype(BF16)
            part = jnp.dot(vt_ref[0, ci_c, :, k0:k0 + 256], p, preferred_element_type=F32)
            pv = part if pv is None else pv + part
        mx_n[...] = mx
        acc_sc[...] = alpha * acc_sc[...] + pv
        m_sc[...] = m_new

    reset()
    if n_chunks == 1:
        for t in range(n_tiles):
            scores(t, 0, *bufs[0])
            accumulate(0, *bufs[0])
            finalize(t)
            reset()
    else:
        scores(0, 0, *bufs[0])
        scores(0, 1, *bufs[1])
        n_fused = n_chunks - 2
        for t in range(n_tiles):
            off = t * n_chunks

            def triple(j, carry, t=t, off=off):
                c = 3 * j
                for r in range(3):
                    fused(t, c + r + 2, *bufs[(r + 2 + off) % 3], c + r, *bufs[(r + off) % 3])
                return carry

            lax.fori_loop(0, n_fused // 3, triple, 0)
            for c in range(3 * (n_fused // 3), n_fused):
                fused(t, c + 2, *bufs[(c + 2 + off) % 3], c, *bufs[(c + off) % 3])
            for c in (n_chunks - 2, n_chunks - 1):
                if t + 1 < n_tiles:
                    c_n = c - (n_chunks - 2)
                    fused(t + 1, c_n, *bufs[(c_n + off + n_chunks) % 3], c, *bufs[(c + off) % 3])
                else:
                    accumulate(c, *bufs[(c + off) % 3])
            finalize(t)
            reset()


def _key_chunk(lk):
    for tk in (1280, 1024, 768, 512, 256):
        if lk % tk == 0:
            return tk
    raise ValueError(f"key length {lk} must be a multiple of 256")


def _flash(qt, k, v, tq_blk):
    bk, g, dq, lq = qt.shape
    lk, dv = v.shape[1], v.shape[2]
    tk = _key_chunk(lk)
    nch = lk // tk
    kc = k.reshape(bk, nch, tk, dq)
    dve = dv + ONES_ROWS
    vt = jnp.concatenate([v, jnp.ones((bk, lk, ONES_ROWS), v.dtype)], axis=-1)
    vt = vt.reshape(bk, nch, tk, dve).transpose(0, 1, 3, 2)
    tq_blk = min(tq_blk, lq)
    n_tiles = FLASH_TILES if (nch > 1 and lq % (FLASH_TILES * tq_blk) == 0) else 1
    assert lq % tq_blk == 0 and tq_blk % LANES == 0
    w = g * tq_blk
    blk = n_tiles * tq_blk
    return pl.pallas_call(
        functools.partial(_flash_kernel, g=g, n_chunks=nch, dv=dv, n_tiles=n_tiles),
        grid=(bk, lq // blk),
        in_specs=[pl.BlockSpec((1, g, dq, blk), lambda b, i: (b, 0, 0, i)),
                  pl.BlockSpec((1, nch, tk, dq), lambda b, i: (b, 0, 0, 0)),
                  pl.BlockSpec((1, nch, dve, tk), lambda b, i: (b, 0, 0, 0))],
        out_specs=pl.BlockSpec((1, g, dv, blk), lambda b, i: (b, 0, 0, i)),
        out_shape=jax.ShapeDtypeStruct((bk, g, dv, lq), BF16),
        scratch_shapes=[pltpu.VMEM((n_tiles, dq, w), BF16)] + [pltpu.VMEM((tk, w), F32)] * 3
                       + [pltpu.VMEM((1, w), F32)] * 4 + [pltpu.VMEM((dve, w), F32)],
        compiler_params=_cp(("parallel", "parallel")),
    )(qt, kc, vt)


def _na_tables(seq_len):
    rows_n = seq_len // GRID_W
    assert rows_n >= NA_WIN_ROWS and NA_ROWS <= rows_n
    nb = seq_len // NA_QB
    rpq = NA_QB // GRID_W
    band = NA_WIN_ROWS * GRID_W
    variants, var_id, bases = {}, [], []
    for j in range(nb):
        base = int(np.clip(rpq * j - NA_ROWS // 2, 0, rows_n - NA_WIN_ROWS))
        bases.append(base)
        t = np.arange(NA_QB) + j * NA_QB
        r, col = t // GRID_W, t % GRID_W
        r0 = np.clip(r - NA_ROWS // 2, 0, rows_n - NA_ROWS)
        c0 = np.clip(col - NA_COLS // 2, 0, GRID_W - NA_COLS)
        kk = np.arange(band)
        kr = base + kk // GRID_W
        kc = kk % GRID_W
        inside = ((kr[None] >= r0[:, None]) & (kr[None] < r0[:, None] + NA_ROWS)
                  & (kc[None] >= c0[:, None]) & (kc[None] < c0[:, None] + NA_COLS))
        rel = (kr[None] - r[:, None] + NA_ROWS - 1) * (2 * NA_COLS - 1) + (kc[None] - col[:, None] + NA_COLS - 1)
        tab = np.where(inside, rel, -1).astype(np.int32)
        assert (inside.sum(axis=1) == NA_ROWS * NA_COLS).all()
        key = tab.tobytes()
        if key not in variants:
            variants[key] = (len(variants), tab)
        var_id.append(variants[key][0])
    tabs = np.stack([v[1] for v in sorted(variants.values(), key=lambda kv: kv[0])])
    n_dr = 2 * NA_ROWS - 1
    n_dc = 2 * NA_COLS - 1
    col = np.arange(GRID_W)
    c0 = np.clip(col - NA_COLS // 2, 0, GRID_W - NA_COLS)
    col_in = (col[None] >= c0[:, None]) & (col[None] < c0[:, None] + NA_COLS)
    dc = col[None] - col[:, None] + NA_COLS - 1
    t5 = tabs.reshape(len(tabs), rpq, GRID_W, NA_WIN_ROWS, GRID_W)
    row_sel = np.full((len(tabs), rpq, NA_WIN_ROWS), n_dr, np.int32)
    for v in range(len(tabs)):
        for a in range(rpq):
            for i in range(NA_WIN_ROWS):
                blk = t5[v, a, :, i, :]
                if (blk >= 0).any():
                    dr = int(blk[blk >= 0][0]) // n_dc
                    assert (np.where(col_in, dr * n_dc + dc, -1) == blk).all()
                    row_sel[v, a, i] = dr
                else:
                    assert (blk < 0).all()
    dc_onehot = (dc[None] == np.arange(n_dc)[:, None, None]).astype(np.float32)
    return np.asarray(bases, np.int32), np.asarray(var_id, np.int32), row_sel, col_in, dc_onehot


def _na_bias(rpb, row_sel, col_in, dc_onehot):
    h = rpb.shape[0]
    n_dr, n_dc = 2 * NA_ROWS - 1, 2 * NA_COLS - 1
    t = jnp.einsum("hdj,jck->hdck", rpb.astype(F32).reshape(h, n_dr, n_dc) * LOG2E, jnp.asarray(dc_onehot),
                   precision=lax.Precision.HIGHEST)
    t = jnp.where(jnp.asarray(col_in)[None, None], t, NEG)
    t = jnp.concatenate([t, jnp.full((h, 1, GRID_W, GRID_W), NEG, F32)], axis=1)
    nv, rpq, nw = row_sel.shape
    b = t[:, jnp.asarray(row_sel)]
    return b.transpose(1, 0, 2, 4, 3, 5).reshape(nv, h, rpq * GRID_W, nw * GRID_W)


def _na_kernel(base_ref, var_ref, q_ref, k_ref, v_ref, kc_ref, vc_ref, bias_ref, o_ref):
    j = pl.program_id(2)
    band = NA_WIN_ROWS * GRID_W
    start = pl.multiple_of(base_ref[j] * GRID_W, GRID_W)
    q = q_ref[...]
    kw = k_ref[pl.ds(start, band), :]
    vw = v_ref[pl.ds(start, band), :]
    kc = kc_ref[...]
    vc = vc_ref[...]
    nt = (((1,), (1,)), ((), ()))
    lane = lax.broadcasted_iota(jnp.int32, q.shape, 1)
    outs = []
    for hh in range(LANES // HEAD_DIM):
        mine = (lane >= hh * HEAD_DIM) & (lane < (hh + 1) * HEAD_DIM)
        qh = jnp.where(mine, q, jnp.zeros_like(q))
        s_win = lax.dot_general(qh, kw, nt, preferred_element_type=F32) + bias_ref[0, hh]
        s_ctx = lax.dot_general(qh, kc, nt, preferred_element_type=F32)
        m = jnp.maximum(jnp.max(s_win, axis=-1, keepdims=True), jnp.max(s_ctx, axis=-1, keepdims=True))
        p_win = jnp.exp2(s_win - m)
        p_ctx = jnp.exp2(s_ctx - m)
        l = jnp.sum(p_win, axis=-1, keepdims=True) + jnp.sum(p_ctx, axis=-1, keepdims=True)
        o = (jnp.dot(p_win.astype(BF16), vw, preferred_element_type=F32)
             + jnp.dot(p_ctx.astype(BF16), vc, preferred_element_type=F32))
        outs.append(o / l)
    o_ref[...] = jnp.where(lane < HEAD_DIM, outs[0], outs[1]).astype(o_ref.dtype)


def _na_attention(q, k, v, bias, bases, var_id, n_batch, seq_len, ctx_len):
    w = q.shape[1]
    assert LANES // HEAD_DIM == 2 and ctx_len % 8 == 0 and (n_batch * seq_len) % ctx_len == 0
    band = NA_WIN_ROWS * GRID_W
    nb = seq_len // NA_QB
    ctx0 = (n_batch * seq_len) // ctx_len
    lat_spec = pl.BlockSpec((seq_len, LANES), lambda b, p, j, bs, vr: (b, p))
    ctx_spec = pl.BlockSpec((ctx_len, LANES), lambda b, p, j, bs, vr: (ctx0 + b, p))
    grid_spec = pltpu.PrefetchScalarGridSpec(
        num_scalar_prefetch=2,
        grid=(n_batch, w // LANES, nb),
        in_specs=[pl.BlockSpec((NA_QB, LANES), lambda b, p, j, bs, vr: (b * nb + j, p)),
                  lat_spec, lat_spec, ctx_spec, ctx_spec,
                  pl.BlockSpec((1, LANES // HEAD_DIM, NA_QB, band), lambda b, p, j, bs, vr: (vr[j], p, 0, 0))],
        out_specs=pl.BlockSpec((NA_QB, LANES), lambda b, p, j, bs, vr: (b * nb + j, p)),
    )
    return pl.pallas_call(
        _na_kernel,
        grid_spec=grid_spec,
        out_shape=jax.ShapeDtypeStruct((n_batch * seq_len, w), BF16),
        compiler_params=_cp(("parallel", "parallel", "arbitrary")),
    )(bases, var_id, q, k, v, k, v, bias)


def _ret_kernel(cdec_ref, qf_ref, kf_ref, vf_ref, qb_ref, kb_ref, vb_ref, dmask_ref, qdec_ref, kdec_ref,
                yf_ref, yb_ref, state_sc):
    @pl.when(pl.program_id(1) == 0)
    def _():
        state_sc[...] = jnp.zeros(state_sc.shape, F32)

    nt = (((1,), (1,)), ((), ()))
    tn = (((0,), (0,)), ((), ()))
    dk = RET_DK
    for d, (q_ref, k_ref, v_ref, y_ref) in enumerate(((qf_ref, kf_ref, vf_ref, yf_ref),
                                                      (qb_ref, kb_ref, vb_ref, yb_ref))):
        for h in range(RET_HEADS):
            sl = slice(h * dk, (h + 1) * dk)
            q = q_ref[:, sl]
            k = k_ref[:, sl]
            v = v_ref[:, sl]
            st = state_sc[d, h]
            a = lax.dot_general(q, k, nt, preferred_element_type=F32) * dmask_ref[d, h]
            inner = jnp.dot(a.astype(BF16), v, preferred_element_type=F32)
            cross = jnp.dot(q, st.astype(BF16), preferred_element_type=F32) * qdec_ref[d, h]
            y_ref[:, sl] = inner + cross
            vs = (v.astype(F32) * kdec_ref[d, h]).astype(BF16)
            state_sc[d, h] = st * cdec_ref[d * RET_HEADS + h] + lax.dot_general(k, vs, tn, preferred_element_type=F32)


def _retention(rq, rk, rv, dmask, qdec, kdec, cdec, n_batch, seq_len, ctx_len):
    n, w = rq.shape
    c = RET_CHUNK
    assert ctx_len == c and seq_len % c == 0
    ncl = seq_len // c
    ctx_blk0 = (n_batch * seq_len) // c

    def fwd(b, s, cd):
        return (jnp.where(s == 0, ctx_blk0 + b, b * ncl + s - 1), 0)

    def bwd(b, s, cd):
        return (jnp.where(s == 0, ctx_blk0 + b, b * ncl + ncl - s), 0)

    full = lambda a: pl.BlockSpec(a.shape, lambda b, s, cd: (0,) * a.ndim)
    grid_spec = pltpu.PrefetchScalarGridSpec(
        num_scalar_prefetch=1,
        grid=(n_batch, ncl + 1),
        in_specs=[pl.BlockSpec((c, w), fwd)] * 3 + [pl.BlockSpec((c, w), bwd)] * 3 + [full(dmask), full(qdec), full(kdec)],
        out_specs=[pl.BlockSpec((c, w), fwd), pl.BlockSpec((c, w), bwd)],
        scratch_shapes=[pltpu.VMEM((2, RET_HEADS, RET_DK, RET_DK), F32)],
    )
    return pl.pallas_call(
        _ret_kernel,
        grid_spec=grid_spec,
        out_shape=[jax.ShapeDtypeStruct((n, w), F32)] * 2,
        compiler_params=_cp(("parallel", "arbitrary")),
    )(cdec, rq, rk, rv, rq, rk, rv, dmask, qdec, kdec)


def _ret_finish_kernel(yf_ref, yb_ref, rg_ref, gn_ref, o_ref):
    y = yf_ref[...] + yb_ref[...]
    gate = rg_ref[...]
    gate = gate * jax.nn.sigmoid(gate)
    for h in range(RET_HEADS):
        sl = slice(h * RET_DK, (h + 1) * RET_DK)
        yh = y[:, sl]
        mu = jnp.mean(yh, axis=-1, keepdims=True)
        var = jnp.mean(jnp.square(yh - mu), axis=-1, keepdims=True)
        o_ref[:, sl] = ((yh - mu) * lax.rsqrt(var + GN_EPS) * gn_ref[:, sl] * gate[:, sl]).astype(o_ref.dtype)


def _ret_finish(yf, yb, rg, gn):
    n, w = yf.shape
    tm = ROW_TILE
    spec = pl.BlockSpec((tm, w), lambda t: (t, 0))
    return pl.pallas_call(
        _ret_finish_kernel,
        grid=(n // tm,),
        in_specs=[spec, spec, spec, pl.BlockSpec((1, w), lambda t: (0, 0))],
        out_specs=spec,
        out_shape=jax.ShapeDtypeStruct((n, w), BF16),
        compiler_params=_cp(("parallel",)),
    )(yf, yb, rg, gn)


def _out_kernel(a1_ref, a2_ref, x_ref, w_ref, g1_ref, ng_ref, sh_ref, sc_ref, wr_ref, br_ref,
                xo_ref, m_ref, e_ref, gt_ref):
    half = a1_ref.shape[1]
    o = (jnp.dot(a1_ref[...], w_ref[:half, :], preferred_element_type=F32)
         + jnp.dot(a2_ref[...], w_ref[half:, :], preferred_element_type=F32))
    x = x_ref[...] + g1_ref[0] * o
    xo_ref[...] = x
    m = _modulated_norm(x, ng_ref[...], sc_ref[0], sh_ref[0])
    _store_token_tiles(m_ref, m)
    m_hi = m.astype(BF16)
    m_lo = (m - m_hi.astype(F32)).astype(BF16)
    hi_prod = jnp.dot(m_hi, wr_ref[...], preferred_element_type=F32)
    logits = (hi_prod[:, :LANES] + hi_prod[:, LANES:]
              + jnp.dot(m_lo, wr_ref[:, :LANES], preferred_element_type=F32) + br_ref[...])
    lane = lax.broadcasted_iota(jnp.int32, logits.shape, 1).astype(F32)
    e_out = jnp.zeros(logits.shape, F32)
    g_out = jnp.zeros(logits.shape, F32)
    top0 = None
    denom = None
    for kk in range(TOP_K):
        mx = jnp.max(logits, axis=-1, keepdims=True)
        idx = jnp.min(jnp.where(logits == mx, lane, float(LANES)), axis=-1, keepdims=True)
        if kk == 0:
            top0 = mx
            ex = jnp.ones_like(mx)
            denom = ex
        else:
            ex = jnp.exp(mx - top0)
            denom = denom + ex
        e_out = jnp.where(lane == kk, idx, e_out)
        g_out = jnp.where(lane == kk, ex, g_out)
        logits = jnp.where(lane == idx, NEG * 2.0, logits)
    e_ref[...] = e_out.astype(jnp.int32)
    gt_ref[...] = g_out / denom


def _out_proj(a1, a2, x_all, w_out, mod_l, norm2_g, w_r, b_r, n_rows, tpb, n_batch):
    d = x_all.shape[1]
    half = a1.shape[1]
    tm = ROW_TILE
    n_lat_tiles = tpb * n_batch

    def mod_row(t):
        return jnp.where(t < n_lat_tiles, t // tpb, n_batch)

    row = lambda wd: pl.BlockSpec((tm, wd), lambda t: (t, 0))
    modspec = lambda col: pl.BlockSpec((1, 1, d), lambda t: (mod_row(t), 0, col))
    return pl.pallas_call(
        _out_kernel,
        grid=(n_rows // tm,),
        in_specs=[row(half), row(half), row(d), pl.BlockSpec((2 * half, d), lambda t: (0, 0)),
                  modspec(2), pl.BlockSpec((1, d), lambda t: (0, 0)), modspec(3), modspec(4),
                  pl.BlockSpec((d, 2 * LANES), lambda t: (0, 0)), pl.BlockSpec((1, LANES), lambda t: (0, 0))],
        out_specs=[row(d), pl.BlockSpec((tm * TOK_ROWS, LANES), lambda t: (t, 0)), row(LANES), row(LANES)],
        out_shape=[jax.ShapeDtypeStruct((n_rows, d), F32), jax.ShapeDtypeStruct((n_rows * TOK_ROWS, LANES), F32),
                   jax.ShapeDtypeStruct((n_rows, LANES), jnp.int32), jax.ShapeDtypeStruct((n_rows, LANES), F32)],
        compiler_params=_cp(("parallel",)),
    )(a1, a2, x_all, w_out, mod_l, norm2_g.reshape(1, d), mod_l, mod_l, w_r, b_r)


def _token_rows(ref, idx):
    return ref.at[pl.ds(pl.multiple_of(idx * TOK_ROWS, TOK_ROWS), TOK_ROWS)]


def _dispatch_kernel(pad_start_ref, pad_cnt_ref, pos_ref, m_ref, hs_ref, zero_sc, sem, pad_sem):
    n_tok = m_ref.shape[0] // TOK_ROWS

    @pl.when(pl.program_id(0) == 0)
    def _():
        zero_sc[...] = jnp.zeros(zero_sc.shape, zero_sc.dtype)

        def per_expert(e, carry):
            def fill(r, c):
                pltpu.make_async_copy(zero_sc, _token_rows(hs_ref, pad_start_ref[e] + r), pad_sem).start()
                return c

            lax.fori_loop(0, pad_cnt_ref[e], fill, 0)

            def fill_done(r, c):
                pltpu.make_async_copy(zero_sc, _token_rows(hs_ref, 0), pad_sem).wait()
                return c

            lax.fori_loop(0, pad_cnt_ref[e], fill_done, 0)
            return carry

        lax.fori_loop(0, N_EXPERTS, per_expert, 0)

    def issue(r, carry):
        for kk in range(TOP_K):
            pltpu.make_async_copy(_token_rows(m_ref, r), _token_rows(hs_ref, pos_ref[r * TOP_K + kk]),
                                  sem).start(priority=kk % 2)
        return carry

    lax.fori_loop(0, n_tok, issue, 0)

    for kk in range(TOP_K):
        pltpu.make_async_copy(m_ref, hs_ref.at[pl.ds(0, n_tok * TOK_ROWS)], sem).wait()


def _dispatch(m_tiles, pos, pad_start, pad_cnt, cap):
    n_tok = m_tiles.shape[0] // TOK_ROWS
    tm = ROW_TILE
    grid_spec = pltpu.PrefetchScalarGridSpec(
        num_scalar_prefetch=2,
        grid=(n_tok // tm,),
        in_specs=[pl.BlockSpec((tm * TOP_K,), lambda t, ps, pc: (t,), memory_space=pltpu.SMEM),
                  pl.BlockSpec((tm * TOK_ROWS, LANES), lambda t, ps, pc: (t, 0))],
        out_specs=pl.BlockSpec(memory_space=pl.ANY),
        scratch_shapes=[pltpu.VMEM((TOK_ROWS, LANES), F32), pltpu.SemaphoreType.DMA(()), pltpu.SemaphoreType.DMA(())],
    )
    return pl.pallas_call(
        _dispatch_kernel,
        grid_spec=grid_spec,
        out_shape=jax.ShapeDtypeStruct((cap * TOK_ROWS, LANES), F32),
        compiler_params=pltpu.CompilerParams(dimension_semantics=("arbitrary",), has_side_effects=True,
                                             vmem_limit_bytes=VMEM_LIMIT),
    )(pad_start, pad_cnt, pos, m_tiles)


def _expert_kernel(be_ref, nu_ref, x_ref, w1_ref, b1_ref, w2_ref, b2_ref, o_ref, w1_sc, w2_sc):
    i = pl.program_id(0)

    @pl.when((i == 0) | (be_ref[i] != be_ref[jnp.maximum(i - 1, 0)]))
    def _():
        w1_sc[...] = w1_ref[0].astype(BF16)
        w2_sc[...] = w2_ref[0].astype(BF16)

    @pl.when(i < nu_ref[0])
    def _():
        dff = w2_ref.shape[1]
        x = _load_token_tiles(x_ref, MOE_BM).astype(BF16)
        u = jnp.dot(x, w1_sc[...], preferred_element_type=F32) + b1_ref[0]
        gl = jnp.minimum(u[:, :dff], SWIGLU_LIMIT)
        up = jnp.clip(u[:, dff:], -SWIGLU_LIMIT, SWIGLU_LIMIT)
        act = gl * jax.nn.sigmoid(SWIGLU_ALPHA * gl) * (up + 1.0)
        _store_token_tiles(o_ref, jnp.dot(act.astype(BF16), w2_sc[...], preferred_element_type=F32) + b2_ref[0])

    @pl.when(i >= nu_ref[0])
    def _():
        o_ref[...] = jnp.zeros(o_ref.shape, o_ref.dtype)


def _experts(hs, blk_e, n_used, w1_all, b1, w2_all, b2, layer):
    _, ne, d, dff2 = w1_all.shape
    cap = hs.shape[0] // TOK_ROWS
    dff = dff2 // 2
    bm = MOE_BM
    tile_spec = pl.BlockSpec((bm * TOK_ROWS, LANES), lambda i, be, nu: (i, 0))
    grid_spec = pltpu.PrefetchScalarGridSpec(
        num_scalar_prefetch=2,
        grid=(cap // bm,),
        in_specs=[tile_spec,
                  pl.BlockSpec((None, 1, d, dff2), lambda i, be, nu: (layer, be[i], 0, 0)),
                  pl.BlockSpec((1, 1, dff2), lambda i, be, nu: (be[i], 0, 0)),
                  pl.BlockSpec((None, 1, dff, d), lambda i, be, nu: (layer, be[i], 0, 0)),
                  pl.BlockSpec((1, 1, d), lambda i, be, nu: (be[i], 0, 0))],
        out_specs=tile_spec,
        scratch_shapes=[pltpu.VMEM((d, dff2), BF16), pltpu.VMEM((dff, d), BF16)],
    )
    return pl.pallas_call(
        _expert_kernel,
        grid_spec=grid_spec,
        out_shape=jax.ShapeDtypeStruct(hs.shape, F32),
        compiler_params=_cp(("arbitrary",)),
    )(blk_e, n_used, hs, w1_all, b1.reshape(ne, 1, dff2), w2_all, b2.reshape(ne, 1, d))


def _combine_kernel(pos_ref, pos_next_ref, x_ref, gt_ref, g2_ref, out_ref, o_ref, y_sc, sem):
    t = pl.program_id(0)
    n_tok = x_ref.shape[0]
    slot = t % 2

    def gather(p_ref, sl):
        def issue(r, carry):
            for kk in range(TOP_K):
                pltpu.make_async_copy(_token_rows(out_ref, p_ref[r * TOP_K + kk]),
                                      _token_rows(y_sc.at[sl * TOP_K + kk], r), sem.at[sl]).start(priority=kk % 2)
            return carry

        lax.fori_loop(0, n_tok, issue, 0)

    @pl.when(t == 0)
    def _():
        gather(pos_ref, 0)

    @pl.when(t + 1 < pl.num_programs(0))
    def _():
        gather(pos_next_ref, 1 - slot)

    for kk in range(TOP_K):
        pltpu.make_async_copy(out_ref.at[pl.ds(0, n_tok * TOK_ROWS)], y_sc.at[slot * TOP_K + kk], sem.at[slot]).wait()
    gt = gt_ref[...]
    for j in range(TOK_ROWS):
        sl = slice(j * LANES, (j + 1) * LANES)
        acc = y_sc[slot * TOP_K, pl.ds(j, n_tok, stride=TOK_ROWS), :] * gt[:, 0:1]
        for kk in range(1, TOP_K):
            acc = acc + y_sc[slot * TOP_K + kk, pl.ds(j, n_tok, stride=TOK_ROWS), :] * gt[:, kk:kk + 1]
        o_ref[:, sl] = x_ref[:, sl] + g2_ref[0, :, sl] * acc


def _combine(x, out_tiles, pos, gates, mod_l, tpb, n_batch):
    n, d = x.shape
    tm = ROW_TILE
    n_lat_tiles = tpb * n_batch

    def mod_row(t):
        return jnp.where(t < n_lat_tiles, t // tpb, n_batch)

    n_tiles = n // tm
    return pl.pallas_call(
        _combine_kernel,
        grid=(n_tiles,),
        in_specs=[pl.BlockSpec((tm * TOP_K,), lambda t: (t,), memory_space=pltpu.SMEM),
                  pl.BlockSpec((tm * TOP_K,), lambda t: (jnp.minimum(t + 1, n_tiles - 1),), memory_space=pltpu.SMEM),
                  pl.BlockSpec((tm, d), lambda t: (t, 0)), pl.BlockSpec((tm, LANES), lambda t: (t, 0)),
                  pl.BlockSpec((1, 1, d), lambda t: (mod_row(t), 0, 5)), pl.BlockSpec(memory_space=pl.ANY)],
        out_specs=pl.BlockSpec((tm, d), lambda t: (t, 0)),
        out_shape=jax.ShapeDtypeStruct((n, d), F32),
        scratch_shapes=[pltpu.VMEM((2 * TOP_K, tm * TOK_ROWS, LANES), F32), pltpu.SemaphoreType.DMA((2,))],
        compiler_params=_cp(("arbitrary",)),
    )(pos, pos, x, gates, mod_l, out_tiles)


def _moe(m_tiles, top_e, gates, x, mod_l, w1_all, b1, w2_all, b2, layer, tpb, n_batch):
    n, d = x.shape
    nk = n * TOP_K
    bm = MOE_BM
    n_blk = (nk + N_EXPERTS * (bm - 1)) // bm + 1
    cap = n_blk * bm
    flat_e = top_e[:, :TOP_K].reshape(nk)
    onehot = (flat_e[:, None] == jnp.arange(N_EXPERTS, dtype=jnp.int32)[None, :]).astype(jnp.int32)
    csum = jnp.cumsum(onehot, axis=0)
    rank = jnp.sum(jnp.where(onehot > 0, csum, 0), axis=1) - 1
    counts = csum[-1]
    padded = ((counts + bm - 1) // bm) * bm
    pend = jnp.cumsum(padded)
    pstart = pend - padded
    pos = (pstart[flat_e] + rank).astype(jnp.int32)
    n_used = (pend[-1] // bm).astype(jnp.int32)
    blk = jnp.minimum(jnp.arange(n_blk, dtype=jnp.int32), n_used - 1)
    blk_e = jnp.sum((pend[None, :] <= (blk * bm)[:, None]).astype(jnp.int32), axis=1)
    blk_e = jnp.clip(blk_e, 0, N_EXPERTS - 1).astype(jnp.int32)
    hs = _dispatch(m_tiles, pos, (pstart + counts).astype(jnp.int32), (padded - counts).astype(jnp.int32), cap)
    out = _experts(hs, blk_e, n_used.reshape(1), w1_all, b1, w2_all, b2, layer)
    return _combine(x, out, pos, gates, mod_l, tpb, n_batch)


def _rope_tables(seq_len, d_rot, reps, n_extra):
    t = jnp.arange(seq_len)
    rows = (t // GRID_W).astype(F32)
    cols = (t % GRID_W).astype(F32)
    n_freq = d_rot // 4
    inv = ROPE_THETA ** (-jnp.arange(n_freq, dtype=F32) / n_freq)
    ang = jnp.concatenate([rows[:, None] * inv, cols[:, None] * inv], axis=-1)
    cos = jnp.repeat(jnp.cos(ang), 2, axis=-1)
    sin = jnp.repeat(jnp.sin(ang), 2, axis=-1) * jnp.tile(jnp.asarray([-1.0, 1.0], F32), d_rot // 2)
    cos = jnp.concatenate([jnp.tile(cos, (1, reps)), jnp.ones((n_extra, d_rot * reps), F32)], axis=0)
    sin = jnp.concatenate([jnp.tile(sin, (1, reps)), jnp.zeros((n_extra, d_rot * reps), F32)], axis=0)
    return cos, sin


def _retention_tables(decay_logit):
    log_g = jax.nn.log_sigmoid(decay_logit.astype(F32))
    c = RET_CHUNK
    pos = jnp.arange(c, dtype=F32)
    diff = pos[:, None] - pos[None, :]
    lf = log_g[0][:, None, None]
    lb = log_g[1][:, None, None]
    dm_f = jnp.where(diff >= 0, jnp.exp(lf * jnp.where(diff >= 0, diff, 0.0)), 0.0)
    dm_b = jnp.where(diff < 0, jnp.exp(lb * jnp.where(diff < 0, -diff, 0.0)), 0.0)
    qd_f = jnp.exp(log_g[0][:, None] * (pos + 1.0))
    qd_b = jnp.exp(log_g[1][:, None] * (c - pos))
    kd_f = jnp.exp(log_g[0][:, None] * (c - 1.0 - pos))
    kd_b = jnp.exp(log_g[1][:, None] * pos)
    bc = lambda a: jnp.broadcast_to(a[..., None], a.shape + (LANES,))
    dmask = jnp.stack([dm_f, dm_b])
    qdec = jnp.stack([bc(qd_f), bc(qd_b)])
    kdec = jnp.stack([bc(kd_f), bc(kd_b)])
    cdec = jnp.exp(log_g * c).reshape(-1)
    return dmask, qdec, kdec, cdec


def _heads_major(t, n_batch, length, n_heads):
    return t.reshape(n_batch, length, n_heads, -1).transpose(0, 2, 1, 3)


def _heads_t(t, n_batch, length, n_heads):
    return t.reshape(n_batch, length, n_heads, -1).transpose(0, 2, 3, 1)


def kernel(x, c, ctx, c_ctx, norm1_g, norm2_g, w_mod, b_mod, w_in_even, w_out_even, a_q_norm, a_k_norm, b_q_norm, b_k_norm, b_rpb, w_in_odd, w_out_odd, ret_decay, ret_gn, mla_cq_norm, mla_ckv_norm, w_uq, w_ukv, mla_q_norm, mla_k_norm, w_router, b_router, w_exp1, b_exp1, w_exp2, b_exp2):
    bsz, s, d = x.shape
    cl = ctx.shape[1]
    depth = w_mod.shape[0]
    tm = ROW_TILE
    assert s % tm == 0 and (bsz * cl) == tm and s % NA_QB == 0 and bsz + 1 <= 8 and d == TOK_ROWS * LANES
    tpb = s // tm
    nl = bsz * s
    n_all = nl + bsz * cl

    c_rows = jnp.zeros((8, d), F32).at[:bsz].set(c).at[bsz].set(c_ctx)
    mod = _mod_vectors(c_rows, w_mod, b_mod)

    cos_a, sin_a = _rope_tables(s, HEAD_DIM, LANES // HEAD_DIM, tm)
    cos_c, sin_c = _rope_tables(s, RET_DK, 1, tm)
    cos_d, sin_d = _rope_tables(s, MLA_ROPE, LANES // MLA_ROPE, tm)
    na_bases, na_var, na_row_sel, na_col_in, na_dc_onehot = _na_tables(s)

    x_all = jnp.concatenate([x.reshape(nl, d), ctx.reshape(bsz * cl, d)], axis=0)
    lat = lambda t: t[:nl]
    cx = lambda t: t[nl:]
    att_scale = HEAD_DIM ** -0.5 * LOG2E
    grp = GQA_HEADS // GQA_KV_HEADS
    bkv = bsz * GQA_KV_HEADS

    for l in range(depth):
        need_ctx = l < depth - 1
        i = l // 2
        mod_l = mod[l].reshape(8, 1, 6 * d)
        if l % 2 == 0:
            wq, wk, wv, wn = GQA_HEADS * HEAD_DIM, GQA_KV_HEADS * HEAD_DIM, GQA_KV_HEADS * HEAD_DIM, NA_HEADS * HEAD_DIM
            starts = np.cumsum([0, wq, wk, wv, wn, wn])
            segs = [(int(starts[0]), wq, "norm_rope", 1.0), (int(starts[1]), wk, "norm_rope", 1.0),
                    (int(starts[2]), wv, "plain", 1.0), (int(starts[3]), wn, "norm", 1.0),
                    (int(starts[4]), wn, "norm", 1.0), (int(starts[5]), wn, "plain", 1.0)]
            gain = jnp.concatenate([jnp.tile(a_q_norm[i], GQA_HEADS) * att_scale, jnp.tile(a_k_norm[i], GQA_KV_HEADS),
                                    jnp.ones((wv,), F32), jnp.tile(b_q_norm[i], NA_HEADS) * att_scale,
                                    jnp.tile(b_k_norm[i], NA_HEADS), jnp.ones((wn,), F32)]).reshape(1, -1)
            qa, ka, va, qb, kb, vb = _proj(x_all, mod_l, norm1_g[l], w_in_even[i].astype(BF16), gain, cos_a, sin_a,
                                           segs, [BF16] * 6, tpb, bsz)
            k_lat = _heads_major(lat(ka), bsz, s, GQA_KV_HEADS)
            k_cx = _heads_major(cx(ka), bsz, cl, GQA_KV_HEADS)
            v_lat = _heads_major(lat(va), bsz, s, GQA_KV_HEADS)
            v_cx = _heads_major(cx(va), bsz, cl, GQA_KV_HEADS)
            k_all = jnp.concatenate([k_lat, k_cx], axis=2).reshape(bkv, s + cl, HEAD_DIM)
            v_all = jnp.concatenate([v_lat, v_cx], axis=2).reshape(bkv, s + cl, HEAD_DIM)
            qt = _heads_t(lat(qa), bsz, s, GQA_HEADS).reshape(bkv, grp, HEAD_DIM, s)
            oa_t = _flash(qt, k_all, v_all, Q_SUB)
            oa = oa_t.reshape(bsz, GQA_HEADS, HEAD_DIM, s).transpose(0, 3, 1, 2).reshape(nl, wq)
            bias = _na_bias(b_rpb[i], na_row_sel, na_col_in, na_dc_onehot)
            bh = bsz * NA_HEADS
            ob = _na_attention(qb, kb, vb, bias, jnp.asarray(na_bases), jnp.asarray(na_var), bsz, s, cl)
            kn_c = _heads_major(cx(kb), bsz, cl, NA_HEADS).reshape(bh, cl, HEAD_DIM)
            vn_c = _heads_major(cx(vb), bsz, cl, NA_HEADS).reshape(bh, cl, HEAD_DIM)
            if need_ctx:
                qt_c = _heads_t(cx(qa), bsz, cl, GQA_HEADS).reshape(bkv, grp, HEAD_DIM, cl)
                oa_c = _flash(qt_c, k_cx.reshape(bkv, cl, HEAD_DIM), v_cx.reshape(bkv, cl, HEAD_DIM), Q_SUB)
                oa_c = oa_c.reshape(bsz, GQA_HEADS, HEAD_DIM, cl).transpose(0, 3, 1, 2).reshape(bsz * cl, wq)
                qnt_c = _heads_t(cx(qb), bsz, cl, NA_HEADS).reshape(bh, 1, HEAD_DIM, cl)
                ob_c = _flash(qnt_c, kn_c, vn_c, Q_SUB)
                ob_c = ob_c.reshape(bsz, NA_HEADS, HEAD_DIM, cl).transpose(0, 3, 1, 2).reshape(bsz * cl, wn)
                a1 = jnp.concatenate([oa, oa_c], axis=0)
                a2 = jnp.concatenate([ob, ob_c], axis=0)
            else:
                a1, a2 = oa, ob
            w_out = w_out_even[i].astype(BF16)
        else:
            rw = RET_HEADS * RET_DK
            kr_cols = w_in_odd[i][:, 4 * rw + MLA_Q_LORA + MLA_KV_LORA:]
            w_ext = jnp.concatenate([w_in_odd[i]] + [kr_cols] * (LANES * 2 // MLA_ROPE - 1), axis=1).astype(BF16)
            mla_w = MLA_Q_LORA + MLA_KV_LORA + 2 * LANES
            segs = [(0, rw, "rope", RET_DK ** -0.5), (rw, rw, "rope", 1.0), (2 * rw, rw, "plain", 1.0),
                    (3 * rw, rw, "plain", 1.0), (4 * rw, mla_w, "plain", 1.0)]
            gain = jnp.ones((1, w_ext.shape[1]), F32)
            rq, rk, rv, rg, mla_in = _proj(x_all, mod_l, norm1_g[l], w_ext, gain, cos_c, sin_c, segs,
                                           [BF16, BF16, BF16, F32, F32], tpb, bsz)
            dmask, qdec, kdec, cdec = _retention_tables(ret_decay[i])
            yf, yb = _retention(rq, rk, rv, dmask, qdec, kdec, cdec, bsz, s, cl)
            a1 = _ret_finish(yf, yb, rg, ret_gn[i].reshape(1, rw))
            dqk = MLA_NOPE + MLA_ROPE
            perm_q = np.concatenate([np.arange(h * dqk, h * dqk + MLA_NOPE) for h in range(MLA_HEADS)]
                                    + [np.arange(h * dqk + MLA_NOPE, (h + 1) * dqk) for h in range(MLA_HEADS)])
            dkv = MLA_NOPE + MLA_V
            perm_kv = np.concatenate([np.arange(h * dkv, h * dkv + MLA_NOPE) for h in range(MLA_HEADS)]
                                     + [np.arange(h * dkv + MLA_NOPE, (h + 1) * dkv) for h in range(MLA_HEADS)])
            mla_scale = dqk ** -0.5 * LOG2E
            gq = jnp.concatenate([jnp.tile(mla_q_norm[i][:MLA_NOPE], MLA_HEADS),
                                  jnp.tile(mla_q_norm[i][MLA_NOPE:], MLA_HEADS)]).reshape(1, -1) * mla_scale
            gkn = mla_k_norm[i][:MLA_NOPE].reshape(1, -1)
            gkr = jnp.tile(mla_k_norm[i][MLA_NOPE:], LANES // MLA_ROPE).reshape(1, -1)
            q_m, k_m, v_m = _mla_proj(mla_in, w_uq[i][:, perm_q].astype(BF16), w_ukv[i][:, perm_kv].astype(BF16),
                                      mla_cq_norm[i].reshape(1, -1), mla_ckv_norm[i].reshape(1, -1), gq, gkn, gkr,
                                      cos_d, sin_d, tpb, bsz)
            nw = MLA_HEADS * MLA_NOPE

            def qk_heads(t, length, transposed):
                nope = t[:, :nw].reshape(bsz, length, MLA_HEADS, MLA_NOPE)
                rope = t[:, nw:].reshape(bsz, length, MLA_HEADS, MLA_ROPE)
                full = jnp.concatenate([nope, rope], axis=-1)
                return full.transpose(0, 2, 3, 1) if transposed else full.transpose(0, 2, 1, 3)

            bhm = bsz * MLA_HEADS
            k_lat = qk_heads(lat(k_m), s, False)
            k_cx = qk_heads(cx(k_m), cl, False)
            v_lat = _heads_major(lat(v_m), bsz, s, MLA_HEADS)
            v_cx = _heads_major(cx(v_m), bsz, cl, MLA_HEADS)
            k_all = jnp.concatenate([k_lat, k_cx], axis=2).reshape(bhm, s + cl, dqk)
            v_all = jnp.concatenate([v_lat, v_cx], axis=2).reshape(bhm, s + cl, MLA_V)
            qt = qk_heads(lat(q_m), s, True).reshape(bhm, 1, dqk, s)
            om = _flash(qt, k_all, v_all, 4 * Q_SUB)
            om = om.reshape(bsz, MLA_HEADS, MLA_V, s).transpose(0, 3, 1, 2).reshape(nl, MLA_HEADS * MLA_V)
            if need_ctx:
                qt_c = qk_heads(cx(q_m), cl, True).reshape(bhm, 1, dqk, cl)
                om_c = _flash(qt_c, k_cx.reshape(bhm, cl, dqk), v_cx.reshape(bhm, cl, MLA_V), Q_SUB)
                om_c = om_c.reshape(bsz, MLA_HEADS, MLA_V, cl).transpose(0, 3, 1, 2).reshape(bsz * cl, MLA_HEADS * MLA_V)
                a2 = jnp.concatenate([om, om_c], axis=0)
            else:
                a1 = a1[:nl]
                a2 = om
            w_out = w_out_odd[i].astype(BF16)

        n_rows = n_all if need_ctx else nl
        w_r_hi32 = lax.reduce_precision(w_router[l].astype(F32), exponent_bits=8, mantissa_bits=7)
        w_r_hi = w_r_hi32.astype(BF16)
        w_r_lo = (w_router[l] - w_r_hi32).astype(BF16)
        w_r = (jnp.zeros((d, 2 * LANES), BF16).at[:, :N_EXPERTS].set(w_r_hi)
               .at[:, LANES:LANES + N_EXPERTS].set(w_r_lo))
        b_r = jnp.full((1, LANES), NEG, F32).at[0, :N_EXPERTS].set(b_router[l])
        x_new, m, top_e, gates = _out_proj(a1, a2, x_all, w_out, mod_l, norm2_g[l], w_r, b_r, n_rows, tpb, bsz)
        x_all = _moe(m, top_e, gates, x_new, mod_l, w_exp1, b_exp1[l], w_exp2, b_exp2[l], l, tpb, bsz)
    return x_all[:nl].reshape(bsz, s, d)
```

```python
import functools
import math

import numpy as np
import jax
import jax.numpy as jnp
from jax import lax
from jax.experimental import pallas as pl
from jax.experimental.pallas import tpu as pltpu

F32 = jnp.float32
BF16 = jnp.bfloat16

GRID_W = 64
HEAD_DIM = 64
GQA_HEADS = 8
GQA_KV_HEADS = 2
NA_HEADS = 8
NA_ROWS = 8
NA_COLS = 16
RET_HEADS = 4
RET_DK = 128
MLA_HEADS = 4
MLA_Q_LORA = 256
MLA_KV_LORA = 128
MLA_NOPE = 128
MLA_ROPE = 64
MLA_V = 128
N_EXPERTS = 32
TOP_K = 4
SWIGLU_LIMIT = 7.0
SWIGLU_ALPHA = 1.702
ROPE_THETA = 10000.0
EPS = 1e-6
GN_EPS = 1e-5
LOG2E = math.log2(math.e)
NEG = -1e30

LANES = 128
ROW_TILE = 512
Q_SUB = 256
FLASH_TILES = 2
ONES_ROWS = 16
NA_QB = 256
NA_WIN_ROWS = 12
RET_CHUNK = 256
MOE_BM = 512
TOK_ROWS = 8
VMEM_LIMIT = 56 * 1024 * 1024


def _cp(sem):
    return pltpu.CompilerParams(dimension_semantics=sem, vmem_limit_bytes=VMEM_LIMIT)


def _mod_kernel(c_ref, w_ref, b_ref, o_ref):
    c = c_ref[...]
    s = c * jax.nn.sigmoid(c)
    o_ref[0] = jnp.dot(s, w_ref[0], precision=lax.Precision.HIGHEST, preferred_element_type=F32) + b_ref[0]


def _mod_vectors(c_rows, w_mod, b_mod):
    depth, d, d6 = w_mod.shape
    tn = 1536
    return pl.pallas_call(
        _mod_kernel,
        grid=(depth, d6 // tn),
        in_specs=[pl.BlockSpec((8, d), lambda l, j: (0, 0)),
                  pl.BlockSpec((1, d, tn), lambda l, j: (l, 0, j)),
                  pl.BlockSpec((1, 1, tn), lambda l, j: (l, 0, j))],
        out_specs=pl.BlockSpec((1, 8, tn), lambda l, j: (l, 0, j)),
        out_shape=jax.ShapeDtypeStruct((depth, 8, d6), F32),
        compiler_params=_cp(("parallel", "parallel")),
    )(c_rows, w_mod, b_mod.reshape(depth, 1, d6))


def _modulated_norm(x, g, sc, sh):
    ms = jnp.mean(x * x, axis=-1, keepdims=True)
    return x * lax.rsqrt(ms + EPS) * g * (1.0 + sc) + sh


def _pair_rope(y, cos, sin_signed):
    lane = lax.broadcasted_iota(jnp.int32, y.shape, 1)
    partner = jnp.where((lane & 1) == 0, pltpu.roll(y, LANES - 1, 1), pltpu.roll(y, 1, 1))
    return y * cos + partner * sin_signed


def _store_token_tiles(ref, val):
    n = val.shape[0]
    for j in range(TOK_ROWS):
        ref[pl.ds(j, n, stride=TOK_ROWS), :] = val[:, j * LANES:(j + 1) * LANES]


def _load_token_tiles(ref, n):
    return jnp.concatenate([ref[pl.ds(j, n, stride=TOK_ROWS), :] for j in range(TOK_ROWS)], axis=-1)


def _split_dot(a_f32, w_bf16):
    hi = a_f32.astype(BF16)
    lo = (a_f32 - hi.astype(F32)).astype(BF16)
    return (jnp.dot(hi, w_bf16, preferred_element_type=F32) + jnp.dot(lo, w_bf16, preferred_element_type=F32))


def _proj_kernel(x_ref, sh_ref, sc_ref, g_ref, w_ref, gain_ref, bd_ref, cos_ref, sin_ref, *out_refs, segs):
    a = _modulated_norm(x_ref[...], g_ref[...], sc_ref[0], sh_ref[0]).astype(BF16)
    cos = cos_ref[...]
    sin = sin_ref[...]
    for (start, width, mode, scale), o_ref in zip(segs, out_refs):
        y_seg = jnp.dot(a, w_ref[:, start:start + width], preferred_element_type=F32)
        if mode == "plain":
            o_ref[...] = y_seg.astype(o_ref.dtype)
            continue
        for j in range(width // LANES):
            y = y_seg[:, j * LANES:(j + 1) * LANES]
            if "norm" in mode:
                ms = _split_dot(y * y, bd_ref[...])
                y = y * lax.rsqrt(ms + EPS) * gain_ref[:, start + j * LANES:start + (j + 1) * LANES]
            if scale != 1.0:
                y = y * scale
            if "rope" in mode:
                y = _pair_rope(y, cos, sin)
            o_ref[:, j * LANES:(j + 1) * LANES] = y.astype(o_ref.dtype)


def _proj(x_all, mod_l, norm_g, w, gain, cos_t, sin_t, segs, out_dtypes, n_lat_tiles_per_batch, n_batch):
    n, d = x_all.shape
    tm = ROW_TILE
    n_tiles = n // tm
    wtot = w.shape[1]
    tpb = n_lat_tiles_per_batch
    n_lat_tiles = tpb * n_batch

    def mod_row(t):
        return jnp.where(t < n_lat_tiles, t // tpb, n_batch)

    def rope_row(t):
        return jnp.where(t < n_lat_tiles, t % tpb, tpb)

    bd = np.kron(np.eye(2, dtype=np.float32), np.full((HEAD_DIM, HEAD_DIM), 1.0 / HEAD_DIM, np.float32))
    in_specs = [
        pl.BlockSpec((tm, d), lambda t: (t, 0)),
        pl.BlockSpec((1, 1, d), lambda t: (mod_row(t), 0, 0)),
        pl.BlockSpec((1, 1, d), lambda t: (mod_row(t), 0, 1)),
        pl.BlockSpec((1, d), lambda t: (0, 0)),
        pl.BlockSpec((d, wtot), lambda t: (0, 0)),
        pl.BlockSpec((1, wtot), lambda t: (0, 0)),
        pl.BlockSpec((LANES, LANES), lambda t: (0, 0)),
        pl.BlockSpec((tm, LANES), lambda t: (rope_row(t), 0)),
        pl.BlockSpec((tm, LANES), lambda t: (rope_row(t), 0)),
    ]
    out_specs = [pl.BlockSpec((tm, s[1]), lambda t: (t, 0)) for s in segs]
    out_shape = [jax.ShapeDtypeStruct((n, s[1]), dt) for s, dt in zip(segs, out_dtypes)]
    return pl.pallas_call(
        functools.partial(_proj_kernel, segs=tuple(segs)),
        grid=(n_tiles,),
        in_specs=in_specs,
        out_specs=out_specs,
        out_shape=out_shape,
        compiler_params=_cp(("parallel",)),
    )(x_all, mod_l, mod_l, norm_g.reshape(1, d), w, gain, jnp.asarray(bd, BF16), cos_t, sin_t)


def _mla_proj_kernel(x_ref, wuq_ref, wukv_ref, gcq_ref, gckv_ref, gq_ref, gkn_ref, gkr_ref, cos_ref, sin_ref,
                     q_ref, k_ref, v_ref):
    x = x_ref[...]
    cq = x[:, :MLA_Q_LORA]
    ckv = x[:, MLA_Q_LORA:MLA_Q_LORA + MLA_KV_LORA]
    kr = x[:, MLA_Q_LORA + MLA_KV_LORA:MLA_Q_LORA + MLA_KV_LORA + LANES]
    cos = cos_ref[...]
    sin = sin_ref[...]
    cqn = cq * lax.rsqrt(jnp.mean(cq * cq, axis=-1, keepdims=True) + EPS) * gcq_ref[...]
    ckvn = ckv * lax.rsqrt(jnp.mean(ckv * ckv, axis=-1, keepdims=True) + EPS) * gckv_ref[...]
    q = jnp.dot(cqn.astype(BF16), wuq_ref[...], preferred_element_type=F32)
    kv = jnp.dot(ckvn.astype(BF16), wukv_ref[...], preferred_element_type=F32)
    nh = MLA_HEADS
    d_qk = float(MLA_NOPE + MLA_ROPE)
    lane = lax.broadcasted_iota(jnp.int32, (1, LANES), 1)
    low = lane < MLA_ROPE

    def half_sums(slab):
        sq = slab * slab
        a = jnp.sum(jnp.where(low, sq, 0.0), axis=-1, keepdims=True)
        return a, jnp.sum(sq, axis=-1, keepdims=True) - a

    rope_w = nh * MLA_NOPE
    q_rope_ss = []
    for r in range(nh // 2):
        q_rope_ss.extend(half_sums(q[:, rope_w + r * LANES:rope_w + (r + 1) * LANES]))
    kr_ss, _ = half_sums(kr)
    rs_q, rs_k = [], []
    for h in range(nh):
        qn = q[:, h * MLA_NOPE:(h + 1) * MLA_NOPE]
        kn = kv[:, h * MLA_NOPE:(h + 1) * MLA_NOPE]
        rs_q.append(lax.rsqrt((jnp.sum(qn * qn, axis=-1, keepdims=True) + q_rope_ss[h]) / d_qk + EPS))
        rs_k.append(lax.rsqrt((jnp.sum(kn * kn, axis=-1, keepdims=True) + kr_ss) / d_qk + EPS))
        q_ref[:, h * MLA_NOPE:(h + 1) * MLA_NOPE] = (qn * rs_q[h] * gq_ref[:, h * MLA_NOPE:(h + 1) * MLA_NOPE]).astype(q_ref.dtype)
        k_ref[:, h * MLA_NOPE:(h + 1) * MLA_NOPE] = (kn * rs_k[h] * gkn_ref[...]).astype(k_ref.dtype)
    kr_rot = _pair_rope(kr * gkr_ref[...], cos, sin)
    for r in range(nh // 2):
        sl = slice(rope_w + r * LANES, rope_w + (r + 1) * LANES)
        yq = q[:, sl] * jnp.where(low, rs_q[2 * r], rs_q[2 * r + 1]) * gq_ref[:, sl]
        q_ref[:, sl] = _pair_rope(yq, cos, sin).astype(q_ref.dtype)
        k_ref[:, sl] = (kr_rot * jnp.where(low, rs_k[2 * r], rs_k[2 * r + 1])).astype(k_ref.dtype)
    v_ref[...] = kv[:, nh * MLA_NOPE:].astype(v_ref.dtype)


def _mla_proj(mla_in, wuq, wukv, gcq, gckv, gq, gkn, gkr, cos_t, sin_t, tpb, n_batch):
    n, win = mla_in.shape
    tm = ROW_TILE
    n_lat_tiles = tpb * n_batch

    def rope_row(t):
        return jnp.where(t < n_lat_tiles, t % tpb, tpb)

    qk_w = MLA_HEADS * (MLA_NOPE + MLA_ROPE)
    v_w = MLA_HEADS * MLA_V
    full = lambda a: pl.BlockSpec(a.shape, lambda t: (0,) * a.ndim)
    return pl.pallas_call(
        _mla_proj_kernel,
        grid=(n // tm,),
        in_specs=[pl.BlockSpec((tm, win), lambda t: (t, 0)), full(wuq), full(wukv), full(gcq), full(gckv),
                  full(gq), full(gkn), full(gkr),
                  pl.BlockSpec((tm, LANES), lambda t: (rope_row(t), 0)),
                  pl.BlockSpec((tm, LANES), lambda t: (rope_row(t), 0))],
        out_specs=[pl.BlockSpec((tm, qk_w), lambda t: (t, 0)), pl.BlockSpec((tm, qk_w), lambda t: (t, 0)),
                   pl.BlockSpec((tm, v_w), lambda t: (t, 0))],
        out_shape=[jax.ShapeDtypeStruct((n, qk_w), BF16), jax.ShapeDtypeStruct((n, qk_w), BF16),
                   jax.ShapeDtypeStruct((n, v_w), BF16)],
        compiler_params=_cp(("parallel",)),
    )(mla_in, wuq, wukv, gcq, gckv, gq, gkn, gkr, cos_t, sin_t)


def _flash_kernel(qt_ref, k_ref, vt_ref, ot_ref, q_sc, s_0, s_1, s_2, mx_0, mx_1, mx_2, m_sc, acc_sc, *,
                  g, n_chunks, dv, n_tiles):
    bufs = ((s_0, mx_0), (s_1, mx_1), (s_2, mx_2))
    tq = qt_ref.shape[3] // n_tiles
    for t in range(n_tiles):
        for gi in range(g):
            q_sc[t, :, gi * tq:(gi + 1) * tq] = qt_ref[0, gi, :, t * tq:(t + 1) * tq]

    def reset():
        m_sc[...] = jnp.full(m_sc.shape, NEG, F32)
        acc_sc[...] = jnp.zeros(acc_sc.shape, F32)

    def finalize(t):
        acc = acc_sc[...]
        o = acc[:dv] / acc[dv:dv + 1]
        for gi in range(g):
            ot_ref[0, gi, :, t * tq:(t + 1) * tq] = o[:, gi * tq:(gi + 1) * tq].astype(ot_ref.dtype)

    def scores(t, ci, s_ref, mx_ref):
        s = jnp.dot(k_ref[0, ci], q_sc[t], preferred_element_type=F32)
        s_ref[...] = s
        mx_ref[...] = jnp.max(s, axis=0, keepdims=True)

    def accumulate(ci, s_ref, mx_ref):
        m_old = m_sc[...]
        m_new = jnp.maximum(m_old, mx_ref[...])
        alpha = jnp.exp2(m_old - m_new)
        p = jnp.exp2(s_ref[...] - m_new).astype(BF16)
        acc_sc[...] = alpha * acc_sc[...] + jnp.dot(vt_ref[0, ci], p, preferred_element_type=F32)
        m_sc[...] = m_new

    def fused(t_n, ci_n, s_n, mx_n, ci_c, s_c, mx_c):
        m_old = m_sc[...]
        m_new = jnp.maximum(m_old, mx_c[...])
        alpha = jnp.exp2(m_old - m_new)
        pv = None
        mx = None
        for k0 in range(0, s_c.shape[0], 256):
            s = jnp.dot(k_ref[0, ci_n, k0:k0 + 256, :], q_sc[t_n], preferred_element_type=F32)
            s_n[k0:k0 + 256, :] = s
            mxj = jnp.max(s, axis=0, keepdims=True)
            mx = mxj if mx is None else jnp.maximum(mx, mxj)
            p = jnp.exp2(s_c[k0:k0 + 256, :] - m_new).astype(BF16)
            part = jnp.dot(vt_ref[0, ci_c, :, k0:k0 + 256], p, preferred_element_type=F32)
            pv = part if pv is None else pv + part
        mx_n[...] = mx
        acc_sc[...] = alpha * acc_sc[...] + pv
        m_sc[...] = m_new

    reset()
    if n_chunks == 1:
        for t in range(n_tiles):
            scores(t, 0, *bufs[0])
            accumulate(0, *bufs[0])
            finalize(t)
            reset()
    else:
        scores(0, 0, *bufs[0])
        scores(0, 1, *bufs[1])
        n_fused = n_chunks - 2
        for t in range(n_tiles):
            off = t * n_chunks

            def triple(j, carry, t=t, off=off):
                c = 3 * j
                for r in range(3):
                    fused(t, c + r + 2, *bufs[(r + 2 + off) % 3], c + r, *bufs[(r + off) % 3])
                return carry

            lax.fori_loop(0, n_fused // 3, triple, 0)
            for c in range(3 * (n_fused // 3), n_fused):
                fused(t, c + 2, *bufs[(c + 2 + off) % 3], c, *bufs[(c + off) % 3])
            for c in (n_chunks - 2, n_chunks - 1):
                if t + 1 < n_tiles:
                    c_n = c - (n_chunks - 2)
                    fused(t + 1, c_n, *bufs[(c_n + off + n_chunks) % 3], c, *bufs[(c + off) % 3])
                else:
                    accumulate(c, *bufs[(c + off) % 3])
            finalize(t)
            reset()


def _key_chunk(lk):
    for tk in (1280, 1024, 768, 512, 256):
        if lk % tk == 0:
            return tk
    raise ValueError(f"key length {lk} must be a multiple of 256")


def _flash(qt, k, v, tq_blk):
    bk, g, dq, lq = qt.shape
    lk, dv = v.shape[1], v.shape[2]
    tk = _key_chunk(lk)
    nch = lk // tk
    kc = k.reshape(bk, nch, tk, dq)
    dve = dv + ONES_ROWS
    vt = jnp.concatenate([v, jnp.ones((bk, lk, ONES_ROWS), v.dtype)], axis=-1)
    vt = vt.reshape(bk, nch, tk, dve).transpose(0, 1, 3, 2)
    tq_blk = min(tq_blk, lq)
    n_tiles = FLASH_TILES if (nch > 1 and lq % (FLASH_TILES * tq_blk) == 0) else 1
    assert lq % tq_blk == 0 and tq_blk % LANES == 0
    w = g * tq_blk
    blk = n_tiles * tq_blk
    return pl.pallas_call(
        functools.partial(_flash_kernel, g=g, n_chunks=nch, dv=dv, n_tiles=n_tiles),
        grid=(bk, lq // blk),
        in_specs=[pl.BlockSpec((1, g, dq, blk), lambda b, i: (b, 0, 0, i)),
                  pl.BlockSpec((1, nch, tk, dq), lambda b, i: (b, 0, 0, 0)),
                  pl.BlockSpec((1, nch, dve, tk), lambda b, i: (b, 0, 0, 0))],
        out_specs=pl.BlockSpec((1, g, dv, blk), lambda b, i: (b, 0, 0, i)),
        out_shape=jax.ShapeDtypeStruct((bk, g, dv, lq), BF16),
        scratch_shapes=[pltpu.VMEM((n_tiles, dq, w), BF16)] + [pltpu.VMEM((tk, w), F32)] * 3
                       + [pltpu.VMEM((1, w), F32)] * 4 + [pltpu.VMEM((dve, w), F32)],
        compiler_params=_cp(("parallel", "parallel")),
    )(qt, kc, vt)


def _na_tables(seq_len):
    rows_n = seq_len // GRID_W
    assert rows_n >= NA_WIN_ROWS and NA_ROWS <= rows_n
    nb = seq_len // NA_QB
    rpq = NA_QB // GRID_W
    band = NA_WIN_ROWS * GRID_W
    variants, var_id, bases = {}, [], []
    for j in range(nb):
        base = int(np.clip(rpq * j - NA_ROWS // 2, 0, rows_n - NA_WIN_ROWS))
        bases.append(base)
        t = np.arange(NA_QB) + j * NA_QB
        r, col = t // GRID_W, t % GRID_W
        r0 = np.clip(r - NA_ROWS // 2, 0, rows_n - NA_ROWS)
        c0 = np.clip(col - NA_COLS // 2, 0, GRID_W - NA_COLS)
        kk = np.arange(band)
        kr = base + kk // GRID_W
        kc = kk % GRID_W
        inside = ((kr[None] >= r0[:, None]) & (kr[None] < r0[:, None] + NA_ROWS)
                  & (kc[None] >= c0[:, None]) & (kc[None] < c0[:, None] + NA_COLS))
        rel = (kr[None] - r[:, None] + NA_ROWS - 1) * (2 * NA_COLS - 1) + (kc[None] - col[:, None] + NA_COLS - 1)
        tab = np.where(inside, rel, -1).astype(np.int32)
        assert (inside.sum(axis=1) == NA_ROWS * NA_COLS).all()
        key = tab.tobytes()
        if key not in variants:
            variants[key] = (len(variants), tab)
        var_id.append(variants[key][0])
    tabs = np.stack([v[1] for v in sorted(variants.values(), key=lambda kv: kv[0])])
    n_dr = 2 * NA_ROWS - 1
    n_dc = 2 * NA_COLS - 1
    col = np.arange(GRID_W)
    c0 = np.clip(col - NA_COLS // 2, 0, GRID_W - NA_COLS)
    col_in = (col[None] >= c0[:, None]) & (col[None] < c0[:, None] + NA_COLS)
    dc = col[None] - col[:, None] + NA_COLS - 1
    t5 = tabs.reshape(len(tabs), rpq, GRID_W, NA_WIN_ROWS, GRID_W)
    row_sel = np.full((len(tabs), rpq, NA_WIN_ROWS), n_dr, np.int32)
    for v in range(len(tabs)):
        for a in range(rpq):
            for i in range(NA_WIN_ROWS):
                blk = t5[v, a, :, i, :]
                if (blk >= 0).any():
                    dr = int(blk[blk >= 0][0]) // n_dc
                    assert (np.where(col_in, dr * n_dc + dc, -1) == blk).all()
                    row_sel[v, a, i] = dr
                else:
                    assert (blk < 0).all()
    dc_onehot = (dc[None] == np.arange(n_dc)[:, None, None]).astype(np.float32)
    return np.asarray(bases, np.int32), np.asarray(var_id, np.int32), row_sel, col_in, dc_onehot


def _na_bias(rpb, row_sel, col_in, dc_onehot):
    h = rpb.shape[0]
    n_dr, n_dc = 2 * NA_ROWS - 1, 2 * NA_COLS - 1
    t = jnp.einsum("hdj,jck->hdck", rpb.astype(F32).reshape(h, n_dr, n_dc) * LOG2E, jnp.asarray(dc_onehot),
                   precision=lax.Precision.HIGHEST)
    t = jnp.where(jnp.asarray(col_in)[None, None], t, NEG)
    t = jnp.concatenate([t, jnp.full((h, 1, GRID_W, GRID_W), NEG, F32)], axis=1)
    nv, rpq, nw = row_sel.shape
    b = t[:, jnp.asarray(row_sel)]
    return b.transpose(1, 0, 2, 4, 3, 5).reshape(nv, h, rpq * GRID_W, nw * GRID_W)


def _na_kernel(base_ref, var_ref, q_ref, k_ref, v_ref, kc_ref, vc_ref, bias_ref, o_ref):
    j = pl.program_id(2)
    band = NA_WIN_ROWS * GRID_W
    start = pl.multiple_of(base_ref[j] * GRID_W, GRID_W)
    q = q_ref[...]
    kw = k_ref[pl.ds(start, band), :]
    vw = v_ref[pl.ds(start, band), :]
    kc = kc_ref[...]
    vc = vc_ref[...]
    nt = (((1,), (1,)), ((), ()))
    lane = lax.broadcasted_iota(jnp.int32, q.shape, 1)
    outs = []
    for hh in range(LANES // HEAD_DIM):
        mine = (lane >= hh * HEAD_DIM) & (lane < (hh + 1) * HEAD_DIM)
        qh = jnp.where(mine, q, jnp.zeros_like(q))
        s_win = lax.dot_general(qh, kw, nt, preferred_element_type=F32) + bias_ref[0, hh]
        s_ctx = lax.dot_general(qh, kc, nt, preferred_element_type=F32)
        m = jnp.maximum(jnp.max(s_win, axis=-1, keepdims=True), jnp.max(s_ctx, axis=-1, keepdims=True))
        p_win = jnp.exp2(s_win - m)
        p_ctx = jnp.exp2(s_ctx - m)
        l = jnp.sum(p_win, axis=-1, keepdims=True) + jnp.sum(p_ctx, axis=-1, keepdims=True)
        o = (jnp.dot(p_win.astype(BF16), vw, preferred_element_type=F32)
             + jnp.dot(p_ctx.astype(BF16), vc, preferred_element_type=F32))
        outs.append(o / l)
    o_ref[...] = jnp.where(lane < HEAD_DIM, outs[0], outs[1]).astype(o_ref.dtype)


def _na_attention(q, k, v, bias, bases, var_id, n_batch, seq_len, ctx_len):
    w = q.shape[1]
    assert LANES // HEAD_DIM == 2 and ctx_len % 8 == 0 and (n_batch * seq_len) % ctx_len == 0
    band = NA_WIN_ROWS * GRID_W
    nb = seq_len // NA_QB
    ctx0 = (n_batch * seq_len) // ctx_len
    lat_spec = pl.BlockSpec((seq_len, LANES), lambda b, p, j, bs, vr: (b, p))
    ctx_spec = pl.BlockSpec((ctx_len, LANES), lambda b, p, j, bs, vr: (ctx0 + b, p))
    grid_spec = pltpu.PrefetchScalarGridSpec(
        num_scalar_prefetch=2,
        grid=(n_batch, w // LANES, nb),
        in_specs=[pl.BlockSpec((NA_QB, LANES), lambda b, p, j, bs, vr: (b * nb + j, p)),
                  lat_spec, lat_spec, ctx_spec, ctx_spec,
                  pl.BlockSpec((1, LANES // HEAD_DIM, NA_QB, band), lambda b, p, j, bs, vr: (vr[j], p, 0, 0))],
        out_specs=pl.BlockSpec((NA_QB, LANES), lambda b, p, j, bs, vr: (b * nb + j, p)),
    )
    return pl.pallas_call(
        _na_kernel,
        grid_spec=grid_spec,
        out_shape=jax.ShapeDtypeStruct((n_batch * seq_len, w), BF16),
        compiler_params=_cp(("parallel", "parallel", "arbitrary")),
    )(bases, var_id, q, k, v, k, v, bias)


def _ret_kernel(cdec_ref, qf_ref, kf_ref, vf_ref, qb_ref, kb_ref, vb_ref, dmask_ref, qdec_ref, kdec_ref,
                yf_ref, yb_ref, state_sc):
    @pl.when(pl.program_id(1) == 0)
    def _():
        state_sc[...] = jnp.zeros(state_sc.shape, F32)

    nt = (((1,), (1,)), ((), ()))
    tn = (((0,), (0,)), ((), ()))
    dk = RET_DK
    for d, (q_ref, k_ref, v_ref, y_ref) in enumerate(((qf_ref, kf_ref, vf_ref, yf_ref),
                                                      (qb_ref, kb_ref, vb_ref, yb_ref))):
        for h in range(RET_HEADS):
            sl = slice(h * dk, (h + 1) * dk)
            q = q_ref[:, sl]
            k = k_ref[:, sl]
            v = v_ref[:, sl]
            st = state_sc[d, h]
            a = lax.dot_general(q, k, nt, preferred_element_type=F32) * dmask_ref[d, h]
            inner = jnp.dot(a.astype(BF16), v, preferred_element_type=F32)
            cross = jnp.dot(q, st.astype(BF16), preferred_element_type=F32) * qdec_ref[d, h]
            y_ref[:, sl] = inner + cross
            vs = (v.astype(F32) * kdec_ref[d, h]).astype(BF16)
            state_sc[d, h] = st * cdec_ref[d * RET_HEADS + h] + lax.dot_general(k, vs, tn, preferred_element_type=F32)


def _retention(rq, rk, rv, dmask, qdec, kdec, cdec, n_batch, seq_len, ctx_len):
    n, w = rq.shape
    c = RET_CHUNK
    assert ctx_len == c and seq_len % c == 0
    ncl = seq_len // c
    ctx_blk0 = (n_batch * seq_len) // c

    def fwd(b, s, cd):
        return (jnp.where(s == 0, ctx_blk0 + b, b * ncl + s - 1), 0)

    def bwd(b, s, cd):
        return (jnp.where(s == 0, ctx_blk0 + b, b * ncl + ncl - s), 0)

    full = lambda a: pl.BlockSpec(a.shape, lambda b, s, cd: (0,) * a.ndim)
    grid_spec = pltpu.PrefetchScalarGridSpec(
        num_scalar_prefetch=1,
        grid=(n_batch, ncl + 1),
        in_specs=[pl.BlockSpec((c, w), fwd)] * 3 + [pl.BlockSpec((c, w), bwd)] * 3 + [full(dmask), full(qdec), full(kdec)],
        out_specs=[pl.BlockSpec((c, w), fwd), pl.BlockSpec((c, w), bwd)],
        scratch_shapes=[pltpu.VMEM((2, RET_HEADS, RET_DK, RET_DK), F32)],
    )
    return pl.pallas_call(
        _ret_kernel,
        grid_spec=grid_spec,
        out_shape=[jax.ShapeDtypeStruct((n, w), F32)] * 2,
        compiler_params=_cp(("parallel", "arbitrary")),
    )(cdec, rq, rk, rv, rq, rk, rv, dmask, qdec, kdec)


def _ret_finish_kernel(yf_ref, yb_ref, rg_ref, gn_ref, o_ref):
    y = yf_ref[...] + yb_ref[...]
    gate = rg_ref[...]
    gate = gate * jax.nn.sigmoid(gate)
    for h in range(RET_HEADS):
        sl = slice(h * RET_DK, (h + 1) * RET_DK)
        yh = y[:, sl]
        mu = jnp.mean(yh, axis=-1, keepdims=True)
        var = jnp.mean(jnp.square(yh - mu), axis=-1, keepdims=True)
        o_ref[:, sl] = ((yh - mu) * lax.rsqrt(var + GN_EPS) * gn_ref[:, sl] * gate[:, sl]).astype(o_ref.dtype)


def _ret_finish(yf, yb, rg, gn):
    n, w = yf.shape
    tm = ROW_TILE
    spec = pl.BlockSpec((tm, w), lambda t: (t, 0))
    return pl.pallas_call(
        _ret_finish_kernel,
        grid=(n // tm,),
        in_specs=[spec, spec, spec, pl.BlockSpec((1, w), lambda t: (0, 0))],
        out_specs=spec,
        out_shape=jax.ShapeDtypeStruct((n, w), BF16),
        compiler_params=_cp(("parallel",)),
    )(yf, yb, rg, gn)


def _out_kernel(a1_ref, a2_ref, x_ref, w_ref, g1_ref, ng_ref, sh_ref, sc_ref, wr_ref, br_ref,
                xo_ref, m_ref, e_ref, gt_ref):
    half = a1_ref.shape[1]
    o = (jnp.dot(a1_ref[...], w_ref[:half, :], preferred_element_type=F32)
         + jnp.dot(a2_ref[...], w_ref[half:, :], preferred_element_type=F32))
    x = x_ref[...] + g1_ref[0] * o
    xo_ref[...] = x
    m = _modulated_norm(x, ng_ref[...], sc_ref[0], sh_ref[0])
    _store_token_tiles(m_ref, m)
    m_hi = m.astype(BF16)
    m_lo = (m - m_hi.astype(F32)).astype(BF16)
    hi_prod = jnp.dot(m_hi, wr_ref[...], preferred_element_type=F32)
    logits = (hi_prod[:, :LANES] + hi_prod[:, LANES:]
              + jnp.dot(m_lo, wr_ref[:, :LANES], preferred_element_type=F32) + br_ref[...])
    lane = lax.broadcasted_iota(jnp.int32, logits.shape, 1).astype(F32)
    e_out = jnp.zeros(logits.shape, F32)
    g_out = jnp.zeros(logits.shape, F32)
    top0 = None
    denom = None
    for kk in range(TOP_K):
        mx = jnp.max(logits, axis=-1, keepdims=True)
        idx = jnp.min(jnp.where(logits == mx, lane, float(LANES)), axis=-1, keepdims=True)
        if kk == 0:
            top0 = mx
            ex = jnp.ones_like(mx)
            denom = ex
        else:
            ex = jnp.exp(mx - top0)
            denom = denom + ex
        e_out = jnp.where(lane == kk, idx, e_out)
        g_out = jnp.where(lane == kk, ex, g_out)
        logits = jnp.where(lane == idx, NEG * 2.0, logits)
    e_ref[...] = e_out.astype(jnp.int32)
    gt_ref[...] = g_out / denom


def _out_proj(a1, a2, x_all, w_out, mod_l, norm2_g, w_r, b_r, n_rows, tpb, n_batch):
    d = x_all.shape[1]
    half = a1.shape[1]
    tm = ROW_TILE
    n_lat_tiles = tpb * n_batch

    def mod_row(t):
        return jnp.where(t < n_lat_tiles, t // tpb, n_batch)

    row = lambda wd: pl.BlockSpec((tm, wd), lambda t: (t, 0))
    modspec = lambda col: pl.BlockSpec((1, 1, d), lambda t: (mod_row(t), 0, col))
    return pl.pallas_call(
        _out_kernel,
        grid=(n_rows // tm,),
        in_specs=[row(half), row(half), row(d), pl.BlockSpec((2 * half, d), lambda t: (0, 0)),
                  modspec(2), pl.BlockSpec((1, d), lambda t: (0, 0)), modspec(3), modspec(4),
                  pl.BlockSpec((d, 2 * LANES), lambda t: (0, 0)), pl.BlockSpec((1, LANES), lambda t: (0, 0))],
        out_specs=[row(d), pl.BlockSpec((tm * TOK_ROWS, LANES), lambda t: (t, 0)), row(LANES), row(LANES)],
        out_shape=[jax.ShapeDtypeStruct((n_rows, d), F32), jax.ShapeDtypeStruct((n_rows * TOK_ROWS, LANES), F32),
                   jax.ShapeDtypeStruct((n_rows, LANES), jnp.int32), jax.ShapeDtypeStruct((n_rows, LANES), F32)],
        compiler_params=_cp(("parallel",)),
    )(a1, a2, x_all, w_out, mod_l, norm2_g.reshape(1, d), mod_l, mod_l, w_r, b_r)


def _token_rows(ref, idx):
    return ref.at[pl.ds(pl.multiple_of(idx * TOK_ROWS, TOK_ROWS), TOK_ROWS)]


def _dispatch_kernel(pad_start_ref, pad_cnt_ref, pos_ref, m_ref, hs_ref, zero_sc, sem, pad_sem):
    n_tok = m_ref.shape[0] // TOK_ROWS

    @pl.when(pl.program_id(0) == 0)
    def _():
        zero_sc[...] = jnp.zeros(zero_sc.shape, zero_sc.dtype)

        def per_expert(e, carry):
            def fill(r, c):
                pltpu.make_async_copy(zero_sc, _token_rows(hs_ref, pad_start_ref[e] + r), pad_sem).start()
                return c

            lax.fori_loop(0, pad_cnt_ref[e], fill, 0)

            def fill_done(r, c):
                pltpu.make_async_copy(zero_sc, _token_rows(hs_ref, 0), pad_sem).wait()
                return c

            lax.fori_loop(0, pad_cnt_ref[e], fill_done, 0)
            return carry

        lax.fori_loop(0, N_EXPERTS, per_expert, 0)

    def issue(r, carry):
        for kk in range(TOP_K):
            pltpu.make_async_copy(_token_rows(m_ref, r), _token_rows(hs_ref, pos_ref[r * TOP_K + kk]),
                                  sem).start(priority=kk % 2)
        return carry

    lax.fori_loop(0, n_tok, issue, 0, unroll=2)

    for kk in range(TOP_K):
        pltpu.make_async_copy(m_ref, hs_ref.at[pl.ds(0, n_tok * TOK_ROWS)], sem).wait()


def _dispatch(m_tiles, pos, pad_start, pad_cnt, cap):
    n_tok = m_tiles.shape[0] // TOK_ROWS
    tm = ROW_TILE
    grid_spec = pltpu.PrefetchScalarGridSpec(
        num_scalar_prefetch=2,
        grid=(n_tok // tm,),
        in_specs=[pl.BlockSpec((tm * TOP_K,), lambda t, ps, pc: (t,), memory_space=pltpu.SMEM),
                  pl.BlockSpec((tm * TOK_ROWS, LANES), lambda t, ps, pc: (t, 0))],
        out_specs=pl.BlockSpec(memory_space=pl.ANY),
        scratch_shapes=[pltpu.VMEM((TOK_ROWS, LANES), F32), pltpu.SemaphoreType.DMA(()), pltpu.SemaphoreType.DMA(())],
    )
    return pl.pallas_call(
        _dispatch_kernel,
        grid_spec=grid_spec,
        out_shape=jax.ShapeDtypeStruct((cap * TOK_ROWS, LANES), F32),
        compiler_params=pltpu.CompilerParams(dimension_semantics=("arbitrary",), has_side_effects=True,
                                             vmem_limit_bytes=VMEM_LIMIT),
    )(pad_start, pad_cnt, pos, m_tiles)


def _expert_kernel(be_ref, nu_ref, x_ref, w1_ref, b1_ref, w2_ref, b2_ref, o_ref, w1_sc, w2_sc):
    i = pl.program_id(0)

    @pl.when((i == 0) | (be_ref[i] != be_ref[jnp.maximum(i - 1, 0)]))
    def _():
        w1_sc[...] = w1_ref[0].astype(BF16)
        w2_sc[...] = w2_ref[0].astype(BF16)

    @pl.when(i < nu_ref[0])
    def _():
        dff = w2_ref.shape[1]
        x = _load_token_tiles(x_ref, MOE_BM).astype(BF16)
        u = jnp.dot(x, w1_sc[...], preferred_element_type=F32) + b1_ref[0]
        gl = jnp.minimum(u[:, :dff], SWIGLU_LIMIT)
        up = jnp.clip(u[:, dff:], -SWIGLU_LIMIT, SWIGLU_LIMIT)
        act = gl * jax.nn.sigmoid(SWIGLU_ALPHA * gl) * (up + 1.0)
        _store_token_tiles(o_ref, jnp.dot(act.astype(BF16), w2_sc[...], preferred_element_type=F32) + b2_ref[0])

    @pl.when(i >= nu_ref[0])
    def _():
        o_ref[...] = jnp.zeros(o_ref.shape, o_ref.dtype)


def _experts(hs, blk_e, n_used, w1_all, b1, w2_all, b2, layer):
    _, ne, d, dff2 = w1_all.shape
    cap = hs.shape[0] // TOK_ROWS
    dff = dff2 // 2
    bm = MOE_BM
    tile_spec = pl.BlockSpec((bm * TOK_ROWS, LANES), lambda i, be, nu: (i, 0))
    grid_spec = pltpu.PrefetchScalarGridSpec(
        num_scalar_prefetch=2,
        grid=(cap // bm,),
        in_specs=[tile_spec,
                  pl.BlockSpec((None, 1, d, dff2), lambda i, be, nu: (layer, be[i], 0, 0)),
                  pl.BlockSpec((1, 1, dff2), lambda i, be, nu: (be[i], 0, 0)),
                  pl.BlockSpec((None, 1, dff, d), lambda i, be, nu: (layer, be[i], 0, 0)),
                  pl.BlockSpec((1, 1, d), lambda i, be, nu: (be[i], 0, 0))],
        out_specs=tile_spec,
        scratch_shapes=[pltpu.VMEM((d, dff2), BF16), pltpu.VMEM((dff, d), BF16)],
    )
    return pl.pallas_call(
        _expert_kernel,
        grid_spec=grid_spec,
        out_shape=jax.ShapeDtypeStruct(hs.shape, F32),
        compiler_params=_cp(("arbitrary",)),
    )(blk_e, n_used, hs, w1_all, b1.reshape(ne, 1, dff2), w2_all, b2.reshape(ne, 1, d))


def _combine_kernel(pos_ref, pos_next_ref, x_ref, gt_ref, g2_ref, out_ref, o_ref, y_sc, sem):
    t = pl.program_id(0)
    n_tok = x_ref.shape[0]
    slot = t % 2

    def gather(p_ref, sl):
        def issue(r, carry):
            for kk in range(TOP_K):
                pltpu.make_async_copy(_token_rows(out_ref, p_ref[r * TOP_K + kk]),
                                      _token_rows(y_sc.at[sl * TOP_K + kk], r), sem.at[sl]).start(priority=kk % 2)
            return carry

        lax.fori_loop(0, n_tok, issue, 0, unroll=2)

    @pl.when(t == 0)
    def _():
        gather(pos_ref, 0)

    @pl.when(t + 1 < pl.num_programs(0))
    def _():
        gather(pos_next_ref, 1 - slot)

    for kk in range(TOP_K):
        pltpu.make_async_copy(out_ref.at[pl.ds(0, n_tok * TOK_ROWS)], y_sc.at[slot * TOP_K + kk], sem.at[slot]).wait()
    gt = gt_ref[...]
    for j in range(TOK_ROWS):
        sl = slice(j * LANES, (j + 1) * LANES)
        acc = y_sc[slot * TOP_K, pl.ds(j, n_tok, stride=TOK_ROWS), :] * gt[:, 0:1]
        for kk in range(1, TOP_K):
            acc = acc + y_sc[slot * TOP_K + kk, pl.ds(j, n_tok, stride=TOK_ROWS), :] * gt[:, kk:kk + 1]
        o_ref[:, sl] = x_ref[:, sl] + g2_ref[0, :, sl] * acc


def _combine(x, out_tiles, pos, gates, mod_l, tpb, n_batch):
    n, d = x.shape
    tm = ROW_TILE
    n_lat_tiles = tpb * n_batch

    def mod_row(t):
        return jnp.where(t < n_lat_tiles, t // tpb, n_batch)

    n_tiles = n // tm
    return pl.pallas_call(
        _combine_kernel,
        grid=(n_tiles,),
        in_specs=[pl.BlockSpec((tm * TOP_K,), lambda t: (t,), memory_space=pltpu.SMEM),
                  pl.BlockSpec((tm * TOP_K,), lambda t: (jnp.minimum(t + 1, n_tiles - 1),), memory_space=pltpu.SMEM),
                  pl.BlockSpec((tm, d), lambda t: (t, 0)), pl.BlockSpec((tm, LANES), lambda t: (t, 0)),
                  pl.BlockSpec((1, 1, d), lambda t: (mod_row(t), 0, 5)), pl.BlockSpec(memory_space=pl.ANY)],
        out_specs=pl.BlockSpec((tm, d), lambda t: (t, 0)),
        out_shape=jax.ShapeDtypeStruct((n, d), F32),
        scratch_shapes=[pltpu.VMEM((2 * TOP_K, tm * TOK_ROWS, LANES), F32), pltpu.SemaphoreType.DMA((2,))],
        compiler_params=_cp(("arbitrary",)),
    )(pos, pos, x, gates, mod_l, out_tiles)


def _moe(m_tiles, top_e, gates, x, mod_l, w1_all, b1, w2_all, b2, layer, tpb, n_batch):
    n, d = x.shape
    nk = n * TOP_K
    bm = MOE_BM
    n_blk = (nk + N_EXPERTS * (bm - 1)) // bm + 1
    cap = n_blk * bm
    flat_e = top_e[:, :TOP_K].reshape(nk)
    onehot = (flat_e[:, None] == jnp.arange(N_EXPERTS, dtype=jnp.int32)[None, :]).astype(jnp.int32)
    csum = jnp.cumsum(onehot, axis=0)
    rank = jnp.sum(jnp.where(onehot > 0, csum, 0), axis=1) - 1
    counts = csum[-1]
    padded = ((counts + bm - 1) // bm) * bm
    pend = jnp.cumsum(padded)
    pstart = pend - padded
    pos = (pstart[flat_e] + rank).astype(jnp.int32)
    n_used = (pend[-1] // bm).astype(jnp.int32)
    blk = jnp.minimum(jnp.arange(n_blk, dtype=jnp.int32), n_used - 1)
    blk_e = jnp.sum((pend[None, :] <= (blk * bm)[:, None]).astype(jnp.int32), axis=1)
    blk_e = jnp.clip(blk_e, 0, N_EXPERTS - 1).astype(jnp.int32)
    hs = _dispatch(m_tiles, pos, (pstart + counts).astype(jnp.int32), (padded - counts).astype(jnp.int32), cap)
    out = _experts(hs, blk_e, n_used.reshape(1), w1_all, b1, w2_all, b2, layer)
    return _combine(x, out, pos, gates, mod_l, tpb, n_batch)


def _rope_tables(seq_len, d_rot, reps, n_extra):
    t = jnp.arange(seq_len)
    rows = (t // GRID_W).astype(F32)
    cols = (t % GRID_W).astype(F32)
    n_freq = d_rot // 4
    inv = ROPE_THETA ** (-jnp.arange(n_freq, dtype=F32) / n_freq)
    ang = jnp.concatenate([rows[:, None] * inv, cols[:, None] * inv], axis=-1)
    cos = jnp.repeat(jnp.cos(ang), 2, axis=-1)
    sin = jnp.repeat(jnp.sin(ang), 2, axis=-1) * jnp.tile(jnp.asarray([-1.0, 1.0], F32), d_rot // 2)
    cos = jnp.concatenate([jnp.tile(cos, (1, reps)), jnp.ones((n_extra, d_rot * reps), F32)], axis=0)
    sin = jnp.concatenate([jnp.tile(sin, (1, reps)), jnp.zeros((n_extra, d_rot * reps), F32)], axis=0)
    return cos, sin


def _retention_tables(decay_logit):
    log_g = jax.nn.log_sigmoid(decay_logit.astype(F32))
    c = RET_CHUNK
    pos = jnp.arange(c, dtype=F32)
    diff = pos[:, None] - pos[None, :]
    lf = log_g[0][:, None, None]
    lb = log_g[1][:, None, None]
    dm_f = jnp.where(diff >= 0, jnp.exp(lf * jnp.where(diff >= 0, diff, 0.0)), 0.0)
    dm_b = jnp.where(diff < 0, jnp.exp(lb * jnp.where(diff < 0, -diff, 0.0)), 0.0)
    qd_f = jnp.exp(log_g[0][:, None] * (pos + 1.0))
    qd_b = jnp.exp(log_g[1][:, None] * (c - pos))
    kd_f = jnp.exp(log_g[0][:, None] * (c - 1.0 - pos))
    kd_b = jnp.exp(log_g[1][:, None] * pos)
    bc = lambda a: jnp.broadcast_to(a[..., None], a.shape + (LANES,))
    dmask = jnp.stack([dm_f, dm_b])
    qdec = jnp.stack([bc(qd_f), bc(qd_b)])
    kdec = jnp.stack([bc(kd_f), bc(kd_b)])
    cdec = jnp.exp(log_g * c).reshape(-1)
    return dmask, qdec, kdec, cdec


def _heads_major(t, n_batch, length, n_heads):
    return t.reshape(n_batch, length, n_heads, -1).transpose(0, 2, 1, 3)


def _heads_t(t, n_batch, length, n_heads):
    return t.reshape(n_batch, length, n_heads, -1).transpose(0, 2, 3, 1)


def kernel(x, c, ctx, c_ctx, norm1_g, norm2_g, w_mod, b_mod, w_in_even, w_out_even, a_q_norm, a_k_norm, b_q_norm, b_k_norm, b_rpb, w_in_odd, w_out_odd, ret_decay, ret_gn, mla_cq_norm, mla_ckv_norm, w_uq, w_ukv, mla_q_norm, mla_k_norm, w_router, b_router, w_exp1, b_exp1, w_exp2, b_exp2):
    bsz, s, d = x.shape
    cl = ctx.shape[1]
    depth = w_mod.shape[0]
    tm = ROW_TILE
    assert s % tm == 0 and (bsz * cl) == tm and s % NA_QB == 0 and bsz + 1 <= 8 and d == TOK_ROWS * LANES
    tpb = s // tm
    nl = bsz * s
    n_all = nl + bsz * cl

    c_rows = jnp.zeros((8, d), F32).at[:bsz].set(c).at[bsz].set(c_ctx)
    mod = _mod_vectors(c_rows, w_mod, b_mod)

    cos_a, sin_a = _rope_tables(s, HEAD_DIM, LANES // HEAD_DIM, tm)
    cos_c, sin_c = _rope_tables(s, RET_DK, 1, tm)
    cos_d, sin_d = _rope_tables(s, MLA_ROPE, LANES // MLA_ROPE, tm)
    na_bases, na_var, na_row_sel, na_col_in, na_dc_onehot = _na_tables(s)

    x_all = jnp.concatenate([x.reshape(nl, d), ctx.reshape(bsz * cl, d)], axis=0)
    lat = lambda t: t[:nl]
    cx = lambda t: t[nl:]
    att_scale = HEAD_DIM ** -0.5 * LOG2E
    grp = GQA_HEADS // GQA_KV_HEADS
    bkv = bsz * GQA_KV_HEADS

    for l in range(depth):
        need_ctx = l < depth - 1
        i = l // 2
        mod_l = mod[l].reshape(8, 1, 6 * d)
        if l % 2 == 0:
            wq, wk, wv, wn = GQA_HEADS * HEAD_DIM, GQA_KV_HEADS * HEAD_DIM, GQA_KV_HEADS * HEAD_DIM, NA_HEADS * HEAD_DIM
            starts = np.cumsum([0, wq, wk, wv, wn, wn])
            segs = [(int(starts[0]), wq, "norm_rope", 1.0), (int(starts[1]), wk, "norm_rope", 1.0),
                    (int(starts[2]), wv, "plain", 1.0), (int(starts[3]), wn, "norm", 1.0),
                    (int(starts[4]), wn, "norm", 1.0), (int(starts[5]), wn, "plain", 1.0)]
            gain = jnp.concatenate([jnp.tile(a_q_norm[i], GQA_HEADS) * att_scale, jnp.tile(a_k_norm[i], GQA_KV_HEADS),
                                    jnp.ones((wv,), F32), jnp.tile(b_q_norm[i], NA_HEADS) * att_scale,
                                    jnp.tile(b_k_norm[i], NA_HEADS), jnp.ones((wn,), F32)]).reshape(1, -1)
            qa, ka, va, qb, kb, vb = _proj(x_all, mod_l, norm1_g[l], w_in_even[i].astype(BF16), gain, cos_a, sin_a,
                                           segs, [BF16] * 6, tpb, bsz)
            k_lat = _heads_major(lat(ka), bsz, s, GQA_KV_HEADS)
            k_cx = _heads_major(cx(ka), bsz, cl, GQA_KV_HEADS)
            v_lat = _heads_major(lat(va), bsz, s, GQA_KV_HEADS)
            v_cx = _heads_major(cx(va), bsz, cl, GQA_KV_HEADS)
            k_all = jnp.concatenate([k_lat, k_cx], axis=2).reshape(bkv, s + cl, HEAD_DIM)
            v_all = jnp.concatenate([v_lat, v_cx], axis=2).reshape(bkv, s + cl, HEAD_DIM)
            qt = _heads_t(lat(qa), bsz, s, GQA_HEADS).reshape(bkv, grp, HEAD_DIM, s)
            oa_t = _flash(qt, k_all, v_all, Q_SUB)
            oa = oa_t.reshape(bsz, GQA_HEADS, HEAD_DIM, s).transpose(0, 3, 1, 2).reshape(nl, wq)
            bias = _na_bias(b_rpb[i], na_row_sel, na_col_in, na_dc_onehot)
            bh = bsz * NA_HEADS
            ob = _na_attention(qb, kb, vb, bias, jnp.asarray(na_bases), jnp.asarray(na_var), bsz, s, cl)
            kn_c = _heads_major(cx(kb), bsz, cl, NA_HEADS).reshape(bh, cl, HEAD_DIM)
            vn_c = _heads_major(cx(vb), bsz, cl, NA_HEADS).reshape(bh, cl, HEAD_DIM)
            if need_ctx:
                qt_c = _heads_t(cx(qa), bsz, cl, GQA_HEADS).reshape(bkv, grp, HEAD_DIM, cl)
                oa_c = _flash(qt_c, k_cx.reshape(bkv, cl, HEAD_DIM), v_cx.reshape(bkv, cl, HEAD_DIM), Q_SUB)
                oa_c = oa_c.reshape(bsz, GQA_HEADS, HEAD_DIM, cl).transpose(0, 3, 1, 2).reshape(bsz * cl, wq)
                qnt_c = _heads_t(cx(qb), bsz, cl, NA_HEADS).reshape(bh, 1, HEAD_DIM, cl)
                ob_c = _flash(qnt_c, kn_c, vn_c, Q_SUB)
                ob_c = ob_c.reshape(bsz, NA_HEADS, HEAD_DIM, cl).transpose(0, 3, 1, 2).reshape(bsz * cl, wn)
                a1 = jnp.concatenate([oa, oa_c], axis=0)
                a2 = jnp.concatenate([ob, ob_c], axis=0)
            else:
                a1, a2 = oa, ob
            w_out = w_out_even[i].astype(BF16)
        else:
            rw = RET_HEADS * RET_DK
            kr_cols = w_in_odd[i][:, 4 * rw + MLA_Q_LORA + MLA_KV_LORA:]
            w_ext = jnp.concatenate([w_in_odd[i]] + [kr_cols] * (LANES * 2 // MLA_ROPE - 1), axis=1).astype(BF16)
            mla_w = MLA_Q_LORA + MLA_KV_LORA + 2 * LANES
            segs = [(0, rw, "rope", RET_DK ** -0.5), (rw, rw, "rope", 1.0), (2 * rw, rw, "plain", 1.0),
                    (3 * rw, rw, "plain", 1.0), (4 * rw, mla_w, "plain", 1.0)]
            gain = jnp.ones((1, w_ext.shape[1]), F32)
            rq, rk, rv, rg, mla_in = _proj(x_all, mod_l, norm1_g[l], w_ext, gain, cos_c, sin_c, segs,
                                           [BF16, BF16, BF16, F32, F32], tpb, bsz)
            dmask, qdec, kdec, cdec = _retention_tables(ret_decay[i])
            yf, yb = _retention(rq, rk, rv, dmask, qdec, kdec, cdec, bsz, s, cl)
            a1 = _ret_finish(yf, yb, rg, ret_gn[i].reshape(1, rw))
            dqk = MLA_NOPE + MLA_ROPE
            perm_q = np.concatenate([np.arange(h * dqk, h * dqk + MLA_NOPE) for h in range(MLA_HEADS)]
                                    + [np.arange(h * dqk + MLA_NOPE, (h + 1) * dqk) for h in range(MLA_HEADS)])
            dkv = MLA_NOPE + MLA_V
            perm_kv = np.concatenate([np.arange(h * dkv, h * dkv + MLA_NOPE) for h in range(MLA_HEADS)]
                                     + [np.arange(h * dkv + MLA_NOPE, (h + 1) * dkv) for h in range(MLA_HEADS)])
            mla_scale = dqk ** -0.5 * LOG2E
            gq = jnp.concatenate([jnp.tile(mla_q_norm[i][:MLA_NOPE], MLA_HEADS),
                                  jnp.tile(mla_q_norm[i][MLA_NOPE:], MLA_HEADS)]).reshape(1, -1) * mla_scale
            gkn = mla_k_norm[i][:MLA_NOPE].reshape(1, -1)
            gkr = jnp.tile(mla_k_norm[i][MLA_NOPE:], LANES // MLA_ROPE).reshape(1, -1)
            q_m, k_m, v_m = _mla_proj(mla_in, w_uq[i][:, perm_q].astype(BF16), w_ukv[i][:, perm_kv].astype(BF16),
                                      mla_cq_norm[i].reshape(1, -1), mla_ckv_norm[i].reshape(1, -1), gq, gkn, gkr,
                                      cos_d, sin_d, tpb, bsz)
            nw = MLA_HEADS * MLA_NOPE

            def qk_heads(t, length, transposed):
                nope = t[:, :nw].reshape(bsz, length, MLA_HEADS, MLA_NOPE)
                rope = t[:, nw:].reshape(bsz, length, MLA_HEADS, MLA_ROPE)
                full = jnp.concatenate([nope, rope], axis=-1)
                return full.transpose(0, 2, 3, 1) if transposed else full.transpose(0, 2, 1, 3)

            bhm = bsz * MLA_HEADS
            k_lat = qk_heads(lat(k_m), s, False)
            k_cx = qk_heads(cx(k_m), cl, False)
            v_lat = _heads_major(lat(v_m), bsz, s, MLA_HEADS)
            v_cx = _heads_major(cx(v_m), bsz, cl, MLA_HEADS)
            k_all = jnp.concatenate([k_lat, k_cx], axis=2).reshape(bhm, s + cl, dqk)
            v_all = jnp.concatenate([v_lat, v_cx], axis=2).reshape(bhm, s + cl, MLA_V)
            qt = qk_heads(lat(q_m), s, True).reshape(bhm, 1, dqk, s)
            om = _flash(qt, k_all, v_all, 4 * Q_SUB)
            om = om.reshape(bsz, MLA_HEADS, MLA_V, s).transpose(0, 3, 1, 2).reshape(nl, MLA_HEADS * MLA_V)
            if need_ctx:
                qt_c = qk_heads(cx(q_m), cl, True).reshape(bhm, 1, dqk, cl)
                om_c = _flash(qt_c, k_cx.reshape(bhm, cl, dqk), v_cx.reshape(bhm, cl, MLA_V), Q_SUB)
                om_c = om_c.reshape(bsz, MLA_HEADS, MLA_V, cl).transpose(0, 3, 1, 2).reshape(bsz * cl, MLA_HEADS * MLA_V)
                a2 = jnp.concatenate([om, om_c], axis=0)
            else:
                a1 = a1[:nl]
                a2 = om
            w_out = w_out_odd[i].astype(BF16)

        n_rows = n_all if need_ctx else nl
        w_r_hi32 = lax.reduce_precision(w_router[l].astype(F32), exponent_bits=8, mantissa_bits=7)
        w_r_hi = w_r_hi32.astype(BF16)
        w_r_lo = (w_router[l] - w_r_hi32).astype(BF16)
        w_r = (jnp.zeros((d, 2 * LANES), BF16).at[:, :N_EXPERTS].set(w_r_hi)
               .at[:, LANES:LANES + N_EXPERTS].set(w_r_lo))
        b_r = jnp.full((1, LANES), NEG, F32).at[0, :N_EXPERTS].set(b_router[l])
        x_new, m, top_e, gates = _out_proj(a1, a2, x_all, w_out, mod_l, norm2_g[l], w_r, b_r, n_rows, tpb, bsz)
        x_all = _moe(m, top_e, gates, x_new, mod_l, w_exp1, b_exp1[l], w_exp2, b_exp2[l], l, tpb, bsz)
    return x_all[:nl].reshape(bsz, s, d)
```
